```python
import math
import jax, jax.numpy as jnp
from jax import lax
import numpy as np

D_MODEL = 4096
BATCH = 32
SEQ = 256
DEPTH = 2
DEC_BATCH = 4
DEC_SEQ = 1024
PAST_LEN = 512

GRID_W = 64
MIX_W = 1024
N_BRANCH = 4
S5_GROUP = 16
S5_GROUPS = MIX_W // S5_GROUP
S5_STATE = 64
GLA_HEADS = 4
GLA_DK_TOTAL = MIX_W // 2
GLA_DK = GLA_DK_TOTAL // GLA_HEADS
GLA_DV = MIX_W // GLA_HEADS
GLA_RANK = 16
GLA_TAU = 16.0
ML_HEADS = 4
ML_DH = MIX_W // ML_HEADS
RET_HEADS = 4
RET_DH = MIX_W // RET_HEADS
ROPE_BASE = 10000.0
CHUNK = 64
N_EXPERTS = 16
N_EXPERT_GROUPS = 4
EXPERTS_PER_GROUP = N_EXPERTS // N_EXPERT_GROUPS
TOP_K = 2
D_EXPERT = 1024
N_MOD = 6
EPS = 1e-6

PROJ_SPLITS = (
    ('s5_u', MIX_W),
    ('gla_q', GLA_DK_TOTAL), ('gla_k', GLA_DK_TOTAL), ('gla_v', MIX_W), ('gla_g', MIX_W), ('gla_r', GLA_RANK),
    ('ml_q', MIX_W), ('ml_k', MIX_W), ('ml_v', MIX_W), ('ml_o', MIX_W), ('ml_if', 4 * ML_HEADS),
    ('ret_q', MIX_W), ('ret_k', MIX_W), ('ret_v', MIX_W), ('ret_g', MIX_W),
    ('merge', N_BRANCH * D_MODEL),
)
PROJ_W = sum(w for _, w in PROJ_SPLITS)

kernel_name = 'bidir_ssm_gla_mlstm_retention_moe_dit_step'


def rmsnorm(x, g):
    xf = x.astype(jnp.float32)
    y = xf * lax.rsqrt(jnp.mean(xf * xf, axis=-1, keepdims=True) + EPS)
    return y.astype(x.dtype) * g


def head_rmsnorm(o, g):
    b, h, t, d = o.shape
    of = o.astype(jnp.float32)
    of = of * lax.rsqrt(jnp.mean(of * of, axis=-1, keepdims=True) + EPS)
    return jnp.transpose(of, (0, 2, 1, 3)).reshape(b, t, h * d) * g


def split_proj(z):
    parts = {}
    off = 0
    for name, w in PROJ_SPLITS:
        parts[name] = z[..., off:off + w]
        off += w
    return parts


def to_heads(x, n_heads):
    b, t, w = x.shape
    return jnp.transpose(x.reshape(b, t, n_heads, w // n_heads), (0, 2, 1, 3))


def flip_t(x):
    return jnp.flip(x, axis=2)


def _to_chunks(x):
    b, h, t = x.shape[:3]
    return jnp.moveaxis(x.reshape((b, h, t // CHUNK, CHUNK) + x.shape[3:]), 2, 0)


def _from_chunks(x):
    n, b, h, l = x.shape[:4]
    return jnp.moveaxis(x, 0, 2).reshape((b, h, n * l) + x.shape[4:])


def axial_rope(t):
    rows = t // GRID_W
    row = jnp.repeat(jnp.arange(rows, dtype=jnp.float32), GRID_W)
    col = (jnp.arange(rows * GRID_W) % GRID_W).astype(jnp.float32)
    n_freq = RET_DH // 4
    inv = ROPE_BASE ** (-jnp.arange(n_freq, dtype=jnp.float32) / n_freq)
    ang = jnp.concatenate([row[:, None] * inv, col[:, None] * inv], axis=-1)
    return jnp.cos(ang), jnp.sin(ang)


def apply_rope(x, cos, sin):
    half = x.shape[-1] // 2
    x1, x2 = x[..., :half], x[..., half:]
    return jnp.concatenate([x1 * cos - x2 * sin, x1 * sin + x2 * cos], axis=-1)


def _complex_affine_combine(e1, e2):
    a1r, a1i, b1r, b1i = e1
    a2r, a2i, b2r, b2i = e2
    return (a1r * a2r - a1i * a2i, a1r * a2i + a1i * a2r,
            a2r * b1r - a2i * b1i + b2r, a2r * b1i + a2i * b1r + b2i)


def s5_scan(u, lam_re, lam_im, log_step, b_re, b_im, c_re, c_im, x0_re, x0_im):
    f32 = jnp.float32
    lam_re, lam_im, b_re, b_im, c_re, c_im = (a.astype(f32) for a in (lam_re, lam_im, b_re, b_im, c_re, c_im))
    step = jnp.exp(log_step.astype(f32))[:, None]
    mag = jnp.exp(lam_re * step)
    a_re = mag * jnp.cos(lam_im * step)
    a_im = mag * jnp.sin(lam_im * step)
    den = lam_re * lam_re + lam_im * lam_im
    z_re = ((a_re - 1.0) * lam_re + a_im * lam_im) / den
    z_im = (a_im * lam_re - (a_re - 1.0) * lam_im) / den
    bb_re = z_re[..., None] * b_re - z_im[..., None] * b_im
    bb_im = z_re[..., None] * b_im + z_im[..., None] * b_re
    bu_re = jnp.einsum('btgc,gpc->tbgp', u, bb_re)
    bu_im = jnp.einsum('btgc,gpc->tbgp', u, bb_im)
    x0_re = x0_re.astype(f32)
    x0_im = x0_im.astype(f32)
    bu_re = bu_re.at[0].add(a_re * x0_re - a_im * x0_im)
    bu_im = bu_im.at[0].add(a_re * x0_im + a_im * x0_re)
    a_re_t = jnp.broadcast_to(a_re, bu_re.shape)
    a_im_t = jnp.broadcast_to(a_im, bu_im.shape)
    _, _, x_re, x_im = lax.associative_scan(_complex_affine_combine, (a_re_t, a_im_t, bu_re, bu_im), axis=0)
    y = jnp.einsum('tbgp,gcp->btgc', x_re, c_re) - jnp.einsum('tbgp,gcp->btgc', x_im, c_im)
    return y, x_re[-1], x_im[-1]


def chunked_decay_attn(q, k, v, log_a, s0):
    causal = jnp.tril(jnp.ones((CHUNK, CHUNK), dtype=bool))

    def step(s, inp):
        qc, kc, vc, lac = inp
        cum = jnp.cumsum(lac, axis=2)
        last = cum[:, :, -1:]
        q_dec = qc * jnp.exp(cum)
        scores = jnp.einsum('bhtd,bhsd->bhts', q_dec, kc * jnp.exp(-cum))
        scores = jnp.where(causal, scores, 0.0)
        out = jnp.einsum('bhtd,bhde->bhte', q_dec, s) + jnp.einsum('bhts,bhse->bhte', scores, vc)
        s_new = jnp.swapaxes(jnp.exp(last), 2, 3) * s + jnp.einsum('bhsd,bhse->bhde', kc * jnp.exp(last - cum), vc)
        return s_new, out

    s_fin, outs = lax.scan(step, s0.astype(jnp.float32), (_to_chunks(q), _to_chunks(k), _to_chunks(v), _to_chunks(log_a)))
    return _from_chunks(outs), s_fin


def chunked_mlstm(q, k, v, i_pre, log_f, c0, n0, m0):
    causal = jnp.tril(jnp.ones((CHUNK, CHUNK), dtype=bool))

    def step(carry, inp):
        c, n, m = carry
        qc, kc, vc, ic, fc = inp
        fcum = jnp.cumsum(fc, axis=-1)
        g = fcum + m[..., None]
        dmat = fcum[..., :, None] - fcum[..., None, :] + ic[..., None, :]
        dmat = jnp.where(causal, dmat, -jnp.inf)
        m_t = jnp.maximum(g, jnp.max(dmat, axis=-1))
        w_state = jnp.exp(g - m_t)
        qk = jnp.einsum('bhtd,bhsd->bhts', qc, kc) * jnp.exp(dmat - m_t[..., None])
        num = w_state[..., None] * jnp.einsum('bhtd,bhde->bhte', qc, c) + jnp.einsum('bhts,bhse->bhte', qk, vc)
        den = w_state * jnp.einsum('bhtd,bhd->bht', qc, n) + jnp.sum(qk, axis=-1)
        h = num / jnp.maximum(jnp.abs(den), jnp.exp(-m_t))[..., None]
        f_last = fcum[..., -1]
        src = f_last[..., None] - fcum + ic
        m_new = jnp.maximum(f_last + m, jnp.max(src, axis=-1))
        w_keep = jnp.exp(f_last + m - m_new)
        kw = kc * jnp.exp(src - m_new[..., None])[..., None]
        c_new = w_keep[..., None, None] * c + jnp.einsum('bhsd,bhse->bhde', kw, vc)
        n_new = w_keep[..., None] * n + jnp.sum(kw, axis=2)
        return (c_new, n_new, m_new), h

    f32 = jnp.float32
    (c_f, n_f, m_f), hs = lax.scan(
        step, (c0.astype(f32), n0.astype(f32), m0.astype(f32)),
        (_to_chunks(q), _to_chunks(k), _to_chunks(v), _to_chunks(i_pre), _to_chunks(log_f)))
    return _from_chunks(hs), c_f, n_f, m_f


def token_mixers(h, lp, st, rope):
    b, t, _ = h.shape
    dt = h.dtype
    f32 = jnp.float32
    z = split_proj(h @ lp['w_in'])
    s5_re, s5_im, gla_s, ml_c, ml_n, ml_m, ret_s = st

    u = z['s5_u'].astype(f32)
    ug = u.reshape(b, t, S5_GROUPS, S5_GROUP)
    y_f, xr_f, xi_f = s5_scan(ug, lp['s5_lambda_re'][0], lp['s5_lambda_im'][0], lp['s5_log_step'][0],
                              lp['s5_b_re'], lp['s5_b_im'], lp['s5_c_re'], lp['s5_c_im'], s5_re[:, 0], s5_im[:, 0])
    y_b, xr_b, xi_b = s5_scan(ug[:, ::-1], lp['s5_lambda_re'][1], lp['s5_lambda_im'][1], lp['s5_log_step'][1],
                              lp['s5_b_re'], lp['s5_b_im'], lp['s5_c_re'], lp['s5_c_im'], s5_re[:, 1], s5_im[:, 1])
    y = (y_f + y_b[:, ::-1]).reshape(b, t, MIX_W) + lp['s5_d'] * u
    y = jax.nn.gelu(y)
    y_s5 = y * jax.nn.sigmoid(y @ lp['s5_w_glu'])

    q = to_heads(z['gla_q'], GLA_HEADS).astype(f32) * GLA_DK ** -0.5
    k = to_heads(z['gla_k'], GLA_HEADS).astype(f32)
    v = to_heads(z['gla_v'], GLA_HEADS).astype(f32)
    r = z['gla_r'].astype(f32)
    la_f = to_heads(jax.nn.log_sigmoid(r @ lp['gla_w_a'][0] + lp['gla_b_a'][0]) / GLA_TAU, GLA_HEADS)
    la_b = to_heads(jax.nn.log_sigmoid(r @ lp['gla_w_a'][1] + lp['gla_b_a'][1]) / GLA_TAU, GLA_HEADS)
    o_f, sg_f = chunked_decay_attn(q, k, v, la_f, gla_s[:, 0])
    o_b, sg_b = chunked_decay_attn(flip_t(q), flip_t(k), flip_t(v), flip_t(la_b), gla_s[:, 1])
    y_gla = head_rmsnorm(o_f + flip_t(o_b), lp['gla_norm_g']) * jax.nn.silu(z['gla_g'].astype(f32))

    q = to_heads(z['ml_q'], ML_HEADS).astype(f32)
    k = to_heads(z['ml_k'], ML_HEADS).astype(f32) * ML_DH ** -0.5
    v = to_heads(z['ml_v'], ML_HEADS).astype(f32)
    gates = jnp.transpose(z['ml_if'].astype(f32).reshape(b, t, 2, 2, ML_HEADS), (2, 3, 0, 4, 1))
    i_pre = gates[:, 0] + lp['ml_i_bias'].astype(f32)[:, None, :, None]
    log_f = jax.nn.log_sigmoid(gates[:, 1] + lp['ml_f_bias'].astype(f32)[:, None, :, None])
    h_f, c_f, n_f, m_f = chunked_mlstm(q, k, v, i_pre[0], log_f[0], ml_c[:, 0], ml_n[:, 0], ml_m[:, 0])
    h_b, c_b, n_b, m_b = chunked_mlstm(flip_t(q), flip_t(k), flip_t(v), flip_t(i_pre[1]), flip_t(log_f[1]),
                                       ml_c[:, 1], ml_n[:, 1], ml_m[:, 1])
    y_ml = head_rmsnorm(h_f + flip_t(h_b), lp['ml_norm_g']) * jax.nn.sigmoid(z['ml_o'].astype(f32))

    q = to_heads(z['ret_q'], RET_HEADS).astype(f32)
    k = to_heads(z['ret_k'], RET_HEADS).astype(f32) * RET_DH ** -0.5
    if rope is not None:
        q = apply_rope(q, rope[0], rope[1])
        k = apply_rope(k, rope[0], rope[1])
    v = to_heads(z['ret_v'], RET_HEADS).astype(f32)
    log_gamma = jax.nn.log_sigmoid(lp['ret_decay_logit'].astype(f32))
    lg_f = jnp.broadcast_to(log_gamma[0][None, :, None, None], (b, RET_HEADS, t, 1))
    lg_b = jnp.broadcast_to(log_gamma[1][None, :, None, None], (b, RET_HEADS, t, 1))
    r_f, sr_f = chunked_decay_attn(q, k, v, lg_f, ret_s[:, 0])
    r_b, sr_b = chunked_decay_attn(flip_t(q), flip_t(k), flip_t(v), lg_b, ret_s[:, 1])
    y_ret = head_rmsnorm(r_f + flip_t(r_b), lp['ret_norm_g']) * jax.nn.silu(z['ret_g'].astype(f32))

    mgates = jax.nn.sigmoid(z['merge'].astype(f32)).reshape(b, t, N_BRANCH, D_MODEL)
    merged = jnp.zeros((b, t, D_MODEL), f32)
    for i, yb in enumerate((y_s5, y_gla, y_ml, y_ret)):
        merged = merged + mgates[:, :, i] * (yb.astype(dt) @ lp['w_branch'][i])
    out = merged.astype(dt) @ lp['w_out']
    new_st = (jnp.stack([xr_f, xr_b], axis=1), jnp.stack([xi_f, xi_b], axis=1), jnp.stack([sg_f, sg_b], axis=1),
              jnp.stack([c_f, c_b], axis=1), jnp.stack([n_f, n_b], axis=1), jnp.stack([m_f, m_b], axis=1),
              jnp.stack([sr_f, sr_b], axis=1))
    return out, new_st


def routed_moe(h, w_router, router_bias, w_gate, w_up, w_down):
    b, t, d = h.shape
    xt = h.reshape(b * t, d)
    scores = jax.nn.sigmoid((xt @ w_router).astype(jnp.float32))
    sel = (scores + router_bias.astype(jnp.float32)).reshape(-1, N_EXPERT_GROUPS, EXPERTS_PER_GROUP)
    group_score = jnp.sum(lax.top_k(sel, TOP_K)[0], axis=-1)
    g_best = jnp.argmax(group_score, axis=-1)
    in_group = jnp.take_along_axis(sel, g_best[:, None, None], axis=1)[:, 0]
    _, local = lax.top_k(in_group, TOP_K)
    idx = g_best[:, None] * EXPERTS_PER_GROUP + local
    w_sel = jnp.take_along_axis(scores, idx, axis=1)
    w_sel = w_sel / jnp.sum(w_sel, axis=-1, keepdims=True)
    gate = jnp.sum(jax.nn.one_hot(idx, N_EXPERTS, dtype=jnp.float32) * w_sel[..., None], axis=1)
    out = jnp.zeros((b * t, d), jnp.float32)
    for e in range(N_EXPERTS):
        he = jax.nn.silu(xt @ w_gate[e]) * (xt @ w_up[e])
        out = out + gate[:, e:e + 1] * (he @ w_down[e])
    return out.astype(h.dtype).reshape(b, t, d)


def trunk_layer(x, mod, lp, states, rope, w_router, router_bias):
    shift1, scale1, gate1, shift2, scale2, gate2 = jnp.split(mod, N_MOD, axis=-1)
    h = rmsnorm(x, lp['norm1']) * (1.0 + scale1) + shift1
    mix, new_states = token_mixers(h, lp, states, rope)
    x = x + gate1 * mix
    h = rmsnorm(x, lp['norm2']) * (1.0 + scale2) + shift2
    x = x + gate2 * routed_moe(h, w_router, router_bias, lp['w_exp_gate'], lp['w_exp_up'], lp['w_exp_down'])
    return x, new_states


def zero_states(bsz):
    f32 = jnp.float32
    return (jnp.zeros((bsz, 2, S5_GROUPS, S5_STATE), f32), jnp.zeros((bsz, 2, S5_GROUPS, S5_STATE), f32),
            jnp.zeros((bsz, 2, GLA_HEADS, GLA_DK, GLA_DV), f32), jnp.zeros((bsz, 2, ML_HEADS, ML_DH, ML_DH), f32),
            jnp.zeros((bsz, 2, ML_HEADS, ML_DH), f32), jnp.zeros((bsz, 2, ML_HEADS), f32),
            jnp.zeros((bsz, 2, RET_HEADS, RET_DH, RET_DH), f32))


def setup_inputs(seed: int = 0) -> dict:
    key = jax.random.key(seed)
    ks = iter(jax.random.split(key, 64))
    f32 = jnp.float32

    def nrm(shape, scale):
        return jax.random.normal(next(ks), shape, f32) * scale

    inp = {}
    inp['x_prompt'] = nrm((BATCH, SEQ, D_MODEL), 1.0)
    inp['x_sample'] = nrm((DEC_BATCH, DEC_SEQ, D_MODEL), 1.0)
    inp['c'] = nrm((DEC_BATCH, D_MODEL), 1.0)
    inp['c_ctx'] = nrm((D_MODEL,), 1.0)
    inp['state_s5_re'] = nrm((DEC_BATCH, DEPTH, 2, S5_GROUPS, S5_STATE), 0.1)
    inp['state_s5_im'] = nrm((DEC_BATCH, DEPTH, 2, S5_GROUPS, S5_STATE), 0.1)
    inp['state_gla'] = nrm((DEC_BATCH, DEPTH, 2, GLA_HEADS, GLA_DK, GLA_DV), 0.1)
    inp['state_ml_c'] = nrm((DEC_BATCH, DEPTH, 2, ML_HEADS, ML_DH, ML_DH), 0.1)
    inp['state_ml_n'] = nrm((DEC_BATCH, DEPTH, 2, ML_HEADS, ML_DH), 0.1)
    inp['state_ml_m'] = nrm((DEC_BATCH, DEPTH, 2, ML_HEADS), 0.5)
    inp['state_ret'] = nrm((DEC_BATCH, DEPTH, 2, RET_HEADS, RET_DH, RET_DH), 0.1)
    inp['w_ada'] = nrm((DEPTH, D_MODEL, N_MOD * D_MODEL), 0.5 * D_MODEL ** -0.5)
    inp['b_ada'] = nrm((DEPTH, N_MOD * D_MODEL), 0.01)
    inp['norm1_g'] = 1.0 + nrm((DEPTH, D_MODEL), 0.01)
    inp['norm2_g'] = 1.0 + nrm((DEPTH, D_MODEL), 0.01)
    inp['w_in'] = nrm((DEPTH, D_MODEL, PROJ_W), D_MODEL ** -0.5)
    n_idx = jnp.arange(S5_STATE, dtype=f32)
    inp['s5_lambda_re'] = -0.5 + nrm((DEPTH, 2, S5_GROUPS, S5_STATE), 0.01)
    inp['s5_lambda_im'] = math.pi * n_idx + nrm((DEPTH, 2, S5_GROUPS, S5_STATE), 0.01)
    inp['s5_log_step'] = jax.random.uniform(next(ks), (DEPTH, 2, S5_GROUPS), f32, math.log(1e-3), math.log(1e-1))
    inp['s5_b_re'] = nrm((DEPTH, S5_GROUPS, S5_STATE, S5_GROUP), (2 * S5_GROUP) ** -0.5)
    inp['s5_b_im'] = nrm((DEPTH, S5_GROUPS, S5_STATE, S5_GROUP), (2 * S5_GROUP) ** -0.5)
    inp['s5_c_re'] = nrm((DEPTH, S5_GROUPS, S5_GROUP, S5_STATE), S5_STATE ** -0.5)
    inp['s5_c_im'] = nrm((DEPTH, S5_GROUPS, S5_GROUP, S5_STATE), S5_STATE ** -0.5)
    inp['s5_d'] = nrm((DEPTH, MIX_W), 1.0)
    inp['s5_w_glu'] = nrm((DEPTH, MIX_W, MIX_W), MIX_W ** -0.5)
    inp['gla_w_a'] = nrm((DEPTH, 2, GLA_RANK, GLA_DK_TOTAL), GLA_RANK ** -0.5)
    inp['gla_b_a'] = nrm((DEPTH, 2, GLA_DK_TOTAL), 0.01)
    inp['gla_norm_g'] = 1.0 + nrm((DEPTH, MIX_W), 0.01)
    inp['ml_i_bias'] = nrm((DEPTH, 2, ML_HEADS), 0.1)
    inp['ml_f_bias'] = jnp.linspace(3.0, 6.0, ML_HEADS, dtype=f32) + nrm((DEPTH, 2, ML_HEADS), 0.01)
    inp['ml_norm_g'] = 1.0 + nrm((DEPTH, MIX_W), 0.01)
    gamma0 = 1.0 - jnp.power(2.0, -5.0 - jnp.arange(RET_HEADS, dtype=f32))
    inp['ret_decay_logit'] = jnp.log(gamma0 / (1.0 - gamma0)) + nrm((DEPTH, 2, RET_HEADS), 0.01)
    inp['ret_norm_g'] = 1.0 + nrm((DEPTH, MIX_W), 0.01)
    inp['w_branch'] = nrm((DEPTH, N_BRANCH, MIX_W, D_MODEL), MIX_W ** -0.5)
    inp['w_out'] = nrm((DEPTH, D_MODEL, D_MODEL), D_MODEL ** -0.5)
    inp['w_router'] = nrm((D_MODEL, N_EXPERTS), D_MODEL ** -0.5)
    inp['router_bias'] = nrm((N_EXPERTS,), 0.01)
    inp['w_exp_gate'] = nrm((DEPTH, N_EXPERTS, D_MODEL, D_EXPERT), D_MODEL ** -0.5)
    inp['w_exp_up'] = nrm((DEPTH, N_EXPERTS, D_MODEL, D_EXPERT), D_MODEL ** -0.5)
    inp['w_exp_down'] = nrm((DEPTH, N_EXPERTS, D_EXPERT, D_MODEL), D_EXPERT ** -0.5)
    inp['final_g'] = 1.0 + nrm((D_MODEL,), 0.01)
    return inp


def reference(x_prompt, x_sample, c, c_ctx,
              state_s5_re, state_s5_im, state_gla, state_ml_c, state_ml_n, state_ml_m, state_ret,
              w_ada, b_ada, norm1_g, norm2_g, w_in,
              s5_lambda_re, s5_lambda_im, s5_log_step, s5_b_re, s5_b_im, s5_c_re, s5_c_im, s5_d, s5_w_glu,
              gla_w_a, gla_b_a, gla_norm_g,
              ml_i_bias, ml_f_bias, ml_norm_g,
              ret_decay_logit, ret_norm_g,
              w_branch, w_out,
              w_router, router_bias, w_exp_gate, w_exp_up, w_exp_down,
              final_g):
    caches = (state_s5_re, state_s5_im, state_gla, state_ml_c, state_ml_n, state_ml_m, state_ret)
    rope = axial_rope(x_sample.shape[1])
    x_ctx = x_prompt
    x_lat = x_sample
    ctx_states = []
    for l in range(DEPTH):
        lp = {'norm1': norm1_g[l], 'norm2': norm2_g[l], 'w_in': w_in[l],
              's5_lambda_re': s5_lambda_re[l], 's5_lambda_im': s5_lambda_im[l], 's5_log_step': s5_log_step[l],
              's5_b_re': s5_b_re[l], 's5_b_im': s5_b_im[l], 's5_c_re': s5_c_re[l], 's5_c_im': s5_c_im[l],
              's5_d': s5_d[l], 's5_w_glu': s5_w_glu[l],
              'gla_w_a': gla_w_a[l], 'gla_b_a': gla_b_a[l], 'gla_norm_g': gla_norm_g[l],
              'ml_i_bias': ml_i_bias[l], 'ml_f_bias': ml_f_bias[l], 'ml_norm_g': ml_norm_g[l],
              'ret_decay_logit': ret_decay_logit[l], 'ret_norm_g': ret_norm_g[l],
              'w_branch': w_branch[l], 'w_out': w_out[l],
              'w_exp_gate': w_exp_gate[l], 'w_exp_up': w_exp_up[l], 'w_exp_down': w_exp_down[l]}
        mod_ctx = (jax.nn.silu(c_ctx) @ w_ada[l] + b_ada[l])[None, None, :]
        mod_lat = (jax.nn.silu(c) @ w_ada[l] + b_ada[l])[:, None, :]
        x_ctx, st = trunk_layer(x_ctx, mod_ctx, lp, zero_states(x_prompt.shape[0]), None, w_router, router_bias)
        ctx_states.append(st)
        lat_init = tuple(s[:, l] for s in caches)
        x_lat, _ = trunk_layer(x_lat, mod_lat, lp, lat_init, rope, w_router, router_bias)
    y_prompt = rmsnorm(x_ctx, final_g)
    y_sample = rmsnorm(x_lat, final_g)
    new_s5_re = jnp.stack([s[0] for s in ctx_states], axis=1)
    new_s5_im = jnp.stack([s[1] for s in ctx_states], axis=1)
    new_gla = jnp.stack([s[2] for s in ctx_states], axis=1)
    new_ml_c = jnp.stack([s[3] for s in ctx_states], axis=1)
    new_ml_n = jnp.stack([s[4] for s in ctx_states], axis=1)
    new_ml_m = jnp.stack([s[5] for s in ctx_states], axis=1)
    new_ret = jnp.stack([s[6] for s in ctx_states], axis=1)
    return (y_prompt, y_sample, new_s5_re, new_s5_im, new_gla, new_ml_c, new_ml_n, new_ml_m, new_ret)
```

```python
import functools
import math

import jax
import jax.numpy as jnp
from jax import lax
from jax.experimental import pallas as pl
from jax.experimental.pallas import tpu as pltpu

F32 = jnp.float32
BF16 = jnp.bfloat16
HIGHEST = lax.Precision.HIGHEST

EPS = 1e-6
GRID_W = 64
ROPE_BASE = 10000.0
GLA_TAU = 16.0
N_EXPERT_GROUPS = 4
N_MOD = 6

LANES = 128
SUBLANES = 8
S5_SLAB = LANES
GLA_CHUNK = 64
SEQ_CHUNK = 256
VMEM_LIMIT_BYTES = 56 * 1024 * 1024
ROUTE_TILE = 256


def _cparams(sem):
    return pltpu.CompilerParams(dimension_semantics=sem, vmem_limit_bytes=VMEM_LIMIT_BYTES)


def _pick(n, cands):
    for c in cands:
        if c <= n and n % c == 0:
            return c
    return n


def _nt(a, b):
    return lax.dot_general(a, b, (((1,), (1,)), ((), ())), preferred_element_type=F32)


def _tn(a, b):
    return lax.dot_general(a, b, (((0,), (0,)), ((), ())), preferred_element_type=F32)


def _dot(a, b):
    return jnp.dot(a, b, preferred_element_type=F32)


def _log_sigmoid(x):
    return jnp.minimum(x, 0.0) - jnp.log1p(jnp.exp(-jnp.abs(x)))


def _silu(x):
    return x * jax.nn.sigmoid(x)


def _iota2(shape, dim):
    return lax.broadcasted_iota(jnp.int32, shape, dim)


def _ada_kernel(c_ref, w_ref, b_ref, o_ref):
    s = _silu(c_ref[...]).astype(BF16)
    o_ref[0] = _dot(s, w_ref[0].astype(BF16)) + b_ref[0]


def _ada(c8, w_ada, b_ada):
    depth, d, n = w_ada.shape
    tn = _pick(n, (512, 256, 128))
    return pl.pallas_call(
        _ada_kernel,
        grid=(depth, n // tn),
        in_specs=[
            pl.BlockSpec((SUBLANES, d), lambda l, j: (0, 0)),
            pl.BlockSpec((1, d, tn), lambda l, j: (l, 0, j)),
            pl.BlockSpec((1, 1, tn), lambda l, j: (l, 0, j)),
        ],
        out_specs=pl.BlockSpec((1, SUBLANES, tn), lambda l, j: (l, 0, j)),
        out_shape=jax.ShapeDtypeStruct((depth, SUBLANES, n), F32),
        compiler_params=_cparams(("arbitrary", "arbitrary")),
        name="ada_mod",
    )(c8, w_ada, b_ada.reshape(depth, 1, n))


def _mod_spec(d, tm, geom, comp):
    n_ctx, dec_seq, ctx_row = geom["n_ctx"], geom["dec_seq"], geom["ctx_row"]

    def index_map(i, *_):
        start = i * tm
        row = jnp.where(start < n_ctx, ctx_row, (start - n_ctx) // dec_seq)
        return (row, comp, 0, 0)

    return pl.BlockSpec((1, 1, 1, d), index_map)


def _norm_kernel(x_ref, g_ref, sh_ref, sc_ref, o_ref):
    x = x_ref[...]
    y = x * lax.rsqrt(jnp.mean(x * x, axis=-1, keepdims=True) + EPS)
    h = (y * g_ref[...]) * (1.0 + sc_ref[0, 0]) + sh_ref[0, 0]
    o_ref[...] = h.astype(o_ref.dtype)


def _norm_mod(x, g, mod4, comp_shift, comp_scale, geom, out_dtype):
    n, d = x.shape
    tm = 256
    return pl.pallas_call(
        _norm_kernel,
        grid=(n // tm,),
        in_specs=[
            pl.BlockSpec((tm, d), lambda i: (i, 0)),
            pl.BlockSpec((1, d), lambda i: (0, 0)),
            _mod_spec(d, tm, geom, comp_shift),
            _mod_spec(d, tm, geom, comp_scale),
        ],
        out_specs=pl.BlockSpec((tm, d), lambda i: (i, 0)),
        out_shape=jax.ShapeDtypeStruct((n, d), out_dtype),
        compiler_params=_cparams(("arbitrary",)),
        name="norm_mod",
    )(x, g.reshape(1, d), mod4, mod4)


def _mm_kernel(x_ref, w_ref, o_ref):
    o_ref[...] = _dot(x_ref[...], w_ref[...]).astype(o_ref.dtype)


def _matmul(x, w, out_dtype, tm, tn, name):
    m, k = x.shape
    n = w.shape[1]
    return pl.pallas_call(
        _mm_kernel,
        grid=(m // tm, n // tn),
        in_specs=[
            pl.BlockSpec((tm, k), lambda i, j: (i, 0)),
            pl.BlockSpec((k, tn), lambda i, j: (0, j)),
        ],
        out_specs=pl.BlockSpec((tm, tn), lambda i, j: (i, j)),
        out_shape=jax.ShapeDtypeStruct((m, n), out_dtype),
        compiler_params=_cparams(("arbitrary", "arbitrary")),
        name=name,
    )(x, w)


def _s5_kernel(*refs, t_len, bp, tc, has_init, want_final):
    it = iter(refs)
    u_ref, bm_ref, cm_ref, a_ref, d_ref = (next(it) for _ in range(5))
    x0_ref = next(it) if has_init else None
    y_ref = next(it)
    xf_ref = next(it) if want_final else None
    bu_ref, yacc_ref, st_ref = next(it), next(it), next(it)

    nc = t_len // tc
    rc = tc * bp
    half = bu_ref.shape[1] // 2
    for d in (0, 1):
        if has_init:
            st_ref[...] = x0_ref[d, 0]
        else:
            st_ref[...] = jnp.zeros_like(st_ref)
        bm = bm_ref[d, 0]
        ar = a_ref[d, 0, :, :half]
        ai = a_ref[d, 0, :, half:]

        def chunk_body(c, carry, d=d, bm=bm, ar=ar, ai=ai):
            cc = c if d == 0 else nc - 1 - c
            r0 = pl.multiple_of(cc * rc, rc)
            bu_ref[...] = _dot(u_ref[pl.ds(r0, rc), :], bm)
            for rt in range(bp // SUBLANES):
                rows = slice(rt * SUBLANES, (rt + 1) * SUBLANES)

                def step(t, s, rt=rt):
                    sr, si = s
                    tt = t if d == 0 else tc - 1 - t
                    row = pl.multiple_of(tt * bp + rt * SUBLANES, SUBLANES)
                    b = bu_ref[pl.ds(row, SUBLANES), :]
                    xr = ar * sr - ai * si + b[:, :half]
                    xi = ar * si + ai * sr + b[:, half:]
                    bu_ref[pl.ds(row, SUBLANES), :] = jnp.concatenate([xr, xi], axis=-1)
                    return xr, xi

                sr, si = lax.fori_loop(0, tc, step, (st_ref[rows, :half], st_ref[rows, half:]))
                st_ref[rows, :] = jnp.concatenate([sr, si], axis=-1)
            y = _dot(bu_ref[...].astype(BF16), cm_ref[0])
            if d == 0:
                yacc_ref[pl.ds(r0, rc), :] = y
            else:
                yacc_ref[pl.ds(r0, rc), :] += y
            return carry

        lax.fori_loop(0, nc, chunk_body, 0)
        if want_final:
            xf_ref[d, 0] = st_ref[...]
    y = yacc_ref[...] + d_ref[...] * u_ref[...].astype(F32)
    y_ref[...] = jax.nn.gelu(y).astype(y_ref.dtype)


def _s5_mixer(u_tm, bm, cm, a8, dvec, x0, t_len, bp, want_final):
    rows, mix = u_tm.shape
    n_slab = mix // S5_SLAB
    two_half = bm.shape[-1]
    tc = _pick(t_len, tuple(max(1, 1024 // bp) >> s for s in range(6)))
    has_init = x0 is not None
    in_specs = [
        pl.BlockSpec((rows, S5_SLAB), lambda s: (0, s)),
        pl.BlockSpec((2, 1, S5_SLAB, two_half), lambda s: (0, s, 0, 0)),
        pl.BlockSpec((1, two_half, S5_SLAB), lambda s: (s, 0, 0)),
        pl.BlockSpec((2, 1, SUBLANES, two_half), lambda s: (0, s, 0, 0)),
        pl.BlockSpec((1, S5_SLAB), lambda s: (0, s)),
    ]
    args = [u_tm, bm, cm, a8, dvec]
    if has_init:
        in_specs.append(pl.BlockSpec((2, 1, bp, two_half), lambda s: (0, s, 0, 0)))
        args.append(x0)
    out_specs = [pl.BlockSpec((rows, S5_SLAB), lambda s: (0, s))]
    out_shape = [jax.ShapeDtypeStruct((rows, mix), BF16)]
    if want_final:
        out_specs.append(pl.BlockSpec((2, 1, bp, two_half), lambda s: (0, s, 0, 0)))
        out_shape.append(jax.ShapeDtypeStruct((2, n_slab, bp, two_half), F32))
    res = pl.pallas_call(
        functools.partial(_s5_kernel, t_len=t_len, bp=bp, tc=tc, has_init=has_init, want_final=want_final),
        grid=(n_slab,),
        in_specs=in_specs,
        out_specs=out_specs,
        out_shape=out_shape,
        scratch_shapes=[
            pltpu.VMEM((tc * bp, two_half), F32),
            pltpu.VMEM((rows, S5_SLAB), F32),
            pltpu.VMEM((bp, two_half), F32),
        ],
        compiler_params=_cparams(("arbitrary",)),
        name="s5_ctx" if want_final else "s5_lat",
    )(*args)
    return (res[0], res[1]) if want_final else (res[0], None)


def _glu_kernel(y_ref, w_ref, o_ref):
    y = y_ref[...]
    z = _dot(y, w_ref[...])
    o_ref[...] = (y.astype(F32) * jax.nn.sigmoid(z)).astype(o_ref.dtype)


def _glu(y, w):
    n, mix = y.shape
    tm = _pick(n, (512, 256, 128))
    return pl.pallas_call(
        _glu_kernel,
        grid=(n // tm,),
        in_specs=[pl.BlockSpec((tm, mix), lambda i: (i, 0)), pl.BlockSpec((mix, mix), lambda i: (0, 0))],
        out_specs=pl.BlockSpec((tm, mix), lambda i: (i, 0)),
        out_shape=jax.ShapeDtypeStruct((n, mix), BF16),
        compiler_params=_cparams(("arbitrary",)),
        name="s5_glu",
    )(y, w)


def _head_norm_gate(o, normg_ref, gate, act):
    of = o * lax.rsqrt(jnp.mean(o * o, axis=-1, keepdims=True) + EPS)
    return of * normg_ref[...] * act(gate)


def _seq_geometry(geom, ctx):
    if ctx:
        return geom["batch"], geom["seq"], 0
    return geom["dec_batch"], geom["dec_seq"], geom["n_ctx"] // geom["dec_seq"]


def _col_spec(t_len, width, col0, blk0):
    base = col0 // width
    return pl.BlockSpec((t_len, width), lambda b, h: (blk0 + b, base + h))


def _launch_mixer(body, name, geom, ctx, width, in_specs, args, extra_out_specs, extra_out_shapes, scratch, **kw):
    nb, t_len, _ = _seq_geometry(geom, ctx)
    heads = geom["mix"] // width
    out_specs = [pl.BlockSpec((t_len, width), lambda b, h: (b, h))] + list(extra_out_specs)
    out_shapes = [jax.ShapeDtypeStruct((nb * t_len, geom["mix"]), BF16)] + list(extra_out_shapes)
    return pl.pallas_call(
        functools.partial(body, **kw),
        grid=(nb, heads),
        in_specs=list(in_specs),
        out_specs=out_specs,
        out_shape=out_shapes,
        scratch_shapes=scratch,
        compiler_params=_cparams(("arbitrary", "arbitrary")),
        name=name,
    )(*args)


def _gla_kernel(*refs, t_len, dk, has_init, want_final):
    it = iter(refs)
    q_ref, k_ref, v_ref, g_ref, zg_ref = (next(it) for _ in range(5))
    wa_ref, wat_ref, ba_ref, bat_ref, normg_ref = (next(it) for _ in range(5))
    s0_ref = next(it) if has_init else None
    y_ref = next(it)
    sf_ref = next(it) if want_final else None
    o_ref, s_ref = next(it), next(it)

    cl = min(GLA_CHUNK, t_len)
    nc = t_len // cl
    scale = dk ** -0.5
    ti = _iota2((cl, cl), 0)
    si = _iota2((cl, cl), 1)
    for d in (0, 1):
        keep = (si <= ti) if d == 0 else (si >= ti)
        tri = keep.astype(F32)
        if has_init:
            s_ref[...] = s0_ref[0, d, 0]
        else:
            s_ref[...] = jnp.zeros_like(s_ref)
        wa = wa_ref[d, 0]
        wat = wat_ref[d, 0]
        ba = ba_ref[d, 0]
        bat = bat_ref[d, 0]

        def chunk_body(c, carry, d=d, keep=keep, tri=tri, wa=wa, wat=wat, ba=ba, bat=bat):
            cc = c if d == 0 else nc - 1 - c
            r0 = pl.multiple_of(cc * cl, cl)
            r = zg_ref[pl.ds(r0, cl), :]
            la = _log_sigmoid(jnp.dot(r, wa, precision=HIGHEST, preferred_element_type=F32) + ba) / GLA_TAU
            lat = _log_sigmoid(lax.dot_general(wat, r, (((1,), (1,)), ((), ())), precision=HIGHEST,
                                               preferred_element_type=F32) + bat) / GLA_TAU
            cum = jnp.dot(tri, la, precision=HIGHEST, preferred_element_type=F32)
            tot_row = jnp.sum(la, axis=0, keepdims=True)
            tot_col = jnp.sum(lat, axis=1, keepdims=True)
            q = q_ref[pl.ds(r0, cl), :].astype(F32) * scale
            k = k_ref[pl.ds(r0, cl), :].astype(F32)
            v = v_ref[pl.ds(r0, cl), :]
            qd = (q * jnp.exp(cum)).astype(BF16)
            kd = (k * jnp.exp(-cum)).astype(BF16)
            sc = jnp.where(keep, _nt(qd, kd), 0.0)
            o = _dot(qd, s_ref[...].astype(BF16)) + _dot(sc.astype(BF16), v)
            if d == 0:
                o_ref[pl.ds(r0, cl), :] = o
            else:
                o_ref[pl.ds(r0, cl), :] += o
            kl = (k * jnp.exp(tot_row - cum)).astype(BF16)
            s_ref[...] = jnp.exp(tot_col) * s_ref[...] + _tn(kl, v)
            return carry

        lax.fori_loop(0, nc, chunk_body, 0)
        if want_final:
            sf_ref[0, d, 0] = s_ref[...]
    y = _head_norm_gate(o_ref[...], normg_ref, g_ref[...].astype(F32), _silu)
    y_ref[...] = y.astype(y_ref.dtype)


def _gla_mixer(z, zg, gp, s0, geom, cols, ctx):
    nb, t_len, blk0 = _seq_geometry(geom, ctx)
    mix, heads = geom["mix"], geom["gla_heads"]
    dk, dv = mix // 2 // heads, mix // heads
    has_init, want_final = s0 is not None, ctx
    in_specs = [
        _col_spec(t_len, dk, cols["gla_q"], blk0),
        _col_spec(t_len, dk, cols["gla_k"], blk0),
        _col_spec(t_len, dv, cols["gla_v"], blk0),
        _col_spec(t_len, dv, cols["gla_g"], blk0),
        pl.BlockSpec((t_len, LANES), lambda b, h: (blk0 + b, 0)),
        pl.BlockSpec((2, 1, LANES, dk), lambda b, h: (0, h, 0, 0)),
        pl.BlockSpec((2, 1, dk, LANES), lambda b, h: (0, h, 0, 0)),
        pl.BlockSpec((2, 1, 1, dk), lambda b, h: (0, h, 0, 0)),
        pl.BlockSpec((2, 1, dk, 1), lambda b, h: (0, h, 0, 0)),
        pl.BlockSpec((1, dv), lambda b, h: (0, h)),
    ]
    args = [z, z, z, z, zg, gp["wa"], gp["wat"], gp["ba"], gp["bat"], gp["normg"]]
    if has_init:
        in_specs.append(pl.BlockSpec((1, 2, 1, dk, dv), lambda b, h: (b, 0, h, 0, 0)))
        args.append(s0)
    x_specs, x_shapes = [], []
    if want_final:
        x_specs.append(pl.BlockSpec((1, 2, 1, dk, dv), lambda b, h: (b, 0, h, 0, 0)))
        x_shapes.append(jax.ShapeDtypeStruct((nb, 2, heads, dk, dv), F32))
    res = _launch_mixer(_gla_kernel, "gla_ctx" if ctx else "gla_lat", geom, ctx, dv, in_specs, args,
                        x_specs, x_shapes, [pltpu.VMEM((t_len, dv), F32), pltpu.VMEM((dk, dv), F32)],
                        t_len=t_len, dk=dk, has_init=has_init, want_final=want_final)
    return (res[0], res[1]) if want_final else (res[0], None)


def _ret_kernel(*refs, t_len, dh, has_init, want_final, rope):
    it = iter(refs)
    lg_ref = next(it)
    q_ref, k_ref, v_ref, g_ref, normg_ref = (next(it) for _ in range(5))
    cos_ref, sin_ref = (next(it), next(it)) if rope else (None, None)
    s0_ref = next(it) if has_init else None
    y_ref = next(it)
    sf_ref = next(it) if want_final else None
    o_ref = next(it)

    h = pl.program_id(1)
    n_heads = lg_ref.shape[0] // 2
    lgf = lg_ref[h]
    lgb = lg_ref[n_heads + h]
    cl = min(SEQ_CHUNK, t_len)
    nc = t_len // cl
    scale = dh ** -0.5
    half = dh // 2

    q = q_ref[...].astype(F32)
    k = k_ref[...].astype(F32) * scale
    if rope:
        cos = cos_ref[...]
        sin = sin_ref[...]

        def rot(x):
            x1, x2 = x[:, :half], x[:, half:]
            return jnp.concatenate([x1 * cos - x2 * sin, x1 * sin + x2 * cos], axis=-1)

        q, k = rot(q), rot(k)
    qb = q.astype(BF16)
    kb = k.astype(BF16)

    dt = (_iota2((cl, cl), 0) - _iota2((cl, cl), 1)).astype(F32)
    decay = (jnp.where(dt >= 0, jnp.exp(lgf * jnp.maximum(dt, 0.0)), 0.0)
             + jnp.where(dt <= 0, jnp.exp(lgb * jnp.maximum(-dt, 0.0)), 0.0))
    tcol = _iota2((cl, 1), 0).astype(F32)

    def rows(c):
        return slice(c * cl, (c + 1) * cl)

    s_f = s0_ref[0, 0, 0] if has_init else None
    for c in range(nc):
        qc, kc, vc = qb[rows(c)], kb[rows(c)], v_ref[rows(c), :]
        o = _dot((_nt(qc, kc) * decay).astype(BF16), vc)
        if s_f is not None:
            o = o + _dot((q[rows(c)] * jnp.exp(lgf * (tcol + 1.0))).astype(BF16), s_f.astype(BF16))
        o_ref[rows(c), :] = o
        if c < nc - 1 or want_final:
            upd = _tn((k[rows(c)] * jnp.exp(lgf * (cl - 1.0 - tcol))).astype(BF16), vc)
            s_f = upd if s_f is None else jnp.exp(lgf * cl) * s_f + upd
    if want_final:
        sf_ref[0, 0, 0] = s_f
    s_b = s0_ref[0, 1, 0] if has_init else None
    for c in range(nc - 1, -1, -1):
        vc = v_ref[rows(c), :]
        if s_b is not None:
            o_ref[rows(c), :] += _dot((q[rows(c)] * jnp.exp(lgb * (cl - tcol))).astype(BF16), s_b.astype(BF16))
        if c > 0 or want_final:
            upd = _tn((k[rows(c)] * jnp.exp(lgb * tcol)).astype(BF16), vc)
            s_b = upd if s_b is None else jnp.exp(lgb * cl) * s_b + upd
    if want_final:
        sf_ref[0, 1, 0] = s_b
    y = _head_norm_gate(o_ref[...], normg_ref, g_ref[...].astype(F32), _silu)
    y_ref[...] = y.astype(y_ref.dtype)


def _ret_mixer(z, lg, normg, rope_tabs, s0, geom, cols, ctx):
    nb, t_len, blk0 = _seq_geometry(geom, ctx)
    mix, heads = geom["mix"], geom["ret_heads"]
    dh = mix // heads
    has_init, want_final, rope = s0 is not None, ctx, rope_tabs is not None
    in_specs = [
        pl.BlockSpec(memory_space=pltpu.SMEM),
        _col_spec(t_len, dh, cols["ret_q"], blk0),
        _col_spec(t_len, dh, cols["ret_k"], blk0),
        _col_spec(t_len, dh, cols["ret_v"], blk0),
        _col_spec(t_len, dh, cols["ret_g"], blk0),
        pl.BlockSpec((1, dh), lambda b, h: (0, h)),
    ]
    args = [lg, z, z, z, z, normg]
    if rope:
        in_specs += [pl.BlockSpec((t_len, dh // 2), lambda b, h: (0, 0))] * 2
        args += list(rope_tabs)
    if has_init:
        in_specs.append(pl.BlockSpec((1, 2, 1, dh, dh), lambda b, h: (b, 0, h, 0, 0)))
        args.append(s0)
    x_specs, x_shapes = [], []
    if want_final:
        x_specs.append(pl.BlockSpec((1, 2, 1, dh, dh), lambda b, h: (b, 0, h, 0, 0)))
        x_shapes.append(jax.ShapeDtypeStruct((nb, 2, heads, dh, dh), F32))
    res = _launch_mixer(_ret_kernel, "ret_ctx" if ctx else "ret_lat", geom, ctx, dh, in_specs, args,
                        x_specs, x_shapes, [pltpu.VMEM((t_len, dh), F32)],
                        t_len=t_len, dh=dh, has_init=has_init, want_final=want_final, rope=rope)
    return (res[0], res[1]) if want_final else (res[0], None)


def _ml_kernel(*refs, t_len, dh, n_heads, layer, depth, gate_lane0, has_init, want_final):
    it = iter(refs)
    bias_ref = next(it)
    m0_ref = next(it) if has_init else None
    q_ref, k_ref, v_ref, og_ref, zg_ref, zgt_ref, normg_ref = (next(it) for _ in range(7))
    c0_ref, n0_ref = (next(it), next(it)) if has_init else (None, None)
    y_ref = next(it)
    cf_ref, nf_ref, mf_ref = (next(it), next(it), next(it)) if want_final else (None, None, None)
    o_ref = next(it)

    b = pl.program_id(0)
    h = pl.program_id(1)
    cl = min(SEQ_CHUNK, t_len)
    nc = t_len // cl
    scale = dh ** -0.5

    qb = q_ref[...]
    qf = qb.astype(F32)
    kf = k_ref[...].astype(F32) * scale
    kb = kf.astype(BF16)

    zg = zg_ref[...]
    zgt = zgt_ref[...]
    lane = _iota2((1, zg.shape[1]), 1)
    sub = _iota2((zgt.shape[0], 1), 0)
    ti = _iota2((cl, cl), 0)
    si = _iota2((cl, cl), 1)

    def rows(c):
        return slice(c * cl, (c + 1) * cl)

    for d in (0, 1):
        gi = d * 2 * n_heads + h
        gf = gi + n_heads
        bi = bias_ref[d * n_heads + h]
        bf = bias_ref[(2 + d) * n_heads + h]
        i_col = jnp.sum(jnp.where(lane == gate_lane0 + gi, zg, 0.0), axis=1, keepdims=True) + bi
        f_col = _log_sigmoid(jnp.sum(jnp.where(lane == gate_lane0 + gf, zg, 0.0), axis=1, keepdims=True) + bf)
        i_row = jnp.sum(jnp.where(sub == gi, zgt, 0.0), axis=0, keepdims=True) + bi
        f_row = _log_sigmoid(jnp.sum(jnp.where(sub == gf, zgt, 0.0), axis=0, keepdims=True) + bf)
        keep = (si <= ti) if d == 0 else (si >= ti)
        keep_t = (ti <= si) if d == 0 else (ti >= si)

        if has_init:
            c_st = c0_ref[0, d, 0]
            n_st = n0_ref[0, d, 0]
            m_st = jnp.full((1, 1), m0_ref[((b * depth + layer) * 2 + d) * n_heads + h], F32)
        else:
            c_st = None
            n_st = None
            m_st = jnp.zeros((1, 1), F32)

        order = range(nc) if d == 0 else range(nc - 1, -1, -1)
        for pos, c in enumerate(order):
            rc = rows(c)
            qc, kc, vc = qb[rc], kb[rc], v_ref[rc, :]
            ic = i_col[rc]
            ir, fr = i_row[:, rc], f_row[:, rc]
            f_cum_col = jnp.sum(jnp.where(keep, fr, 0.0), axis=1, keepdims=True)
            f_cum_row = jnp.sum(jnp.where(keep_t, f_col[rc], 0.0), axis=0, keepdims=True)
            dmat = jnp.where(keep, f_cum_col + (ir - f_cum_row), -jnp.inf)
            g = f_cum_col + m_st
            m_t = jnp.maximum(g, jnp.max(dmat, axis=1, keepdims=True))
            p = _nt(qc, kc) * jnp.exp(dmat - m_t)
            num = _dot(p.astype(BF16), vc)
            den = jnp.sum(p, axis=1, keepdims=True)
            if c_st is not None:
                w_state = jnp.exp(g - m_t)
                num = num + w_state * _dot(qc, c_st.astype(BF16))
                den = den + w_state * jnp.sum(qf[rc] * n_st, axis=1, keepdims=True)
            hh = num / jnp.maximum(jnp.abs(den), jnp.exp(-m_t))
            if d == 0:
                o_ref[rc, :] = hh
            else:
                o_ref[rc, :] += hh
            if pos < nc - 1 or want_final:
                f_last = jnp.sum(fr, axis=1, keepdims=True)
                src = f_last - f_cum_col + ic
                m_new = jnp.maximum(f_last + m_st, jnp.max(src, axis=0, keepdims=True))
                kw = kf[rc] * jnp.exp(src - m_new)
                upd_c = _tn(kw.astype(BF16), vc)
                upd_n = jnp.sum(kw, axis=0, keepdims=True)
                if c_st is not None:
                    w_keep = jnp.exp(f_last + m_st - m_new)
                    c_st = w_keep * c_st + upd_c
                    n_st = w_keep * n_st + upd_n
                else:
                    c_st, n_st = upd_c, upd_n
                m_st = m_new
        if want_final:
            cf_ref[0, d, 0] = c_st
            nf_ref[0, d, 0] = n_st
            mf_ref[0, d, 0] = jnp.broadcast_to(m_st, (1, mf_ref.shape[-1]))
    y = _head_norm_gate(o_ref[...], normg_ref, og_ref[...].astype(F32), jax.nn.sigmoid)
    y_ref[...] = y.astype(y_ref.dtype)


def _ml_mixer(z, zg, zgt, bias, normg, init, layer, depth, gate_lane0, geom, cols, ctx):
    nb, t_len, blk0 = _seq_geometry(geom, ctx)
    mix, heads = geom["mix"], geom["ml_heads"]
    dh = mix // heads
    has_init, want_final = init is not None, ctx
    smem = pl.BlockSpec(memory_space=pltpu.SMEM)
    in_specs, args = [smem], [bias]
    if has_init:
        in_specs.append(smem)
        args.append(init["m"])
    in_specs += [
        _col_spec(t_len, dh, cols["ml_q"], blk0),
        _col_spec(t_len, dh, cols["ml_k"], blk0),
        _col_spec(t_len, dh, cols["ml_v"], blk0),
        _col_spec(t_len, dh, cols["ml_o"], blk0),
        pl.BlockSpec((t_len, LANES), lambda b, h: (blk0 + b, 0)),
        pl.BlockSpec((zgt.shape[0], t_len), lambda b, h: (0, blk0 + b)),
        pl.BlockSpec((1, dh), lambda b, h: (0, h)),
    ]
    args += [z, z, z, z, zg, zgt, normg]
    if has_init:
        in_specs += [pl.BlockSpec((1, 2, 1, dh, dh), lambda b, h: (b, 0, h, 0, 0)),
                     pl.BlockSpec((1, 2, 1, 1, dh), lambda b, h: (b, 0, h, 0, 0))]
        args += [init["c"], init["n"]]
    x_specs, x_shapes = [], []
    if want_final:
        x_specs = [pl.BlockSpec((1, 2, 1, dh, dh), lambda b, h: (b, 0, h, 0, 0)),
                   pl.BlockSpec((1, 2, 1, 1, dh), lambda b, h: (b, 0, h, 0, 0)),
                   pl.BlockSpec((1, 2, 1, 1, LANES), lambda b, h: (b, 0, h, 0, 0))]
        x_shapes = [jax.ShapeDtypeStruct((nb, 2, heads, dh, dh), F32),
                    jax.ShapeDtypeStruct((nb, 2, heads, 1, dh), F32),
                    jax.ShapeDtypeStruct((nb, 2, heads, 1, LANES), F32)]
    res = _launch_mixer(_ml_kernel, "mlstm_ctx" if ctx else "mlstm_lat", geom, ctx, dh, in_specs, args,
                        x_specs, x_shapes, [pltpu.VMEM((t_len, dh), F32)],
                        t_len=t_len, dh=dh, n_heads=heads, layer=layer, depth=depth, gate_lane0=gate_lane0,
                        has_init=has_init, want_final=want_final)
    return (res[0], res[1:]) if want_final else (res[0], None)


def _merge_kernel(*refs, nbr, ctx_tiles):
    h_ref = refs[0]
    y_refs = refs[1:1 + 2 * nbr]
    m_refs = refs[1 + 2 * nbr:1 + 3 * nbr]
    wb_ref, o_ref = refs[1 + 3 * nbr], refs[2 + 3 * nbr]
    is_ctx = pl.program_id(0) < ctx_tiles
    h = h_ref[...]
    acc = None
    for i in range(nbr):
        y = jnp.where(is_ctx, y_refs[2 * i][...], y_refs[2 * i + 1][...])
        term = jax.nn.sigmoid(_dot(h, m_refs[i][...])) * _dot(y, wb_ref[i])
        acc = term if acc is None else acc + term
    o_ref[...] = acc.astype(o_ref.dtype)


def _merge(h, y_pairs, w_merge, w_branch, geom):
    n, d = h.shape
    mix = geom["mix"]
    nbr = len(y_pairs)
    tm = _pick(math.gcd(geom["n_ctx"], n - geom["n_ctx"]), (512, 256, 128))
    tn = _pick(d, (256, 128))
    nj = d // tn
    ctx_tiles = geom["n_ctx"] // tm
    in_specs = [pl.BlockSpec((tm, d), lambda i, j: (i, 0))]
    for _ in range(nbr):
        in_specs.append(pl.BlockSpec((tm, mix), lambda i, j: (jnp.minimum(i, ctx_tiles - 1), 0)))
        in_specs.append(pl.BlockSpec((tm, mix), lambda i, j: (jnp.maximum(i - ctx_tiles, 0), 0)))
    in_specs += [pl.BlockSpec((d, tn), lambda i, j, br=br: (0, br * nj + j)) for br in range(nbr)]
    in_specs += [pl.BlockSpec((nbr, mix, tn), lambda i, j: (0, 0, j))]
    flat = [y for pair in y_pairs for y in pair]
    return pl.pallas_call(
        functools.partial(_merge_kernel, nbr=nbr, ctx_tiles=ctx_tiles),
        grid=(n // tm, nj),
        in_specs=in_specs,
        out_specs=pl.BlockSpec((tm, tn), lambda i, j: (i, j)),
        out_shape=jax.ShapeDtypeStruct((n, d), BF16),
        compiler_params=_cparams(("arbitrary", "arbitrary")),
        name="merge",
    )(h, *flat, *([w_merge] * nbr), w_branch)


def _outproj_kernel(m_ref, w_ref, x_ref, g_ref, o_ref):
    o_ref[...] = x_ref[...] + g_ref[0, 0] * _dot(m_ref[...], w_ref[...])


def _outproj(merged, w_out, x, mod4, comp_gate, geom):
    n, d = x.shape
    tm = _pick(n, (512, 256, 128))
    tn = _pick(d, (512, 256, 128))
    n_ctx, dec_seq, ctx_row = geom["n_ctx"], geom["dec_seq"], geom["ctx_row"]

    def gate_map(i, j):
        start = i * tm
        return (jnp.where(start < n_ctx, ctx_row, (start - n_ctx) // dec_seq), comp_gate, 0, j)

    return pl.pallas_call(
        _outproj_kernel,
        grid=(n // tm, d // tn),
        in_specs=[
            pl.BlockSpec((tm, d), lambda i, j: (i, 0)),
            pl.BlockSpec((d, tn), lambda i, j: (0, j)),
            pl.BlockSpec((tm, tn), lambda i, j: (i, j)),
            pl.BlockSpec((1, 1, 1, tn), gate_map),
        ],
        out_specs=pl.BlockSpec((tm, tn), lambda i, j: (i, j)),
        out_shape=jax.ShapeDtypeStruct((n, d), F32),
        compiler_params=_cparams(("arbitrary", "arbitrary")),
        name="out_proj",
    )(merged, w_out, x, mod4)


def _first_argmax(vals):
    best, idx = vals[0], jnp.zeros_like(vals[0])
    for j in range(1, len(vals)):
        better = vals[j] > best
        idx = jnp.where(better, float(j), idx)
        best = jnp.where(better, vals[j], best)
    return idx, best


def _pick_row(vals, idx):
    out = vals[0]
    for j in range(1, len(vals)):
        out = jnp.where(idx == float(j), vals[j], out)
    return out


def _router_kernel(x_ref, g_ref, sh_ref, sc_ref, wrt_ref, rb_ref, h_ref, eidx_ref, wgt_ref, rank_ref, cnt_ref,
                   carry_ref, *, n_exp, n_groups):
    i = pl.program_id(0)
    x = x_ref[...]
    y = x * lax.rsqrt(jnp.mean(x * x, axis=-1, keepdims=True) + EPS)
    h = (y * g_ref[...]) * (1.0 + sc_ref[0, 0]) + sh_ref[0, 0]
    h_ref[...] = h
    tm = x.shape[0]
    per = n_exp // n_groups

    logits = lax.dot_general(wrt_ref[...], h, (((1,), (1,)), ((), ())), precision=HIGHEST,
                             preferred_element_type=F32)
    scores = jax.nn.sigmoid(logits)
    sel = scores + rb_ref[...]
    sel_rows = [sel[e:e + 1, :] for e in range(n_exp)]
    sc_rows = [scores[e:e + 1, :] for e in range(n_exp)]
    group_scores = []
    for g in range(n_groups):
        r = sel_rows[g * per:(g + 1) * per]
        pair = None
        for a in range(per):
            for bb in range(a + 1, per):
                s2 = r[a] + r[bb]
                pair = s2 if pair is None else jnp.maximum(pair, s2)
        group_scores.append(pair)
    g_best, _ = _first_argmax(group_scores)
    in_sel = [_pick_row([sel_rows[g * per + j] for g in range(n_groups)], g_best) for j in range(per)]
    in_sc = [_pick_row([sc_rows[g * per + j] for g in range(n_groups)], g_best) for j in range(per)]
    j1, _ = _first_argmax(in_sel)
    masked = [jnp.where(j1 == float(j), -jnp.inf, in_sel[j]) for j in range(per)]
    j2, _ = _first_argmax(masked)
    w1 = _pick_row(in_sc, j1)
    w2 = _pick_row(in_sc, j2)
    tot = w1 + w2
    e1 = (g_best * per + j1).astype(jnp.int32)
    e2 = (g_best * per + j2).astype(jnp.int32)
    eidx_ref[...] = jnp.concatenate([e1, e2], axis=0)
    wgt_ref[...] = jnp.concatenate([w1 / tot, w2 / tot], axis=0)

    @pl.when(i == 0)
    def _():
        carry_ref[...] = jnp.zeros_like(carry_ref)

    eio = _iota2((n_exp, tm), 0)
    hit1 = eio == e1
    hit2 = eio == e2
    onehot = jnp.where(hit1 | hit2, 1.0, 0.0)
    before = (_iota2((tm, tm), 0) < _iota2((tm, tm), 1)).astype(BF16)
    prefix = _dot(onehot.astype(BF16), before) + carry_ref[:, 0:1]
    r1 = jnp.sum(jnp.where(hit1, prefix, 0.0), axis=0, keepdims=True)
    r2 = jnp.sum(jnp.where(hit2, prefix, 0.0), axis=0, keepdims=True)
    rank_ref[...] = jnp.concatenate([r1, r2], axis=0).astype(jnp.int32)
    carry_ref[...] = carry_ref[...] + jnp.sum(onehot, axis=1, keepdims=True)
    cnt_ref[...] = carry_ref[...]


def _norm_router(x, g, mod4, comp_shift, comp_scale, w_router_t, router_bias, geom):
    n, d = x.shape
    n_exp = w_router_t.shape[0]
    tm = 256
    row2 = pl.BlockSpec((2, tm), lambda i: (0, i))
    return pl.pallas_call(
        functools.partial(_router_kernel, n_exp=n_exp, n_groups=N_EXPERT_GROUPS),
        grid=(n // tm,),
        in_specs=[
            pl.BlockSpec((tm, d), lambda i: (i, 0)),
            pl.BlockSpec((1, d), lambda i: (0, 0)),
            _mod_spec(d, tm, geom, comp_shift),
            _mod_spec(d, tm, geom, comp_scale),
            pl.BlockSpec((n_exp, d), lambda i: (0, 0)),
            pl.BlockSpec((n_exp, 1), lambda i: (0, 0)),
        ],
        out_specs=[pl.BlockSpec((tm, d), lambda i: (i, 0)), row2, row2, row2,
                   pl.BlockSpec((n_exp, LANES), lambda i: (0, 0))],
        out_shape=[jax.ShapeDtypeStruct((n, d), F32),
                   jax.ShapeDtypeStruct((2, n), jnp.int32),
                   jax.ShapeDtypeStruct((2, n), F32),
                   jax.ShapeDtypeStruct((2, n), jnp.int32),
                   jax.ShapeDtypeStruct((n_exp, LANES), F32)],
        scratch_shapes=[pltpu.VMEM((n_exp, LANES), F32)],
        compiler_params=_cparams(("arbitrary",)),
        name="norm_router",
    )(x, g.reshape(1, d), mod4, mod4, w_router_t, router_bias.reshape(n_exp, 1))


def _row_copy(src_hbm, row, dst_vmem, r, sem):
    return pltpu.make_async_copy(src_hbm.at[pl.ds(row, 1), :], dst_vmem.at[pl.ds(r, 1), :], sem)


def _dispatch_kernel(src_ref, h_hbm, o_ref, sem):
    tg = o_ref.shape[0]

    def start(r, c):
        _row_copy(h_hbm, src_ref[0, 0, r], o_ref, r, sem).start()
        return c

    def wait(r, c):
        _row_copy(h_hbm, 0, o_ref, r, sem).wait()
        return c

    lax.fori_loop(0, tg, start, 0)
    lax.fori_loop(0, tg, wait, 0)


def _dispatch(h, src):
    p_rows = src.shape[0]
    d = h.shape[1]
    tg = ROUTE_TILE
    return pl.pallas_call(
        _dispatch_kernel,
        grid=(p_rows // tg,),
        in_specs=[
            pl.BlockSpec((1, 1, tg), lambda i: (i, 0, 0), memory_space=pltpu.SMEM),
            pl.BlockSpec(memory_space=pl.ANY),
        ],
        out_specs=pl.BlockSpec((tg, d), lambda i: (i, 0)),
        out_shape=jax.ShapeDtypeStruct((p_rows, d), F32),
        scratch_shapes=[pltpu.SemaphoreType.DMA(())],
        compiler_params=_cparams(("arbitrary",)),
        name="moe_dispatch",
    )(src.reshape(p_rows // tg, 1, tg), h)


def _ffn_up_kernel(te_ref, tv_ref, x_ref, wg_ref, wu_ref, o_ref):
    t = pl.program_id(1)

    @pl.when(tv_ref[t] == 1)
    def _():
        x = x_ref[...].astype(BF16)
        o_ref[...] = (_silu(_dot(x, wg_ref[0])) * _dot(x, wu_ref[0])).astype(o_ref.dtype)

    @pl.when(tv_ref[t] == 0)
    def _():
        o_ref[...] = jnp.zeros_like(o_ref)


def _ffn_up(xs, w_gate, w_up, tile_expert, tile_valid):
    p_rows, d = xs.shape
    f = w_gate.shape[2]
    tm = ROUTE_TILE
    tf = _pick(f, (512, 256, 128))
    grid_spec = pltpu.PrefetchScalarGridSpec(
        num_scalar_prefetch=2,
        grid=(f // tf, p_rows // tm),
        in_specs=[
            pl.BlockSpec((tm, d), lambda j, t, te, tv: (t, 0)),
            pl.BlockSpec((1, d, tf), lambda j, t, te, tv: (te[t], 0, j)),
            pl.BlockSpec((1, d, tf), lambda j, t, te, tv: (te[t], 0, j)),
        ],
        out_specs=pl.BlockSpec((tm, tf), lambda j, t, te, tv: (t, j)),
    )
    return pl.pallas_call(
        _ffn_up_kernel,
        grid_spec=grid_spec,
        out_shape=jax.ShapeDtypeStruct((p_rows, f), BF16),
        compiler_params=_cparams(("arbitrary", "arbitrary")),
        name="moe_ffn_up",
    )(tile_expert, tile_valid, xs, w_gate, w_up)


def _ffn_down_kernel(te_ref, tv_ref, h_ref, wd_ref, o_ref):
    t = pl.program_id(1)

    @pl.when(tv_ref[t] == 1)
    def _():
        o_ref[...] = _dot(h_ref[...], wd_ref[0])

    @pl.when(tv_ref[t] == 0)
    def _():
        o_ref[...] = jnp.zeros_like(o_ref)


def _ffn_down(hmid, w_down, tile_expert, tile_valid):
    p_rows, f = hmid.shape
    d = w_down.shape[2]
    tm = ROUTE_TILE
    tn = _pick(d, (1024, 512, 256, 128))
    grid_spec = pltpu.PrefetchScalarGridSpec(
        num_scalar_prefetch=2,
        grid=(d // tn, p_rows // tm),
        in_specs=[
            pl.BlockSpec((tm, f), lambda j, t, te, tv: (t, 0)),
            pl.BlockSpec((1, f, tn), lambda j, t, te, tv: (te[t], 0, j)),
        ],
        out_specs=pl.BlockSpec((tm, tn), lambda j, t, te, tv: (t, j)),
    )
    return pl.pallas_call(
        _ffn_down_kernel,
        grid_spec=grid_spec,
        out_shape=jax.ShapeDtypeStruct((p_rows, d), F32),
        compiler_params=_cparams(("arbitrary", "arbitrary")),
        name="moe_ffn_down",
    )(tile_expert, tile_valid, hmid, w_down)


def _combine_kernel(pos_ref, x_ref, wt_ref, g_ref, fg_ref, ys_hbm, o_ref, buf_ref, sem, *, final_norm):
    tm = x_ref.shape[0]

    def start(r, c):
        _row_copy(ys_hbm, pos_ref[0, 0, r], buf_ref.at[0], r, sem).start()
        _row_copy(ys_hbm, pos_ref[0, 1, r], buf_ref.at[1], r, sem).start()
        return c

    def wait(r, c):
        _row_copy(ys_hbm, 0, buf_ref.at[0], r, sem).wait()
        _row_copy(ys_hbm, 0, buf_ref.at[1], r, sem).wait()
        return c

    lax.fori_loop(0, tm, start, 0)
    lax.fori_loop(0, tm, wait, 0)
    w = wt_ref[...]
    lane = _iota2((1, w.shape[1]), 1)
    w0 = jnp.sum(jnp.where(lane == 0, w, 0.0), axis=1, keepdims=True)
    w1 = jnp.sum(jnp.where(lane == 1, w, 0.0), axis=1, keepdims=True)
    moe = w0 * buf_ref[0] + w1 * buf_ref[1]
    x = x_ref[...] + g_ref[0, 0] * moe
    if final_norm:
        x = (x * lax.rsqrt(jnp.mean(x * x, axis=-1, keepdims=True) + EPS)) * fg_ref[...]
    o_ref[...] = x


def _combine(x, ys, pos, wgt_cols, mod4, comp_gate, final_g, geom, final_norm):
    n, d = x.shape
    tm = ROUTE_TILE
    return pl.pallas_call(
        functools.partial(_combine_kernel, final_norm=final_norm),
        grid=(n // tm,),
        in_specs=[
            pl.BlockSpec((1, 2, tm), lambda i: (i, 0, 0), memory_space=pltpu.SMEM),
            pl.BlockSpec((tm, d), lambda i: (i, 0)),
            pl.BlockSpec((tm, wgt_cols.shape[1]), lambda i: (i, 0)),
            _mod_spec(d, tm, geom, comp_gate),
            pl.BlockSpec((1, d), lambda i: (0, 0)),
            pl.BlockSpec(memory_space=pl.ANY),
        ],
        out_specs=pl.BlockSpec((tm, d), lambda i: (i, 0)),
        out_shape=jax.ShapeDtypeStruct((n, d), F32),
        scratch_shapes=[pltpu.VMEM((2, tm, d), F32), pltpu.SemaphoreType.DMA(())],
        compiler_params=_cparams(("arbitrary",)),
        name="moe_combine",
    )(pos, x, wgt_cols, mod4, final_g.reshape(1, d), ys)


def _route_plan(eidx, rank, counts, n_exp):
    n = eidx.shape[1]
    tile = ROUTE_TILE
    n_tiles = (2 * n) // tile + n_exp
    cnt = counts.astype(jnp.int32)
    padded = ((cnt + tile - 1) // tile) * tile
    ends = jnp.cumsum(padded)
    offs = ends - padded
    pos = offs[eidx] + rank
    tile_start = jnp.arange(n_tiles, dtype=jnp.int32) * tile
    tile_expert = jnp.minimum(jnp.searchsorted(ends, tile_start, side="right"), n_exp - 1).astype(jnp.int32)
    tile_valid = (tile_start < ends[-1]).astype(jnp.int32)
    tok = jnp.broadcast_to(jnp.arange(n, dtype=jnp.int32), (2, n))
    src = jnp.zeros((n_tiles * tile,), jnp.int32).at[pos.reshape(-1)].set(tok.reshape(-1))
    pos_tiles = pos.reshape(2, n // tile, tile).transpose(1, 0, 2)
    return pos_tiles, src, tile_expert, tile_valid


def _s5_params(lam_re, lam_im, log_step, b_re, b_im, c_re, c_im):
    n_dir, groups, p = lam_re.shape
    gch = b_re.shape[-1]
    gps = S5_SLAB // gch
    n_slab = groups // gps
    step = jnp.exp(log_step)[..., None]
    mag = jnp.exp(lam_re * step)
    a_re = mag * jnp.cos(lam_im * step)
    a_im = mag * jnp.sin(lam_im * step)
    den = lam_re * lam_re + lam_im * lam_im
    z_re = ((a_re - 1.0) * lam_re + a_im * lam_im) / den
    z_im = (a_im * lam_re - (a_re - 1.0) * lam_im) / den
    bb_re = z_re[..., None] * b_re - z_im[..., None] * b_im
    bb_im = z_re[..., None] * b_im + z_im[..., None] * b_re
    eye = jnp.eye(gps, dtype=F32)
    bbs = jnp.stack([bb_re, bb_im], axis=2).reshape(n_dir, n_slab, gps, 2, p, gch)
    bm = jnp.einsum("dsgrpc,gh->dsgcrhp", bbs, eye).reshape(n_dir, n_slab, S5_SLAB, 2 * gps * p)
    cs = jnp.stack([c_re, -c_im], axis=0).reshape(2, n_slab, gps, gch, p)
    cm = jnp.einsum("rsgcp,gh->srgphc", cs, eye).reshape(n_slab, 2 * gps * p, S5_SLAB)
    a = jnp.stack([a_re, a_im], axis=2).reshape(n_dir, n_slab, gps, 2, p).transpose(0, 1, 3, 2, 4)
    a = a.reshape(n_dir, n_slab, 1, 2 * gps * p)
    a8 = jnp.broadcast_to(a, (n_dir, n_slab, SUBLANES, 2 * gps * p))
    return bm.astype(BF16), cm.astype(BF16), a8


def _axial_rope(t, dh):
    rows = t // GRID_W
    row = jnp.repeat(jnp.arange(rows, dtype=F32), GRID_W)
    col = (jnp.arange(rows * GRID_W) % GRID_W).astype(F32)
    n_freq = dh // 4
    inv = ROPE_BASE ** (-jnp.arange(n_freq, dtype=F32) / n_freq)
    ang = jnp.concatenate([row[:, None] * inv, col[:, None] * inv], axis=-1)
    return jnp.cos(ang), jnp.sin(ang)


def _to_time_major(u, nb, t_len, bp):
    u = u.reshape(nb, t_len, -1).transpose(1, 0, 2)
    if bp != nb:
        u = jnp.pad(u, ((0, 0), (0, bp - nb), (0, 0)))
    return u.reshape(t_len * bp, -1)


def _from_time_major(y, nb, t_len, bp):
    return y.reshape(t_len, bp, -1)[:, :nb].transpose(1, 0, 2).reshape(nb * t_len, -1)


def _s5_state_to_slabs(s_re, s_im, n_slab, bp):
    nb = s_re.shape[0]
    st = jnp.concatenate([s_re.reshape(nb, 2, n_slab, -1), s_im.reshape(nb, 2, n_slab, -1)], axis=-1)
    st = st.transpose(1, 2, 0, 3)
    return jnp.pad(st, ((0, 0), (0, 0), (0, bp - nb), (0, 0)))


def _s5_slabs_to_state(xf, nb, groups, p):
    half = xf.shape[-1] // 2
    re = xf[:, :, :nb, :half].transpose(2, 0, 1, 3).reshape(nb, 2, groups, p)
    im = xf[:, :, :nb, half:].transpose(2, 0, 1, 3).reshape(nb, 2, groups, p)
    return re, im


def _round_up(x, m):
    return (x + m - 1) // m * m


def kernel(x_prompt, x_sample, c, c_ctx, state_s5_re, state_s5_im, state_gla, state_ml_c, state_ml_n, state_ml_m, state_ret, w_ada, b_ada, norm1_g, norm2_g, w_in, s5_lambda_re, s5_lambda_im, s5_log_step, s5_b_re, s5_b_im, s5_c_re, s5_c_im, s5_d, s5_w_glu, gla_w_a, gla_b_a, gla_norm_g, ml_i_bias, ml_f_bias, ml_norm_g, ret_decay_logit, ret_norm_g, w_branch, w_out, w_router, router_bias, w_exp_gate, w_exp_up, w_exp_down, final_g):
    batch, seq, d = x_prompt.shape
    dec_batch, dec_seq, _ = x_sample.shape
    depth = w_in.shape[0]
    mix = s5_d.shape[-1]
    n_exp = w_router.shape[1]
    gla_heads = state_gla.shape[3]
    ml_heads = ml_i_bias.shape[-1]
    ret_heads = ret_decay_logit.shape[-1]
    gla_rank = gla_w_a.shape[2]
    groups, p_state = s5_lambda_re.shape[2], s5_lambda_re.shape[3]
    n_ctx, n_lat = batch * seq, dec_batch * dec_seq
    n = n_ctx + n_lat
    assert dec_batch + 1 <= SUBLANES and n_ctx % dec_seq == 0 and dec_seq % seq == 0
    geom = dict(batch=batch, seq=seq, dec_batch=dec_batch, dec_seq=dec_seq, n_ctx=n_ctx, ctx_row=dec_batch,
                mix=mix, gla_heads=gla_heads, ml_heads=ml_heads, ret_heads=ret_heads)
    n_slab = mix // S5_SLAB
    bp_ctx, bp_lat = _round_up(batch, SUBLANES), _round_up(dec_batch, SUBLANES)

    half = mix // 2
    widths = [("s5_u", mix), ("gla_q", half), ("gla_k", half), ("gla_v", mix), ("gla_g", mix), ("gla_r", gla_rank),
              ("ml_q", mix), ("ml_k", mix), ("ml_v", mix), ("ml_o", mix), ("ml_if", 4 * ml_heads),
              ("ret_q", mix), ("ret_k", mix), ("ret_v", mix), ("ret_g", mix), ("merge", 4 * d)]
    src_off, o = {}, 0
    for name, w in widths:
        src_off[name] = (o, w)
        o += w
    main_names = [nm for nm, _ in widths if nm not in ("gla_r", "ml_if", "merge")]
    cols, o = {}, 0
    for nm in main_names:
        cols[nm] = o
        o += src_off[nm][1]

    def w_cols(wl, names):
        return jnp.concatenate([wl[:, src_off[nm][0]:src_off[nm][0] + src_off[nm][1]] for nm in names], axis=1)

    x = jnp.concatenate([x_prompt.reshape(n_ctx, d), x_sample.reshape(n_lat, d)], axis=0)
    c8 = jnp.zeros((SUBLANES, d), F32).at[:dec_batch].set(c).at[dec_batch].set(c_ctx)
    mod = _ada(c8, w_ada, b_ada)
    rope_tabs = _axial_rope(dec_seq, mix // ret_heads)
    w_router_t = w_router.T

    ctx_states = []
    for l in range(depth):
        mod4 = mod[l].reshape(SUBLANES, N_MOD, 1, d)
        wl = w_in[l]
        w_main = w_cols(wl, main_names).astype(BF16)
        gate_w = w_cols(wl, ["gla_r", "ml_if"])
        w_gate = jnp.pad(gate_w, ((0, 0), (0, LANES - gate_w.shape[1]))).astype(BF16)
        w_merge = wl[:, src_off["merge"][0]:].astype(BF16)

        h = _norm_mod(x, norm1_g[l], mod4, 0, 1, geom, BF16)
        z = _matmul(h, w_main, BF16, _pick(n, (1024, 512, 256)), _pick(w_main.shape[1], (512, 256, 128)), "in_proj")
        zg = _matmul(h, w_gate, F32, _pick(n, (1024, 512, 256)), LANES, "gate_proj")
        zgt = zg[:, gla_rank:gla_rank + 4 * ml_heads].T

        bm, cm, a8 = _s5_params(s5_lambda_re[l], s5_lambda_im[l], s5_log_step[l], s5_b_re[l], s5_b_im[l],
                                s5_c_re[l], s5_c_im[l])
        dvec = s5_d[l].reshape(1, mix)
        u = z[:, :mix]
        y_c, xf = _s5_mixer(_to_time_major(u[:n_ctx], batch, seq, bp_ctx), bm, cm, a8, dvec, None, seq, bp_ctx, True)
        x0 = _s5_state_to_slabs(state_s5_re[:, l], state_s5_im[:, l], n_slab, bp_lat)
        y_l, _ = _s5_mixer(_to_time_major(u[n_ctx:], dec_batch, dec_seq, bp_lat), bm, cm, a8, dvec, x0, dec_seq,
                           bp_lat, False)
        w_glu = s5_w_glu[l].astype(BF16)
        y_s5 = (_glu(_from_time_major(y_c, batch, seq, bp_ctx), w_glu),
                _glu(_from_time_major(y_l, dec_batch, dec_seq, bp_lat), w_glu))
        s5_re_l, s5_im_l = _s5_slabs_to_state(xf, batch, groups, p_state)

        dk = half // gla_heads
        wa = gla_w_a[l].reshape(2, gla_rank, gla_heads, dk).transpose(0, 2, 1, 3)
        wa = jnp.pad(wa, ((0, 0), (0, 0), (0, LANES - gla_rank), (0, 0)))
        ba = gla_b_a[l].reshape(2, gla_heads, 1, dk)
        gp = dict(wa=wa, wat=wa.transpose(0, 1, 3, 2), ba=ba, bat=ba.transpose(0, 1, 3, 2),
                  normg=gla_norm_g[l].reshape(1, mix))
        y_gla_c, gla_l = _gla_mixer(z, zg, gp, None, geom, cols, True)
        y_gla_l, _ = _gla_mixer(z, zg, gp, state_gla[:, l], geom, cols, False)

        ml_bias = jnp.stack([ml_i_bias[l], ml_f_bias[l]], axis=0).reshape(-1)
        ml_ng = ml_norm_g[l].reshape(1, mix)
        y_ml_c, ml_l = _ml_mixer(z, zg, zgt, ml_bias, ml_ng, None, l, depth, gla_rank, geom, cols, True)
        ml_init = dict(c=state_ml_c[:, l], n=state_ml_n[:, l][:, :, :, None, :], m=state_ml_m.reshape(-1))
        y_ml_l, _ = _ml_mixer(z, zg, zgt, ml_bias, ml_ng, ml_init, l, depth, gla_rank, geom, cols, False)

        lg = jax.nn.log_sigmoid(ret_decay_logit[l]).reshape(-1)
        ret_ng = ret_norm_g[l].reshape(1, mix)
        y_ret_c, ret_l = _ret_mixer(z, lg, ret_ng, None, None, geom, cols, True)
        y_ret_l, _ = _ret_mixer(z, lg, ret_ng, rope_tabs, state_ret[:, l], geom, cols, False)

        y_pairs = [y_s5, (y_gla_c, y_gla_l), (y_ml_c, y_ml_l), (y_ret_c, y_ret_l)]
        merged = _merge(h, y_pairs, w_merge, w_branch[l].astype(BF16), geom)
        x = _outproj(merged, w_out[l].astype(BF16), x, mod4, 2, geom)
        ctx_states.append((s5_re_l, s5_im_l, gla_l, ml_l[0], ml_l[1][:, :, :, 0, :], ml_l[2][:, :, :, 0, 0], ret_l))

        h2, eidx, wgt, rank, counts = _norm_router(x, norm2_g[l], mod4, 3, 4, w_router_t, router_bias, geom)
        pos_tiles, src, tile_expert, tile_valid = _route_plan(eidx, rank, counts[:, 0], n_exp)
        xs = _dispatch(h2, src)
        hmid = _ffn_up(xs, w_exp_gate[l].astype(BF16), w_exp_up[l].astype(BF16), tile_expert, tile_valid)
        ys = _ffn_down(hmid, w_exp_down[l].astype(BF16), tile_expert, tile_valid)
        x = _combine(x, ys, pos_tiles, wgt.T, mod4, 5, final_g, geom, final_norm=(l == depth - 1))

    y_prompt = x[:n_ctx].reshape(batch, seq, d)
    y_sample = x[n_ctx:].reshape(dec_batch, dec_seq, d)
    stacked = [jnp.stack([st[i] for st in ctx_states], axis=1) for i in range(7)]
    return (y_prompt, y_sample, *stacked)
```

```python
import functools
import math

import jax
import jax.numpy as jnp
from jax import lax
from jax.experimental import pallas as pl
from jax.experimental.pallas import tpu as pltpu

F32 = jnp.float32
BF16 = jnp.bfloat16
HIGHEST = lax.Precision.HIGHEST

EPS = 1e-6
GRID_W = 64
ROPE_BASE = 10000.0
GLA_TAU = 16.0
N_EXPERT_GROUPS = 4
N_MOD = 6

LANES = 128
SUBLANES = 8
S5_SLAB = LANES
GLA_CHUNK = 64
SEQ_CHUNK = 256
VMEM_LIMIT_BYTES = 56 * 1024 * 1024
ROUTE_TILE = 256


def _cparams(sem):
    return pltpu.CompilerParams(dimension_semantics=sem, vmem_limit_bytes=VMEM_LIMIT_BYTES)


def _pick(n, cands):
    for c in cands:
        if c <= n and n % c == 0:
            return c
    return n


def _nt(a, b):
    return lax.dot_general(a, b, (((1,), (1,)), ((), ())), preferred_element_type=F32)


def _tn(a, b):
    return lax.dot_general(a, b, (((0,), (0,)), ((), ())), preferred_element_type=F32)


def _dot(a, b):
    return jnp.dot(a, b, preferred_element_type=F32)


def _log_sigmoid(x):
    return jnp.minimum(x, 0.0) - jnp.log1p(jnp.exp(-jnp.abs(x)))


def _silu(x):
    return x * jax.nn.sigmoid(x)


def _iota2(shape, dim):
    return lax.broadcasted_iota(jnp.int32, shape, dim)


HI16 = 0xFFFF0000


def _pack_bf16_pairs(x):
    c = x.shape[1] // 2
    bits = lax.bitcast_convert_type(x.astype(BF16).astype(F32), jnp.uint32)
    return (bits[:, :c] >> 16) | (bits[:, c:] & jnp.uint32(HI16))


def _unpack_bf16_pairs(w):
    lo = lax.bitcast_convert_type(w << 16, F32)
    hi = lax.bitcast_convert_type(w & jnp.uint32(HI16), F32)
    return lo, hi


def _ada_kernel(c_ref, w_ref, b_ref, o_ref):
    s = _silu(c_ref[...]).astype(BF16)
    o_ref[0] = _dot(s, w_ref[0].astype(BF16)) + b_ref[0]


def _ada(c8, w_ada, b_ada):
    depth, d, n = w_ada.shape
    tn = _pick(n, (512, 256, 128))
    return pl.pallas_call(
        _ada_kernel,
        grid=(depth, n // tn),
        in_specs=[
            pl.BlockSpec((SUBLANES, d), lambda l, j: (0, 0)),
            pl.BlockSpec((1, d, tn), lambda l, j: (l, 0, j)),
            pl.BlockSpec((1, 1, tn), lambda l, j: (l, 0, j)),
        ],
        out_specs=pl.BlockSpec((1, SUBLANES, tn), lambda l, j: (l, 0, j)),
        out_shape=jax.ShapeDtypeStruct((depth, SUBLANES, n), F32),
        compiler_params=_cparams(("arbitrary", "arbitrary")),
        name="ada_mod",
    )(c8, w_ada, b_ada.reshape(depth, 1, n))


def _mod_spec(d, tm, geom, comp):
    n_ctx, dec_seq, ctx_row = geom["n_ctx"], geom["dec_seq"], geom["ctx_row"]

    def index_map(i, *_):
        start = i * tm
        row = jnp.where(start < n_ctx, ctx_row, (start - n_ctx) // dec_seq)
        return (row, comp, 0, 0)

    return pl.BlockSpec((1, 1, 1, d), index_map)


def _norm_kernel(x_ref, g_ref, sh_ref, sc_ref, o_ref):
    x = x_ref[...]
    y = x * lax.rsqrt(jnp.mean(x * x, axis=-1, keepdims=True) + EPS)
    h = (y * g_ref[...]) * (1.0 + sc_ref[0, 0]) + sh_ref[0, 0]
    o_ref[...] = h.astype(o_ref.dtype)


def _norm_mod(x, g, mod4, comp_shift, comp_scale, geom, out_dtype):
    n, d = x.shape
    tm = 256
    return pl.pallas_call(
        _norm_kernel,
        grid=(n // tm,),
        in_specs=[
            pl.BlockSpec((tm, d), lambda i: (i, 0)),
            pl.BlockSpec((1, d), lambda i: (0, 0)),
            _mod_spec(d, tm, geom, comp_shift),
            _mod_spec(d, tm, geom, comp_scale),
        ],
        out_specs=pl.BlockSpec((tm, d), lambda i: (i, 0)),
        out_shape=jax.ShapeDtypeStruct((n, d), out_dtype),
        compiler_params=_cparams(("arbitrary",)),
        name="norm_mod",
    )(x, g.reshape(1, d), mod4, mod4)


def _mm_kernel(x_ref, w_ref, o_ref):
    o_ref[...] = _dot(x_ref[...], w_ref[...]).astype(o_ref.dtype)


def _matmul(x, w, out_dtype, tm, tn, name):
    m, k = x.shape
    n = w.shape[1]
    return pl.pallas_call(
        _mm_kernel,
        grid=(m // tm, n // tn),
        in_specs=[
            pl.BlockSpec((tm, k), lambda i, j: (i, 0)),
            pl.BlockSpec((k, tn), lambda i, j: (0, j)),
        ],
        out_specs=pl.BlockSpec((tm, tn), lambda i, j: (i, j)),
        out_shape=jax.ShapeDtypeStruct((m, n), out_dtype),
        compiler_params=_cparams(("arbitrary", "arbitrary")),
        name=name,
    )(x, w)


def _s5_kernel(*refs, t_len, bp, tc, has_init, want_final):
    it = iter(refs)
    u_ref, bm_ref, cm_ref, a_ref, d_ref = (next(it) for _ in range(5))
    x0_ref = next(it) if has_init else None
    y_ref = next(it)
    xf_ref = next(it) if want_final else None
    bu_ref, yacc_ref, st_ref = next(it), next(it), next(it)

    nc = t_len // tc
    rc = tc * bp
    half = bu_ref.shape[1] // 2
    for d in (0, 1):
        if has_init:
            st_ref[...] = x0_ref[d, 0]
        else:
            st_ref[...] = jnp.zeros_like(st_ref)
        bm = bm_ref[d, 0]
        ar = a_ref[d, 0, :, :half]
        ai = a_ref[d, 0, :, half:]

        def chunk_body(c, carry, d=d, bm=bm, ar=ar, ai=ai):
            cc = c if d == 0 else nc - 1 - c
            r0 = pl.multiple_of(cc * rc, rc)
            bu_ref[...] = _dot(u_ref[pl.ds(r0, rc), :], bm)
            for rt in range(bp // SUBLANES):
                rows = slice(rt * SUBLANES, (rt + 1) * SUBLANES)

                def step(t, s, rt=rt):
                    sr, si = s
                    tt = t if d == 0 else tc - 1 - t
                    row = pl.multiple_of(tt * bp + rt * SUBLANES, SUBLANES)
                    b = bu_ref[pl.ds(row, SUBLANES), :]
                    xr = ar * sr - ai * si + b[:, :half]
                    xi = ar * si + ai * sr + b[:, half:]
                    bu_ref[pl.ds(row, SUBLANES), :] = jnp.concatenate([xr, xi], axis=-1)
                    return xr, xi

                sr, si = lax.fori_loop(0, tc, step, (st_ref[rows, :half], st_ref[rows, half:]))
                st_ref[rows, :] = jnp.concatenate([sr, si], axis=-1)
            y = _dot(bu_ref[...].astype(BF16), cm_ref[0])
            if d == 0:
                yacc_ref[pl.ds(r0, rc), :] = y
            else:
                yacc_ref[pl.ds(r0, rc), :] += y
            return carry

        lax.fori_loop(0, nc, chunk_body, 0)
        if want_final:
            xf_ref[d, 0] = st_ref[...]
    y = yacc_ref[...] + d_ref[...] * u_ref[...].astype(F32)
    y_ref[...] = jax.nn.gelu(y).astype(y_ref.dtype)


def _s5_mixer(u_tm, bm, cm, a8, dvec, x0, t_len, bp, want_final):
    rows, mix = u_tm.shape
    n_slab = mix // S5_SLAB
    two_half = bm.shape[-1]
    tc = _pick(t_len, tuple(max(1, 1024 // bp) >> s for s in range(6)))
    has_init = x0 is not None
    in_specs = [
        pl.BlockSpec((rows, S5_SLAB), lambda s: (0, s)),
        pl.BlockSpec((2, 1, S5_SLAB, two_half), lambda s: (0, s, 0, 0)),
        pl.BlockSpec((1, two_half, S5_SLAB), lambda s: (s, 0, 0)),
        pl.BlockSpec((2, 1, SUBLANES, two_half), lambda s: (0, s, 0, 0)),
        pl.BlockSpec((1, S5_SLAB), lambda s: (0, s)),
    ]
    args = [u_tm, bm, cm, a8, dvec]
    if has_init:
        in_specs.append(pl.BlockSpec((2, 1, bp, two_half), lambda s: (0, s, 0, 0)))
        args.append(x0)
    out_specs = [pl.BlockSpec((rows, S5_SLAB), lambda s: (0, s))]
    out_shape = [jax.ShapeDtypeStruct((rows, mix), BF16)]
    if want_final:
        out_specs.append(pl.BlockSpec((2, 1, bp, two_half), lambda s: (0, s, 0, 0)))
        out_shape.append(jax.ShapeDtypeStruct((2, n_slab, bp, two_half), F32))
    res = pl.pallas_call(
        functools.partial(_s5_kernel, t_len=t_len, bp=bp, tc=tc, has_init=has_init, want_final=want_final),
        grid=(n_slab,),
        in_specs=in_specs,
        out_specs=out_specs,
        out_shape=out_shape,
        scratch_shapes=[
            pltpu.VMEM((tc * bp, two_half), F32),
            pltpu.VMEM((rows, S5_SLAB), F32),
            pltpu.VMEM((bp, two_half), F32),
        ],
        compiler_params=_cparams(("arbitrary",)),
        name="s5_ctx" if want_final else "s5_lat",
    )(*args)
    return (res[0], res[1]) if want_final else (res[0], None)


def _glu_kernel(y_ref, w_ref, o_ref):
    y = y_ref[...]
    z = _dot(y, w_ref[...])
    o_ref[...] = (y.astype(F32) * jax.nn.sigmoid(z)).astype(o_ref.dtype)


def _glu(y, w):
    n, mix = y.shape
    tm = _pick(n, (512, 256, 128))
    return pl.pallas_call(
        _glu_kernel,
        grid=(n // tm,),
        in_specs=[pl.BlockSpec((tm, mix), lambda i: (i, 0)), pl.BlockSpec((mix, mix), lambda i: (0, 0))],
        out_specs=pl.BlockSpec((tm, mix), lambda i: (i, 0)),
        out_shape=jax.ShapeDtypeStruct((n, mix), BF16),
        compiler_params=_cparams(("arbitrary",)),
        name="s5_glu",
    )(y, w)


def _head_norm_gate(o, normg_ref, gate, act):
    of = o * lax.rsqrt(jnp.mean(o * o, axis=-1, keepdims=True) + EPS)
    return of * normg_ref[...] * act(gate)


def _seq_geometry(geom, ctx):
    if ctx:
        return geom["batch"], geom["seq"], 0
    return geom["dec_batch"], geom["dec_seq"], geom["n_ctx"] // geom["dec_seq"]


def _col_spec(t_len, width, col0, blk0):
    base = col0 // width
    return pl.BlockSpec((t_len, width), lambda b, h: (blk0 + b, base + h))


def _launch_mixer(body, name, geom, ctx, width, in_specs, args, extra_out_specs, extra_out_shapes, scratch, **kw):
    nb, t_len, _ = _seq_geometry(geom, ctx)
    heads = geom["mix"] // width
    out_specs = [pl.BlockSpec((t_len, width), lambda b, h: (b, h))] + list(extra_out_specs)
    out_shapes = [jax.ShapeDtypeStruct((nb * t_len, geom["mix"]), BF16)] + list(extra_out_shapes)
    return pl.pallas_call(
        functools.partial(body, **kw),
        grid=(nb, heads),
        in_specs=list(in_specs),
        out_specs=out_specs,
        out_shape=out_shapes,
        scratch_shapes=scratch,
        compiler_params=_cparams(("arbitrary", "arbitrary")),
        name=name,
    )(*args)


def _gla_kernel(*refs, t_len, dk, has_init, want_final):
    it = iter(refs)
    q_ref, k_ref, v_ref, g_ref, zg_ref, wa_ref, ba_ref, normg_ref = (next(it) for _ in range(8))
    s0_ref = next(it) if has_init else None
    y_ref = next(it)
    sf_ref = next(it) if want_final else None
    of_ref, ob_ref, cumf_ref, cumb_ref, stf_ref, stb_ref = (next(it) for _ in range(6))

    cl = min(GLA_CHUNK, t_len)
    nc = t_len // cl
    blk = min(SEQ_CHUNK, t_len)
    scale = dk ** -0.5
    dirs = ((of_ref, cumf_ref, stf_ref), (ob_ref, cumb_ref, stb_ref))

    zg = zg_ref[...]
    ti = _iota2((blk, blk), 0)
    si = _iota2((blk, blk), 1)
    same_chunk = (ti // cl) == (si // cl)
    for d, (_, cum_ref, st_ref) in enumerate(dirs):
        la = _log_sigmoid(jnp.dot(zg, wa_ref[d, 0], precision=HIGHEST, preferred_element_type=F32)
                          + ba_ref[d, 0]) / GLA_TAU
        tri = jnp.where(same_chunk & ((si <= ti) if d == 0 else (si >= ti)), 1.0, 0.0)
        for p in range(t_len // blk):
            rows = slice(p * blk, (p + 1) * blk)
            cum_ref[rows, :] = jnp.dot(tri, la[rows], precision=HIGHEST, preferred_element_type=F32)
        st_ref[...] = s0_ref[0, d, 0].T if has_init else jnp.zeros_like(st_ref)

    tl = _iota2((cl, cl), 0)
    sl = _iota2((cl, cl), 1)
    keeps = (sl <= tl, sl >= tl)

    def chunk_body(c, carry):
        for d, (o_ref, cum_ref, st_ref) in enumerate(dirs):
            cc = c if d == 0 else nc - 1 - c
            r0 = pl.multiple_of(cc * cl, cl)
            cum = cum_ref[pl.ds(r0, cl), :]
            tot = cum[cl - 1:cl, :] if d == 0 else cum[0:1, :]
            q = q_ref[pl.ds(r0, cl), :].astype(F32) * scale
            k = k_ref[pl.ds(r0, cl), :].astype(F32)
            v = v_ref[pl.ds(r0, cl), :]
            qd = (q * jnp.exp(cum)).astype(BF16)
            kd = (k * jnp.exp(-cum)).astype(BF16)
            sc = jnp.where(keeps[d], _nt(qd, kd), 0.0)
            st = st_ref[...]
            o_ref[pl.ds(r0, cl), :] = _nt(qd, st.astype(BF16)) + _dot(sc.astype(BF16), v)
            kl = (k * jnp.exp(tot - cum)).astype(BF16)
            st_ref[...] = st * jnp.exp(tot) + _tn(v, kl)
        return carry

    lax.fori_loop(0, nc, chunk_body, 0)
    if want_final:
        sf_ref[0, 0, 0] = stf_ref[...].T
        sf_ref[0, 1, 0] = stb_ref[...].T
    y = _head_norm_gate(of_ref[...] + ob_ref[...], normg_ref, g_ref[...].astype(F32), _silu)
    y_ref[...] = y.astype(y_ref.dtype)


def _gla_mixer(z, zg, gp, s0, geom, cols, ctx):
    nb, t_len, blk0 = _seq_geometry(geom, ctx)
    mix, heads = geom["mix"], geom["gla_heads"]
    dk, dv = mix // 2 // heads, mix // heads
    has_init, want_final = s0 is not None, ctx
    in_specs = [
        _col_spec(t_len, dk, cols["gla_q"], blk0),
        _col_spec(t_len, dk, cols["gla_k"], blk0),
        _col_spec(t_len, dv, cols["gla_v"], blk0),
        _col_spec(t_len, dv, cols["gla_g"], blk0),
        pl.BlockSpec((t_len, LANES), lambda b, h: (blk0 + b, 0)),
        pl.BlockSpec((2, 1, LANES, dk), lambda b, h: (0, h, 0, 0)),
        pl.BlockSpec((2, 1, 1, dk), lambda b, h: (0, h, 0, 0)),
        pl.BlockSpec((1, dv), lambda b, h: (0, h)),
    ]
    args = [z, z, z, z, zg, gp["wa"], gp["ba"], gp["normg"]]
    if has_init:
        in_specs.append(pl.BlockSpec((1, 2, 1, dk, dv), lambda b, h: (b, 0, h, 0, 0)))
        args.append(s0)
    x_specs, x_shapes = [], []
    if want_final:
        x_specs.append(pl.BlockSpec((1, 2, 1, dk, dv), lambda b, h: (b, 0, h, 0, 0)))
        x_shapes.append(jax.ShapeDtypeStruct((nb, 2, heads, dk, dv), F32))
    res = _launch_mixer(_gla_kernel, "gla_ctx" if ctx else "gla_lat", geom, ctx, dv, in_specs, args,
                        x_specs, x_shapes,
                        [pltpu.VMEM((t_len, dv), F32)] * 2 + [pltpu.VMEM((t_len, dk), F32)] * 2
                        + [pltpu.VMEM((dv, dk), F32)] * 2,
                        t_len=t_len, dk=dk, has_init=has_init, want_final=want_final)
    return (res[0], res[1]) if want_final else (res[0], None)


def _ret_kernel(*refs, t_len, dh, has_init, want_final, rope):
    it = iter(refs)
    lg_ref = next(it)
    q_ref, k_ref, v_ref, g_ref, normg_ref = (next(it) for _ in range(5))
    cos_ref, sin_ref = (next(it), next(it)) if rope else (None, None)
    s0_ref = next(it) if has_init else None
    y_ref = next(it)
    sf_ref = next(it) if want_final else None
    o_ref = next(it)

    h = pl.program_id(1)
    n_heads = lg_ref.shape[0] // 2
    lgf = lg_ref[h]
    lgb = lg_ref[n_heads + h]
    cl = min(SEQ_CHUNK, t_len)
    nc = t_len // cl
    scale = dh ** -0.5
    half = dh // 2

    q = q_ref[...].astype(F32)
    k = k_ref[...].astype(F32) * scale
    if rope:
        cos = cos_ref[...]
        sin = sin_ref[...]

        def rot(x):
            x1, x2 = x[:, :half], x[:, half:]
            return jnp.concatenate([x1 * cos - x2 * sin, x1 * sin + x2 * cos], axis=-1)

        q, k = rot(q), rot(k)
    qb = q.astype(BF16)
    kb = k.astype(BF16)

    dt = (_iota2((cl, cl), 0) - _iota2((cl, cl), 1)).astype(F32)
    decay = (jnp.where(dt >= 0, jnp.exp(lgf * jnp.maximum(dt, 0.0)), 0.0)
             + jnp.where(dt <= 0, jnp.exp(lgb * jnp.maximum(-dt, 0.0)), 0.0))
    tcol = _iota2((cl, 1), 0).astype(F32)

    def rows(c):
        return slice(c * cl, (c + 1) * cl)

    s_f = s0_ref[0, 0, 0] if has_init else None
    for c in range(nc):
        qc, kc, vc = qb[rows(c)], kb[rows(c)], v_ref[rows(c), :]
        o = _dot((_nt(qc, kc) * decay).astype(BF16), vc)
        if s_f is not None:
            o = o + _dot((q[rows(c)] * jnp.exp(lgf * (tcol + 1.0))).astype(BF16), s_f.astype(BF16))
        o_ref[rows(c), :] = o
        if c < nc - 1 or want_final:
            upd = _tn((k[rows(c)] * jnp.exp(lgf * (cl - 1.0 - tcol))).astype(BF16), vc)
            s_f = upd if s_f is None else jnp.exp(lgf * cl) * s_f + upd
    if want_final:
        sf_ref[0, 0, 0] = s_f
    s_b = s0_ref[0, 1, 0] if has_init else None
    for c in range(nc - 1, -1, -1):
        vc = v_ref[rows(c), :]
        if s_b is not None:
            o_ref[rows(c), :] += _dot((q[rows(c)] * jnp.exp(lgb * (cl - tcol))).astype(BF16), s_b.astype(BF16))
        if c > 0 or want_final:
            upd = _tn((k[rows(c)] * jnp.exp(lgb * tcol)).astype(BF16), vc)
            s_b = upd if s_b is None else jnp.exp(lgb * cl) * s_b + upd
    if want_final:
        sf_ref[0, 1, 0] = s_b
    y = _head_norm_gate(o_ref[...], normg_ref, g_ref[...].astype(F32), _silu)
    y_ref[...] = y.astype(y_ref.dtype)


def _ret_mixer(z, lg, normg, rope_tabs, s0, geom, cols, ctx):
    nb, t_len, blk0 = _seq_geometry(geom, ctx)
    mix, heads = geom["mix"], geom["ret_heads"]
    dh = mix // heads
    has_init, want_final, rope = s0 is not None, ctx, rope_tabs is not None
    in_specs = [
        pl.BlockSpec(memory_space=pltpu.SMEM),
        _col_spec(t_len, dh, cols["ret_q"], blk0),
        _col_spec(t_len, dh, cols["ret_k"], blk0),
        _col_spec(t_len, dh, cols["ret_v"], blk0),
        _col_spec(t_len, dh, cols["ret_g"], blk0),
        pl.BlockSpec((1, dh), lambda b, h: (0, h)),
    ]
    args = [lg, z, z, z, z, normg]
    if rope:
        in_specs += [pl.BlockSpec((t_len, dh // 2), lambda b, h: (0, 0))] * 2
        args += list(rope_tabs)
    if has_init:
        in_specs.append(pl.BlockSpec((1, 2, 1, dh, dh), lambda b, h: (b, 0, h, 0, 0)))
        args.append(s0)
    x_specs, x_shapes = [], []
    if want_final:
        x_specs.append(pl.BlockSpec((1, 2, 1, dh, dh), lambda b, h: (b, 0, h, 0, 0)))
        x_shapes.append(jax.ShapeDtypeStruct((nb, 2, heads, dh, dh), F32))
    res = _launch_mixer(_ret_kernel, "ret_ctx" if ctx else "ret_lat", geom, ctx, dh, in_specs, args,
                        x_specs, x_shapes, [pltpu.VMEM((t_len, dh), F32)],
                        t_len=t_len, dh=dh, has_init=has_init, want_final=want_final, rope=rope)
    return (res[0], res[1]) if want_final else (res[0], None)


def _ml_kernel(*refs, t_len, dh, n_heads, layer, depth, gate_lane0, has_init, want_final):
    it = iter(refs)
    bias_ref = next(it)
    m0_ref = next(it) if has_init else None
    q_ref, k_ref, v_ref, og_ref, zg_ref, zgt_ref, normg_ref = (next(it) for _ in range(7))
    c0_ref, n0_ref = (next(it), next(it)) if has_init else (None, None)
    y_ref = next(it)
    cf_ref, nf_ref, mf_ref = (next(it), next(it), next(it)) if want_final else (None, None, None)
    o_ref = next(it)

    b = pl.program_id(0)
    h = pl.program_id(1)
    cl = min(SEQ_CHUNK, t_len)
    nc = t_len // cl
    scale = dh ** -0.5

    qb = q_ref[...]
    qf = qb.astype(F32)
    kf = k_ref[...].astype(F32) * scale
    kb = kf.astype(BF16)

    zg = zg_ref[...]
    zgt = zgt_ref[...]
    lane = _iota2((1, zg.shape[1]), 1)
    sub = _iota2((zgt.shape[0], 1), 0)
    ti = _iota2((cl, cl), 0)
    si = _iota2((cl, cl), 1)

    def rows(c):
        return slice(c * cl, (c + 1) * cl)

    for d in (0, 1):
        gi = d * 2 * n_heads + h
        gf = gi + n_heads
        bi = bias_ref[d * n_heads + h]
        bf = bias_ref[(2 + d) * n_heads + h]
        i_col = jnp.sum(jnp.where(lane == gate_lane0 + gi, zg, 0.0), axis=1, keepdims=True) + bi
        f_col = _log_sigmoid(jnp.sum(jnp.where(lane == gate_lane0 + gf, zg, 0.0), axis=1, keepdims=True) + bf)
        i_row = jnp.sum(jnp.where(sub == gi, zgt, 0.0), axis=0, keepdims=True) + bi
        f_row = _log_sigmoid(jnp.sum(jnp.where(sub == gf, zgt, 0.0), axis=0, keepdims=True) + bf)
        keep = (si <= ti) if d == 0 else (si >= ti)
        keep_t = (ti <= si) if d == 0 else (ti >= si)

        if has_init:
            c_st = c0_ref[0, d, 0]
            n_st = n0_ref[0, d, 0]
            m_st = jnp.full((1, 1), m0_ref[((b * depth + layer) * 2 + d) * n_heads + h], F32)
        else:
            c_st = None
            n_st = None
            m_st = jnp.zeros((1, 1), F32)

        order = range(nc) if d == 0 else range(nc - 1, -1, -1)
        for pos, c in enumerate(order):
            rc = rows(c)
            qc, kc, vc = qb[rc], kb[rc], v_ref[rc, :]
            ic = i_col[rc]
            ir, fr = i_row[:, rc], f_row[:, rc]
            f_cum_col = jnp.sum(jnp.where(keep, fr, 0.0), axis=1, keepdims=True)
            f_cum_row = jnp.sum(jnp.where(keep_t, f_col[rc], 0.0), axis=0, keepdims=True)
            dmat = jnp.where(keep, f_cum_col + (ir - f_cum_row), -jnp.inf)
            g = f_cum_col + m_st
            m_t = jnp.maximum(g, jnp.max(dmat, axis=1, keepdims=True))
            p = _nt(qc, kc) * jnp.exp(dmat - m_t)
            num = _dot(p.astype(BF16), vc)
            den = jnp.sum(p, axis=1, keepdims=True)
            if c_st is not None:
                w_state = jnp.exp(g - m_t)
                num = num + w_state * _dot(qc, c_st.astype(BF16))
                den = den + w_state * jnp.sum(qf[rc] * n_st, axis=1, keepdims=True)
            hh = num / jnp.maximum(jnp.abs(den), jnp.exp(-m_t))
            if d == 0:
                o_ref[rc, :] = hh
            else:
                o_ref[rc, :] += hh
            if pos < nc - 1 or want_final:
                f_last = jnp.sum(fr, axis=1, keepdims=True)
                src = f_last - f_cum_col + ic
                m_new = jnp.maximum(f_last + m_st, jnp.max(src, axis=0, keepdims=True))
                kw = kf[rc] * jnp.exp(src - m_new)
                upd_c = _tn(kw.astype(BF16), vc)
                upd_n = jnp.sum(kw, axis=0, keepdims=True)
                if c_st is not None:
                    w_keep = jnp.exp(f_last + m_st - m_new)
                    c_st = w_keep * c_st + upd_c
                    n_st = w_keep * n_st + upd_n
                else:
                    c_st, n_st = upd_c, upd_n
                m_st = m_new
        if want_final:
            cf_ref[0, d, 0] = c_st
            nf_ref[0, d, 0] = n_st
            mf_ref[0, d, 0] = jnp.broadcast_to(m_st, (1, mf_ref.shape[-1]))
    y = _head_norm_gate(o_ref[...], normg_ref, og_ref[...].astype(F32), jax.nn.sigmoid)
    y_ref[...] = y.astype(y_ref.dtype)


def _ml_mixer(z, zg, zgt, bias, normg, init, layer, depth, gate_lane0, geom, cols, ctx):
    nb, t_len, blk0 = _seq_geometry(geom, ctx)
    mix, heads = geom["mix"], geom["ml_heads"]
    dh = mix // heads
    has_init, want_final = init is not None, ctx
    smem = pl.BlockSpec(memory_space=pltpu.SMEM)
    in_specs, args = [smem], [bias]
    if has_init:
        in_specs.append(smem)
        args.append(init["m"])
    in_specs += [
        _col_spec(t_len, dh, cols["ml_q"], blk0),
        _col_spec(t_len, dh, cols["ml_k"], blk0),
        _col_spec(t_len, dh, cols["ml_v"], blk0),
        _col_spec(t_len, dh, cols["ml_o"], blk0),
        pl.BlockSpec((t_len, LANES), lambda b, h: (blk0 + b, 0)),
        pl.BlockSpec((zgt.shape[0], t_len), lambda b, h: (0, blk0 + b)),
        pl.BlockSpec((1, dh), lambda b, h: (0, h)),
    ]
    args += [z, z, z, z, zg, zgt, normg]
    if has_init:
        in_specs += [pl.BlockSpec((1, 2, 1, dh, dh), lambda b, h: (b, 0, h, 0, 0)),
                     pl.BlockSpec((1, 2, 1, 1, dh), lambda b, h: (b, 0, h, 0, 0))]
        args += [init["c"], init["n"]]
    x_specs, x_shapes = [], []
    if want_final:
        x_specs = [pl.BlockSpec((1, 2, 1, dh, dh), lambda b, h: (b, 0, h, 0, 0)),
                   pl.BlockSpec((1, 2, 1, 1, dh), lambda b, h: (b, 0, h, 0, 0)),
                   pl.BlockSpec((1, 2, 1, 1, LANES), lambda b, h: (b, 0, h, 0, 0))]
        x_shapes = [jax.ShapeDtypeStruct((nb, 2, heads, dh, dh), F32),
                    jax.ShapeDtypeStruct((nb, 2, heads, 1, dh), F32),
                    jax.ShapeDtypeStruct((nb, 2, heads, 1, LANES), F32)]
    res = _launch_mixer(_ml_kernel, "mlstm_ctx" if ctx else "mlstm_lat", geom, ctx, dh, in_specs, args,
                        x_specs, x_shapes, [pltpu.VMEM((t_len, dh), F32)],
                        t_len=t_len, dh=dh, n_heads=heads, layer=layer, depth=depth, gate_lane0=gate_lane0,
                        has_init=has_init, want_final=want_final)
    return (res[0], res[1:]) if want_final else (res[0], None)


def _merge_kernel(*refs, nbr, ctx_tiles):
    h_ref = refs[0]
    y_refs = refs[1:1 + 2 * nbr]
    m_refs = refs[1 + 2 * nbr:1 + 3 * nbr]
    wb_ref, o_ref = refs[1 + 3 * nbr], refs[2 + 3 * nbr]
    is_ctx = pl.program_id(0) < ctx_tiles
    h = h_ref[...]
    acc = None
    for i in range(nbr):
        y = jnp.where(is_ctx, y_refs[2 * i][...], y_refs[2 * i + 1][...])
        term = jax.nn.sigmoid(_dot(h, m_refs[i][...])) * _dot(y, wb_ref[i])
        acc = term if acc is None else acc + term
    o_ref[...] = acc.astype(o_ref.dtype)


def _merge(h, y_pairs, w_merge, w_branch, geom):
    n, d = h.shape
    mix = geom["mix"]
    nbr = len(y_pairs)
    tm = _pick(math.gcd(geom["n_ctx"], n - geom["n_ctx"]), (512, 256, 128))
    tn = _pick(d, (256, 128))
    nj = d // tn
    ctx_tiles = geom["n_ctx"] // tm
    in_specs = [pl.BlockSpec((tm, d), lambda i, j: (i, 0))]
    for _ in range(nbr):
        in_specs.append(pl.BlockSpec((tm, mix), lambda i, j: (jnp.minimum(i, ctx_tiles - 1), 0)))
        in_specs.append(pl.BlockSpec((tm, mix), lambda i, j: (jnp.maximum(i - ctx_tiles, 0), 0)))
    in_specs += [pl.BlockSpec((d, tn), lambda i, j, br=br: (0, br * nj + j)) for br in range(nbr)]
    in_specs += [pl.BlockSpec((nbr, mix, tn), lambda i, j: (0, 0, j))]
    flat = [y for pair in y_pairs for y in pair]
    return pl.pallas_call(
        functools.partial(_merge_kernel, nbr=nbr, ctx_tiles=ctx_tiles),
        grid=(n // tm, nj),
        in_specs=in_specs,
        out_specs=pl.BlockSpec((tm, tn), lambda i, j: (i, j)),
        out_shape=jax.ShapeDtypeStruct((n, d), BF16),
        compiler_params=_cparams(("arbitrary", "arbitrary")),
        name="merge",
    )(h, *flat, *([w_merge] * nbr), w_branch)


def _outproj_kernel(m_ref, w_ref, x_ref, g_ref, o_ref):
    o_ref[...] = x_ref[...] + g_ref[0, 0] * _dot(m_ref[...], w_ref[...])


def _outproj(merged, w_out, x, mod4, comp_gate, geom):
    n, d = x.shape
    tm = _pick(n, (512, 256, 128))
    tn = _pick(d, (512, 256, 128))
    n_ctx, dec_seq, ctx_row = geom["n_ctx"], geom["dec_seq"], geom["ctx_row"]

    def gate_map(i, j):
        start = i * tm
        return (jnp.where(start < n_ctx, ctx_row, (start - n_ctx) // dec_seq), comp_gate, 0, j)

    return pl.pallas_call(
        _outproj_kernel,
        grid=(n // tm, d // tn),
        in_specs=[
            pl.BlockSpec((tm, d), lambda i, j: (i, 0)),
            pl.BlockSpec((d, tn), lambda i, j: (0, j)),
            pl.BlockSpec((tm, tn), lambda i, j: (i, j)),
            pl.BlockSpec((1, 1, 1, tn), gate_map),
        ],
        out_specs=pl.BlockSpec((tm, tn), lambda i, j: (i, j)),
        out_shape=jax.ShapeDtypeStruct((n, d), F32),
        compiler_params=_cparams(("arbitrary", "arbitrary")),
        name="out_proj",
    )(merged, w_out, x, mod4)


def _first_argmax(vals):
    best, idx = vals[0], jnp.zeros_like(vals[0])
    for j in range(1, len(vals)):
        better = vals[j] > best
        idx = jnp.where(better, float(j), idx)
        best = jnp.where(better, vals[j], best)
    return idx, best


def _pick_row(vals, idx):
    out = vals[0]
    for j in range(1, len(vals)):
        out = jnp.where(idx == float(j), vals[j], out)
    return out


def _router_kernel(x_ref, g_ref, sh_ref, sc_ref, wrt_ref, rb_ref, h_ref, eidx_ref, wgt_ref, rank_ref, cnt_ref,
                   carry_ref, *, n_exp, n_groups):
    i = pl.program_id(0)
    x = x_ref[...]
    y = x * lax.rsqrt(jnp.mean(x * x, axis=-1, keepdims=True) + EPS)
    h = (y * g_ref[...]) * (1.0 + sc_ref[0, 0]) + sh_ref[0, 0]
    h_ref[...] = _pack_bf16_pairs(h)
    tm = x.shape[0]
    per = n_exp // n_groups

    logits = lax.dot_general(wrt_ref[...], h, (((1,), (1,)), ((), ())), precision=HIGHEST,
                             preferred_element_type=F32)
    scores = jax.nn.sigmoid(logits)
    sel = scores + rb_ref[...]
    sel_rows = [sel[e:e + 1, :] for e in range(n_exp)]
    sc_rows = [scores[e:e + 1, :] for e in range(n_exp)]
    group_scores = []
    for g in range(n_groups):
        r = sel_rows[g * per:(g + 1) * per]
        pair = None
        for a in range(per):
            for bb in range(a + 1, per):
                s2 = r[a] + r[bb]
                pair = s2 if pair is None else jnp.maximum(pair, s2)
        group_scores.append(pair)
    g_best, _ = _first_argmax(group_scores)
    in_sel = [_pick_row([sel_rows[g * per + j] for g in range(n_groups)], g_best) for j in range(per)]
    in_sc = [_pick_row([sc_rows[g * per + j] for g in range(n_groups)], g_best) for j in range(per)]
    j1, _ = _first_argmax(in_sel)
    masked = [jnp.where(j1 == float(j), -jnp.inf, in_sel[j]) for j in range(per)]
    j2, _ = _first_argmax(masked)
    w1 = _pick_row(in_sc, j1)
    w2 = _pick_row(in_sc, j2)
    tot = w1 + w2
    e1 = (g_best * per + j1).astype(jnp.int32)
    e2 = (g_best * per + j2).astype(jnp.int32)
    eidx_ref[...] = jnp.concatenate([e1, e2], axis=0)
    wgt_ref[...] = jnp.concatenate([w1 / tot, w2 / tot], axis=0)

    @pl.when(i == 0)
    def _():
        carry_ref[...] = jnp.zeros_like(carry_ref)

    eio = _iota2((n_exp, tm), 0)
    hit1 = eio == e1
    hit2 = eio == e2
    onehot = jnp.where(hit1 | hit2, 1.0, 0.0)
    before = (_iota2((tm, tm), 0) < _iota2((tm, tm), 1)).astype(BF16)
    prefix = _dot(onehot.astype(BF16), before) + carry_ref[:, 0:1]
    r1 = jnp.sum(jnp.where(hit1, prefix, 0.0), axis=0, keepdims=True)
    r2 = jnp.sum(jnp.where(hit2, prefix, 0.0), axis=0, keepdims=True)
    rank_ref[...] = jnp.concatenate([r1, r2], axis=0).astype(jnp.int32)
    carry_ref[...] = carry_ref[...] + jnp.sum(onehot, axis=1, keepdims=True)
    cnt_ref[...] = carry_ref[...]


def _norm_router(x, g, mod4, comp_shift, comp_scale, w_router_t, router_bias, geom):
    n, d = x.shape
    n_exp = w_router_t.shape[0]
    tm = 256
    row2 = pl.BlockSpec((2, tm), lambda i: (0, i))
    return pl.pallas_call(
        functools.partial(_router_kernel, n_exp=n_exp, n_groups=N_EXPERT_GROUPS),
        grid=(n // tm,),
        in_specs=[
            pl.BlockSpec((tm, d), lambda i: (i, 0)),
            pl.BlockSpec((1, d), lambda i: (0, 0)),
            _mod_spec(d, tm, geom, comp_shift),
            _mod_spec(d, tm, geom, comp_scale),
            pl.BlockSpec((n_exp, d), lambda i: (0, 0)),
            pl.BlockSpec((n_exp, 1), lambda i: (0, 0)),
        ],
        out_specs=[pl.BlockSpec((tm, d // 2), lambda i: (i, 0)), row2, row2, row2,
                   pl.BlockSpec((n_exp, LANES), lambda i: (0, 0))],
        out_shape=[jax.ShapeDtypeStruct((n, d // 2), jnp.uint32),
                   jax.ShapeDtypeStruct((2, n), jnp.int32),
                   jax.ShapeDtypeStruct((2, n), F32),
                   jax.ShapeDtypeStruct((2, n), jnp.int32),
                   jax.ShapeDtypeStruct((n_exp, LANES), F32)],
        scratch_shapes=[pltpu.VMEM((n_exp, LANES), F32)],
        compiler_params=_cparams(("arbitrary",)),
        name="norm_router",
    )(x, g.reshape(1, d), mod4, mod4, w_router_t, router_bias.reshape(n_exp, 1))


def _row_copy(src_hbm, row, dst_vmem, r, sem):
    return pltpu.make_async_copy(src_hbm.at[pl.ds(row, 1), :], dst_vmem.at[pl.ds(r, 1), :], sem)


def _dispatch_kernel(src_ref, h_hbm, o_ref, sem):
    tg = o_ref.shape[0]

    def start(i, c):
        for u in range(2):
            r = 2 * i + u
            _row_copy(h_hbm, src_ref[0, 0, r], o_ref, r, sem).start(priority=u)
        return c

    def wait(r, c):
        _row_copy(h_hbm, 0, o_ref, r, sem).wait()
        return c

    lax.fori_loop(0, tg // 2, start, 0, unroll=4)
    lax.fori_loop(0, tg, wait, 0, unroll=8)


def _dispatch(h, src):
    p_rows = src.shape[0]
    dw = h.shape[1]
    tg = ROUTE_TILE
    return pl.pallas_call(
        _dispatch_kernel,
        grid=(p_rows // tg,),
        in_specs=[
            pl.BlockSpec((1, 1, tg), lambda i: (i, 0, 0), memory_space=pltpu.SMEM),
            pl.BlockSpec(memory_space=pl.ANY),
        ],
        out_specs=pl.BlockSpec((tg, dw), lambda i: (i, 0)),
        out_shape=jax.ShapeDtypeStruct((p_rows, dw), h.dtype),
        scratch_shapes=[pltpu.SemaphoreType.DMA(())],
        compiler_params=_cparams(("arbitrary",)),
        name="moe_dispatch",
    )(src.reshape(p_rows // tg, 1, tg), h)


def _new_expert(te_ref, t):
    return (t == 0) | (te_ref[t] != te_ref[jnp.maximum(t - 1, 0)])


def _ffn_up_kernel(te_ref, tv_ref, x_ref, wg_ref, wu_ref, o_ref, wgb_ref, wub_ref):
    t = pl.program_id(1)

    @pl.when(_new_expert(te_ref, t))
    def _():
        wgb_ref[...] = wg_ref[0].astype(BF16)
        wub_ref[...] = wu_ref[0].astype(BF16)

    @pl.when(tv_ref[t] == 1)
    def _():
        lo, hi = _unpack_bf16_pairs(x_ref[...])
        lo, hi = lo.astype(BF16), hi.astype(BF16)
        half = lo.shape[1]
        g = _dot(lo, wgb_ref[:half, :]) + _dot(hi, wgb_ref[half:, :])
        u = _dot(lo, wub_ref[:half, :]) + _dot(hi, wub_ref[half:, :])
        o_ref[...] = (_silu(g) * u).astype(o_ref.dtype)

    @pl.when(tv_ref[t] == 0)
    def _():
        o_ref[...] = jnp.zeros_like(o_ref)


def _ffn_up(xs, w_gate, w_up, layer, tile_expert, tile_valid):
    p_rows, dw = xs.shape
    d, f = w_gate.shape[2], w_gate.shape[3]
    tm = ROUTE_TILE
    tf = _pick(f, (512, 256, 128))
    w_spec = pl.BlockSpec((None, 1, d, tf), lambda j, t, te, tv: (layer, te[t], 0, j))
    grid_spec = pltpu.PrefetchScalarGridSpec(
        num_scalar_prefetch=2,
        grid=(f // tf, p_rows // tm),
        in_specs=[pl.BlockSpec((tm, dw), lambda j, t, te, tv: (t, 0)), w_spec, w_spec],
        out_specs=pl.BlockSpec((tm, tf), lambda j, t, te, tv: (t, j)),
        scratch_shapes=[pltpu.VMEM((d, tf), BF16), pltpu.VMEM((d, tf), BF16)],
    )
    return pl.pallas_call(
        _ffn_up_kernel,
        grid_spec=grid_spec,
        out_shape=jax.ShapeDtypeStruct((p_rows, f), BF16),
        compiler_params=_cparams(("arbitrary", "arbitrary")),
        name="moe_ffn_up",
    )(tile_expert, tile_valid, xs, w_gate, w_up)


def _ffn_down_kernel(te_ref, tv_ref, h_ref, wd_ref, o_ref, wdb_ref):
    t = pl.program_id(1)

    @pl.when(_new_expert(te_ref, t))
    def _():
        wdb_ref[...] = wd_ref[0].astype(BF16)

    @pl.when(tv_ref[t] == 1)
    def _():
        o_ref[...] = _pack_bf16_pairs(_dot(h_ref[...], wdb_ref[...]))

    @pl.when(tv_ref[t] == 0)
    def _():
        o_ref[...] = jnp.zeros_like(o_ref)


def _ffn_down_tile(d):
    return _pick(d, (2048, 1024, 512, 256))


def _ffn_down(hmid, w_down, layer, tile_expert, tile_valid):
    p_rows, f = hmid.shape
    d = w_down.shape[3]
    tm = ROUTE_TILE
    tn = _ffn_down_tile(d)
    grid_spec = pltpu.PrefetchScalarGridSpec(
        num_scalar_prefetch=2,
        grid=(d // tn, p_rows // tm),
        in_specs=[
            pl.BlockSpec((tm, f), lambda j, t, te, tv: (t, 0)),
            pl.BlockSpec((None, 1, f, tn), lambda j, t, te, tv: (layer, te[t], 0, j)),
        ],
        out_specs=pl.BlockSpec((tm, tn // 2), lambda j, t, te, tv: (t, j)),
        scratch_shapes=[pltpu.VMEM((f, tn), BF16)],
    )
    return pl.pallas_call(
        _ffn_down_kernel,
        grid_spec=grid_spec,
        out_shape=jax.ShapeDtypeStruct((p_rows, d // 2), jnp.uint32),
        compiler_params=_cparams(("arbitrary", "arbitrary")),
        name="moe_ffn_down",
    )(tile_expert, tile_valid, hmid, w_down)


def _combine_kernel(*refs, final_norm, ctx_tiles, tn):
    pos_ref, x_ref, wt_ref, g_ref, fg_ref, ys_hbm = refs[:6]
    out_refs = refs[6:-2]
    buf_ref, sem = refs[-2:]
    tm = x_ref.shape[0]

    def start(r, c):
        for s in range(2):
            _row_copy(ys_hbm, pos_ref[0, s, r], buf_ref.at[s], r, sem).start(priority=s)
        return c

    def wait(r, c):
        for s in range(2):
            _row_copy(ys_hbm, 0, buf_ref.at[s], r, sem).wait()
        return c

    lax.fori_loop(0, tm, start, 0, unroll=4)
    lax.fori_loop(0, tm, wait, 0, unroll=4)
    w = wt_ref[...]
    lane = _iota2((1, w.shape[1]), 1)
    w0 = jnp.sum(jnp.where(lane == 0, w, 0.0), axis=1, keepdims=True)
    w1 = jnp.sum(jnp.where(lane == 1, w, 0.0), axis=1, keepdims=True)
    pieces = []
    for j in range(2 * buf_ref.shape[2] // tn):
        seg = slice(j * tn // 2, (j + 1) * tn // 2)
        lo0, hi0 = _unpack_bf16_pairs(buf_ref[0, :, seg])
        lo1, hi1 = _unpack_bf16_pairs(buf_ref[1, :, seg])
        pieces += [w0 * lo0 + w1 * lo1, w0 * hi0 + w1 * hi1]
    x = x_ref[...] + g_ref[0, 0] * jnp.concatenate(pieces, axis=1)
    if final_norm:
        x = (x * lax.rsqrt(jnp.mean(x * x, axis=-1, keepdims=True) + EPS)) * fg_ref[...]
    if len(out_refs) == 1:
        out_refs[0][...] = x
    else:
        is_ctx = pl.program_id(0) < ctx_tiles

        @pl.when(is_ctx)
        def _():
            out_refs[0][...] = x

        @pl.when(jnp.logical_not(is_ctx))
        def _():
            out_refs[1][...] = x


def _combine(x, ys, pos, wgt_cols, mod4, comp_gate, final_g, geom, final_norm, split_out):
    n, d = x.shape
    tm = ROUTE_TILE
    ctx_tiles = geom["n_ctx"] // tm
    if split_out:
        out_specs = [pl.BlockSpec((tm, d), lambda i: (jnp.minimum(i, ctx_tiles - 1), 0)),
                     pl.BlockSpec((tm, d), lambda i: (jnp.maximum(i - ctx_tiles, 0), 0))]
        out_shape = [jax.ShapeDtypeStruct((geom["n_ctx"], d), F32),
                     jax.ShapeDtypeStruct((n - geom["n_ctx"], d), F32)]
    else:
        out_specs = pl.BlockSpec((tm, d), lambda i: (i, 0))
        out_shape = jax.ShapeDtypeStruct((n, d), F32)
    return pl.pallas_call(
        functools.partial(_combine_kernel, final_norm=final_norm, ctx_tiles=ctx_tiles, tn=_ffn_down_tile(d)),
        grid=(n // tm,),
        in_specs=[
            pl.BlockSpec((1, 2, tm), lambda i: (i, 0, 0), memory_space=pltpu.SMEM),
            pl.BlockSpec((tm, d), lambda i: (i, 0)),
            pl.BlockSpec((tm, wgt_cols.shape[1]), lambda i: (i, 0)),
            _mod_spec(d, tm, geom, comp_gate),
            pl.BlockSpec((1, d), lambda i: (0, 0)),
            pl.BlockSpec(memory_space=pl.ANY),
        ],
        out_specs=out_specs,
        out_shape=out_shape,
        scratch_shapes=[pltpu.VMEM((2, tm, d // 2), jnp.uint32), pltpu.SemaphoreType.DMA(())],
        compiler_params=_cparams(("arbitrary",)),
        name="moe_combine",
    )(pos, x, wgt_cols, mod4, final_g.reshape(1, d), ys)


def _route_plan(eidx, rank, counts, n_exp):
    n = eidx.shape[1]
    tile = ROUTE_TILE
    n_tiles = (2 * n) // tile + n_exp
    cnt = counts.astype(jnp.int32)
    padded = ((cnt + tile - 1) // tile) * tile
    e_ids = jnp.arange(n_exp, dtype=jnp.int32)
    ends = jnp.sum(jnp.where(e_ids[None, :] <= e_ids[:, None], padded[None, :], 0), axis=1)
    offs = ends - padded
    pos = jnp.sum(jnp.where(eidx[:, :, None] == e_ids, offs, 0), axis=-1) + rank
    tile_start = jnp.arange(n_tiles, dtype=jnp.int32) * tile
    tile_expert = jnp.minimum(jnp.sum((ends[None, :] <= tile_start[:, None]).astype(jnp.int32), axis=1), n_exp - 1)
    tile_valid = (tile_start < ends[-1]).astype(jnp.int32)
    tok = jnp.broadcast_to(jnp.arange(n, dtype=jnp.int32), (2, n))
    src = jnp.zeros((n_tiles * tile,), jnp.int32).at[pos.reshape(-1)].set(tok.reshape(-1))
    pos_tiles = pos.reshape(2, n // tile, tile).transpose(1, 0, 2)
    return pos_tiles, src, tile_expert, tile_valid


def _s5_params(lam_re, lam_im, log_step, b_re, b_im, c_re, c_im):
    n_dir, groups, p = lam_re.shape
    gch = b_re.shape[-1]
    gps = S5_SLAB // gch
    n_slab = groups // gps
    step = jnp.exp(log_step)[..., None]
    mag = jnp.exp(lam_re * step)
    a_re = mag * jnp.cos(lam_im * step)
    a_im = mag * jnp.sin(lam_im * step)
    den = lam_re * lam_re + lam_im * lam_im
    z_re = ((a_re - 1.0) * lam_re + a_im * lam_im) / den
    z_im = (a_im * lam_re - (a_re - 1.0) * lam_im) / den
    bb_re = z_re[..., None] * b_re - z_im[..., None] * b_im
    bb_im = z_re[..., None] * b_im + z_im[..., None] * b_re
    eye = jnp.eye(gps, dtype=F32)
    bbs = jnp.stack([bb_re, bb_im], axis=2).reshape(n_dir, n_slab, gps, 2, p, gch)
    bm = jnp.einsum("dsgrpc,gh->dsgcrhp", bbs, eye).reshape(n_dir, n_slab, S5_SLAB, 2 * gps * p)
    cs = jnp.stack([c_re, -c_im], axis=0).reshape(2, n_slab, gps, gch, p)
    cm = jnp.einsum("rsgcp,gh->srgphc", cs, eye).reshape(n_slab, 2 * gps * p, S5_SLAB)
    a = jnp.stack([a_re, a_im], axis=2).reshape(n_dir, n_slab, gps, 2, p).transpose(0, 1, 3, 2, 4)
    a = a.reshape(n_dir, n_slab, 1, 2 * gps * p)
    a8 = jnp.broadcast_to(a, (n_dir, n_slab, SUBLANES, 2 * gps * p))
    return bm.astype(BF16), cm.astype(BF16), a8


def _axial_rope(t, dh):
    rows = t // GRID_W
    row = jnp.repeat(jnp.arange(rows, dtype=F32), GRID_W)
    col = (jnp.arange(rows * GRID_W) % GRID_W).astype(F32)
    n_freq = dh // 4
    inv = ROPE_BASE ** (-jnp.arange(n_freq, dtype=F32) / n_freq)
    ang = jnp.concatenate([row[:, None] * inv, col[:, None] * inv], axis=-1)
    return jnp.cos(ang), jnp.sin(ang)


def _to_time_major(u, nb, t_len, bp):
    u = u.reshape(nb, t_len, -1).transpose(1, 0, 2)
    if bp != nb:
        u = jnp.pad(u, ((0, 0), (0, bp - nb), (0, 0)))
    return u.reshape(t_len * bp, -1)


def _from_time_major(y, nb, t_len, bp):
    return y.reshape(t_len, bp, -1)[:, :nb].transpose(1, 0, 2).reshape(nb * t_len, -1)


def _s5_state_to_slabs(s_re, s_im, n_slab, bp):
    nb = s_re.shape[0]
    st = jnp.concatenate([s_re.reshape(nb, 2, n_slab, -1), s_im.reshape(nb, 2, n_slab, -1)], axis=-1)
    st = st.transpose(1, 2, 0, 3)
    return jnp.pad(st, ((0, 0), (0, 0), (0, bp - nb), (0, 0)))


def _s5_slabs_to_state(xf, nb, groups, p):
    half = xf.shape[-1] // 2
    re = xf[:, :, :nb, :half].transpose(2, 0, 1, 3).reshape(nb, 2, groups, p)
    im = xf[:, :, :nb, half:].transpose(2, 0, 1, 3).reshape(nb, 2, groups, p)
    return re, im


def _round_up(x, m):
    return (x + m - 1) // m * m


def kernel(x_prompt, x_sample, c, c_ctx, state_s5_re, state_s5_im, state_gla, state_ml_c, state_ml_n, state_ml_m, state_ret, w_ada, b_ada, norm1_g, norm2_g, w_in, s5_lambda_re, s5_lambda_im, s5_log_step, s5_b_re, s5_b_im, s5_c_re, s5_c_im, s5_d, s5_w_glu, gla_w_a, gla_b_a, gla_norm_g, ml_i_bias, ml_f_bias, ml_norm_g, ret_decay_logit, ret_norm_g, w_branch, w_out, w_router, router_bias, w_exp_gate, w_exp_up, w_exp_down, final_g):
    batch, seq, d = x_prompt.shape
    dec_batch, dec_seq, _ = x_sample.shape
    depth = w_in.shape[0]
    mix = s5_d.shape[-1]
    n_exp = w_router.shape[1]
    gla_heads = state_gla.shape[3]
    ml_heads = ml_i_bias.shape[-1]
    ret_heads = ret_decay_logit.shape[-1]
    gla_rank = gla_w_a.shape[2]
    groups, p_state = s5_lambda_re.shape[2], s5_lambda_re.shape[3]
    n_ctx, n_lat = batch * seq, dec_batch * dec_seq
    n = n_ctx + n_lat
    assert dec_batch + 1 <= SUBLANES and n_ctx % dec_seq == 0 and dec_seq % seq == 0
    geom = dict(batch=batch, seq=seq, dec_batch=dec_batch, dec_seq=dec_seq, n_ctx=n_ctx, ctx_row=dec_batch,
                mix=mix, gla_heads=gla_heads, ml_heads=ml_heads, ret_heads=ret_heads)
    n_slab = mix // S5_SLAB
    bp_ctx, bp_lat = _round_up(batch, SUBLANES), _round_up(dec_batch, SUBLANES)

    half = mix // 2
    widths = [("s5_u", mix), ("gla_q", half), ("gla_k", half), ("gla_v", mix), ("gla_g", mix), ("gla_r", gla_rank),
              ("ml_q", mix), ("ml_k", mix), ("ml_v", mix), ("ml_o", mix), ("ml_if", 4 * ml_heads),
              ("ret_q", mix), ("ret_k", mix), ("ret_v", mix), ("ret_g", mix), ("merge", 4 * d)]
    src_off, o = {}, 0
    for name, w in widths:
        src_off[name] = (o, w)
        o += w
    main_names = [nm for nm, _ in widths if nm not in ("gla_r", "ml_if", "merge")]
    cols, o = {}, 0
    for nm in main_names:
        cols[nm] = o
        o += src_off[nm][1]

    def w_cols(wl, names):
        return jnp.concatenate([wl[:, src_off[nm][0]:src_off[nm][0] + src_off[nm][1]] for nm in names], axis=1)

    x = jnp.concatenate([x_prompt.reshape(n_ctx, d), x_sample.reshape(n_lat, d)], axis=0)
    c8 = jnp.zeros((SUBLANES, d), F32).at[:dec_batch].set(c).at[dec_batch].set(c_ctx)
    mod = _ada(c8, w_ada, b_ada)
    rope_tabs = _axial_rope(dec_seq, mix // ret_heads)
    w_router_t = w_router.T

    ctx_states = []
    for l in range(depth):
        mod4 = mod[l].reshape(SUBLANES, N_MOD, 1, d)
        wl = w_in[l]
        w_main = w_cols(wl, main_names).astype(BF16)
        gate_w = w_cols(wl, ["gla_r", "ml_if"])
        w_gate = jnp.pad(gate_w, ((0, 0), (0, LANES - gate_w.shape[1]))).astype(BF16)
        w_merge = wl[:, src_off["merge"][0]:].astype(BF16)

        h = _norm_mod(x, norm1_g[l], mod4, 0, 1, geom, BF16)
        z = _matmul(h, w_main, BF16, _pick(n, (1024, 512, 256)), _pick(w_main.shape[1], (512, 256, 128)), "in_proj")
        zg = _matmul(h, w_gate, F32, _pick(n, (1024, 512, 256)), LANES, "gate_proj")
        zgt = zg[:, gla_rank:gla_rank + 4 * ml_heads].T

        bm, cm, a8 = _s5_params(s5_lambda_re[l], s5_lambda_im[l], s5_log_step[l], s5_b_re[l], s5_b_im[l],
                                s5_c_re[l], s5_c_im[l])
        dvec = s5_d[l].reshape(1, mix)
        u = z[:, :mix]
        y_c, xf = _s5_mixer(_to_time_major(u[:n_ctx], batch, seq, bp_ctx), bm, cm, a8, dvec, None, seq, bp_ctx, True)
        x0 = _s5_state_to_slabs(state_s5_re[:, l], state_s5_im[:, l], n_slab, bp_lat)
        y_l, _ = _s5_mixer(_to_time_major(u[n_ctx:], dec_batch, dec_seq, bp_lat), bm, cm, a8, dvec, x0, dec_seq,
                           bp_lat, False)
        w_glu = s5_w_glu[l].astype(BF16)
        y_s5 = (_glu(_from_time_major(y_c, batch, seq, bp_ctx), w_glu),
                _glu(_from_time_major(y_l, dec_batch, dec_seq, bp_lat), w_glu))
        s5_re_l, s5_im_l = _s5_slabs_to_state(xf, batch, groups, p_state)

        dk = half // gla_heads
        wa = gla_w_a[l].reshape(2, gla_rank, gla_heads, dk).transpose(0, 2, 1, 3)
        wa = jnp.pad(wa, ((0, 0), (0, 0), (0, LANES - gla_rank), (0, 0)))
        ba = gla_b_a[l].reshape(2, gla_heads, 1, dk)
        gp = dict(wa=wa, ba=ba, normg=gla_norm_g[l].reshape(1, mix))
        y_gla_c, gla_l = _gla_mixer(z, zg, gp, None, geom, cols, True)
        y_gla_l, _ = _gla_mixer(z, zg, gp, state_gla[:, l], geom, cols, False)

        ml_bias = jnp.stack([ml_i_bias[l], ml_f_bias[l]], axis=0).reshape(-1)
        ml_ng = ml_norm_g[l].reshape(1, mix)
        y_ml_c, ml_l = _ml_mixer(z, zg, zgt, ml_bias, ml_ng, None, l, depth, gla_rank, geom, cols, True)
        ml_init = dict(c=state_ml_c[:, l], n=state_ml_n[:, l][:, :, :, None, :], m=state_ml_m.reshape(-1))
        y_ml_l, _ = _ml_mixer(z, zg, zgt, ml_bias, ml_ng, ml_init, l, depth, gla_rank, geom, cols, False)

        lg = jax.nn.log_sigmoid(ret_decay_logit[l]).reshape(-1)
        ret_ng = ret_norm_g[l].reshape(1, mix)
        y_ret_c, ret_l = _ret_mixer(z, lg, ret_ng, None, None, geom, cols, True)
        y_ret_l, _ = _ret_mixer(z, lg, ret_ng, rope_tabs, state_ret[:, l], geom, cols, False)

        y_pairs = [y_s5, (y_gla_c, y_gla_l), (y_ml_c, y_ml_l), (y_ret_c, y_ret_l)]
        merged = _merge(h, y_pairs, w_merge, w_branch[l].astype(BF16), geom)
        x = _outproj(merged, w_out[l].astype(BF16), x, mod4, 2, geom)
        ctx_states.append((s5_re_l, s5_im_l, gla_l, ml_l[0], ml_l[1][:, :, :, 0, :], ml_l[2][:, :, :, 0, 0], ret_l))

        h2, eidx, wgt, rank, counts = _norm_router(x, norm2_g[l], mod4, 3, 4, w_router_t, router_bias, geom)
        pos_tiles, src, tile_expert, tile_valid = _route_plan(eidx, rank, counts[:, 0], n_exp)
        xs = _dispatch(h2, src)
        hmid = _ffn_up(xs, w_exp_gate, w_exp_up, l, tile_expert, tile_valid)
        ys = _ffn_down(hmid, w_exp_down, l, tile_expert, tile_valid)
        last = l == depth - 1
        x = _combine(x, ys, pos_tiles, wgt.T, mod4, 5, final_g, geom, final_norm=last, split_out=last)

    y_prompt = x[0].reshape(batch, seq, d)
    y_sample = x[1].reshape(dec_batch, dec_seq, d)
    stacked = [jnp.stack([st[i] for st in ctx_states], axis=1) for i in range(7)]
    return (y_prompt, y_sample, *stacked)
```

```python
import functools
import math

import jax
import jax.numpy as jnp
from jax import lax
from jax.experimental import pallas as pl
from jax.experimental.pallas import tpu as pltpu

F32 = jnp.float32
BF16 = jnp.bfloat16
HIGHEST = lax.Precision.HIGHEST

EPS = 1e-6
GRID_W = 64
ROPE_BASE = 10000.0
GLA_TAU = 16.0
N_EXPERT_GROUPS = 4
N_MOD = 6

LANES = 128
SUBLANES = 8
S5_SLAB = LANES
GLA_CHUNK = 64
SEQ_CHUNK = 256
VMEM_LIMIT_BYTES = 56 * 1024 * 1024
ROUTE_TILE = 256


def _cparams(sem):
    return pltpu.CompilerParams(dimension_semantics=sem, vmem_limit_bytes=VMEM_LIMIT_BYTES)


def _pick(n, cands):
    for c in cands:
        if c <= n and n % c == 0:
            return c
    return n


def _nt(a, b):
    return lax.dot_general(a, b, (((1,), (1,)), ((), ())), preferred_element_type=F32)


def _tn(a, b):
    return lax.dot_general(a, b, (((0,), (0,)), ((), ())), preferred_element_type=F32)


def _dot(a, b):
    return jnp.dot(a, b, preferred_element_type=F32)


def _log_sigmoid(x):
    return jnp.minimum(x, 0.0) - jnp.log1p(jnp.exp(-jnp.abs(x)))


def _silu(x):
    return x * jax.nn.sigmoid(x)


def _iota2(shape, dim):
    return lax.broadcasted_iota(jnp.int32, shape, dim)


HI16 = 0xFFFF0000


def _pack_bf16_pairs(x):
    c = x.shape[1] // 2
    bits = lax.bitcast_convert_type(x.astype(BF16).astype(F32), jnp.uint32)
    return (bits[:, :c] >> 16) | (bits[:, c:] & jnp.uint32(HI16))


def _unpack_bf16_pairs(w):
    lo = lax.bitcast_convert_type(w << 16, F32)
    hi = lax.bitcast_convert_type(w & jnp.uint32(HI16), F32)
    return lo, hi


def _ada_kernel(c_ref, w_ref, b_ref, o_ref):
    s = _silu(c_ref[...]).astype(BF16)
    o_ref[0] = _dot(s, w_ref[0].astype(BF16)) + b_ref[0]


def _ada(c8, w_ada, b_ada):
    depth, d, n = w_ada.shape
    tn = _pick(n, (512, 256, 128))
    return pl.pallas_call(
        _ada_kernel,
        grid=(depth, n // tn),
        in_specs=[
            pl.BlockSpec((SUBLANES, d), lambda l, j: (0, 0)),
            pl.BlockSpec((1, d, tn), lambda l, j: (l, 0, j)),
            pl.BlockSpec((1, 1, tn), lambda l, j: (l, 0, j)),
        ],
        out_specs=pl.BlockSpec((1, SUBLANES, tn), lambda l, j: (l, 0, j)),
        out_shape=jax.ShapeDtypeStruct((depth, SUBLANES, n), F32),
        compiler_params=_cparams(("arbitrary", "arbitrary")),
        name="ada_mod",
    )(c8, w_ada, b_ada.reshape(depth, 1, n))


def _mod_spec(d, tm, geom, comp):
    n_ctx, dec_seq, ctx_row = geom["n_ctx"], geom["dec_seq"], geom["ctx_row"]

    def index_map(i, *_):
        start = i * tm
        row = jnp.where(start < n_ctx, ctx_row, (start - n_ctx) // dec_seq)
        return (row, comp, 0, 0)

    return pl.BlockSpec((1, 1, 1, d), index_map)


def _row_specs(x, block, ctx_tiles, col=lambda *g: 0):
    if not isinstance(x, tuple):
        return [pl.BlockSpec(block, lambda i, *g: (i, col(i, *g)))], [x]
    return ([pl.BlockSpec(block, lambda i, *g: (jnp.minimum(i, ctx_tiles - 1),
                                                jnp.where(i < ctx_tiles, col(i, *g), 0))),
             pl.BlockSpec(block, lambda i, *g: (jnp.maximum(i - ctx_tiles, 0),
                                                jnp.where(i >= ctx_tiles, col(i, *g), 0)))], list(x))


def _read_rows(refs, ctx_tiles):
    if len(refs) == 1:
        return refs[0][...]
    return jnp.where(pl.program_id(0) < ctx_tiles, refs[0][...], refs[1][...])


def _norm_kernel(*refs, n_x, ctx_tiles):
    g_ref, sh_ref, sc_ref, o_ref = refs[n_x:]
    x = _read_rows(refs[:n_x], ctx_tiles)
    y = x * lax.rsqrt(jnp.mean(x * x, axis=-1, keepdims=True) + EPS)
    h = (y * g_ref[...]) * (1.0 + sc_ref[0, 0]) + sh_ref[0, 0]
    o_ref[...] = h.astype(o_ref.dtype)


def _norm_mod(x, g, mod4, comp_shift, comp_scale, geom, out_dtype):
    d = g.shape[0]
    n = geom["n"]
    tm = 256
    ctx_tiles = geom["n_ctx"] // tm
    x_specs, x_args = _row_specs(x, (tm, d), ctx_tiles)
    return pl.pallas_call(
        functools.partial(_norm_kernel, n_x=len(x_args), ctx_tiles=ctx_tiles),
        grid=(n // tm,),
        in_specs=x_specs + [
            pl.BlockSpec((1, d), lambda i: (0, 0)),
            _mod_spec(d, tm, geom, comp_shift),
            _mod_spec(d, tm, geom, comp_scale),
        ],
        out_specs=pl.BlockSpec((tm, d), lambda i: (i, 0)),
        out_shape=jax.ShapeDtypeStruct((n, d), out_dtype),
        compiler_params=_cparams(("arbitrary",)),
        name="norm_mod",
    )(*x_args, g.reshape(1, d), mod4, mod4)


def _mm_kernel(x_ref, w_ref, o_ref):
    o_ref[...] = _dot(x_ref[...], w_ref[...]).astype(o_ref.dtype)


def _matmul(x, w, out_dtype, tm, tn, name):
    m, k = x.shape
    n = w.shape[1]
    return pl.pallas_call(
        _mm_kernel,
        grid=(m // tm, n // tn),
        in_specs=[
            pl.BlockSpec((tm, k), lambda i, j: (i, 0)),
            pl.BlockSpec((k, tn), lambda i, j: (0, j)),
        ],
        out_specs=pl.BlockSpec((tm, tn), lambda i, j: (i, j)),
        out_shape=jax.ShapeDtypeStruct((m, n), out_dtype),
        compiler_params=_cparams(("arbitrary", "arbitrary")),
        name=name,
    )(x, w)


def _repack_kernel(w_ref, main_ref, gate_ref, merge_ref, *, moves, piece):
    dsts = (main_ref, gate_ref, merge_ref)
    gate_ref[...] = jnp.zeros_like(gate_ref)
    for which, s0, d0, width in moves:
        for off in range(0, width, piece):
            w = min(piece, width - off)
            lo = (s0 + off) // LANES * LANES
            hi = min(-(-(s0 + off + w) // LANES) * LANES, w_ref.shape[-1])
            win = w_ref[:, lo:hi]
            dsts[which][:, d0 + off:d0 + off + w] = win[:, s0 + off - lo:s0 + off - lo + w].astype(BF16)


def _repack_proj(w_in, layer, moves, n_main, n_merge):
    depth, d, pw = w_in.shape
    tk = _pick(d, (128, 64, 32, 16))
    return pl.pallas_call(
        functools.partial(_repack_kernel, moves=moves, piece=1024),
        grid=(d // tk,),
        in_specs=[pl.BlockSpec((None, tk, pw), lambda i: (layer, i, 0))],
        out_specs=[pl.BlockSpec((tk, n_main), lambda i: (i, 0)),
                   pl.BlockSpec((tk, LANES), lambda i: (i, 0)),
                   pl.BlockSpec((tk, n_merge), lambda i: (i, 0))],
        out_shape=[jax.ShapeDtypeStruct((d, n_main), BF16),
                   jax.ShapeDtypeStruct((d, LANES), BF16),
                   jax.ShapeDtypeStruct((d, n_merge), BF16)],
        compiler_params=_cparams(("arbitrary",)),
        name="repack_proj",
    )(w_in)


def _s5_kernel(*refs, t_len, bp, tc, has_init, want_final):
    it = iter(refs)
    u_ref, bm_ref, cm_ref, a_ref, d_ref = (next(it) for _ in range(5))
    x0_ref = next(it) if has_init else None
    y_ref = next(it)
    xf_ref = next(it) if want_final else None
    bu_ref, yacc_ref, st_ref = next(it), next(it), next(it)

    nc = t_len // tc
    rc = tc * bp
    half = bu_ref.shape[1] // 2
    for d in (0, 1):
        if has_init:
            st_ref[...] = x0_ref[d, 0]
        else:
            st_ref[...] = jnp.zeros_like(st_ref)
        bm = bm_ref[d, 0]
        ar = a_ref[d, 0, :, :half]
        ai = a_ref[d, 0, :, half:]

        def chunk_body(c, carry, d=d, bm=bm, ar=ar, ai=ai):
            cc = c if d == 0 else nc - 1 - c
            r0 = pl.multiple_of(cc * rc, rc)
            bu_ref[...] = _dot(u_ref[pl.ds(r0, rc), :], bm)
            for rt in range(bp // SUBLANES):
                rows = slice(rt * SUBLANES, (rt + 1) * SUBLANES)

                def step(t, s, rt=rt):
                    sr, si = s
                    tt = t if d == 0 else tc - 1 - t
                    row = pl.multiple_of(tt * bp + rt * SUBLANES, SUBLANES)
                    b = bu_ref[pl.ds(row, SUBLANES), :]
                    xr = ar * sr - ai * si + b[:, :half]
                    xi = ar * si + ai * sr + b[:, half:]
                    bu_ref[pl.ds(row, SUBLANES), :] = jnp.concatenate([xr, xi], axis=-1)
                    return xr, xi

                sr, si = lax.fori_loop(0, tc, step, (st_ref[rows, :half], st_ref[rows, half:]))
                st_ref[rows, :] = jnp.concatenate([sr, si], axis=-1)
            y = _dot(bu_ref[...].astype(BF16), cm_ref[0])
            if d == 0:
                yacc_ref[pl.ds(r0, rc), :] = y
            else:
                yacc_ref[pl.ds(r0, rc), :] += y
            return carry

        lax.fori_loop(0, nc, chunk_body, 0)
        if want_final:
            xf_ref[d, 0] = st_ref[...]
    y = yacc_ref[...] + d_ref[...] * u_ref[...].astype(F32)
    y_ref[...] = jax.nn.gelu(y).astype(y_ref.dtype)


def _s5_mixer(u_tm, bm, cm, a8, dvec, x0, t_len, bp, want_final):
    rows, mix = u_tm.shape
    n_slab = mix // S5_SLAB
    two_half = bm.shape[-1]
    tc = _pick(t_len, tuple(max(1, 1024 // bp) >> s for s in range(6)))
    has_init = x0 is not None
    in_specs = [
        pl.BlockSpec((rows, S5_SLAB), lambda s: (0, s)),
        pl.BlockSpec((2, 1, S5_SLAB, two_half), lambda s: (0, s, 0, 0)),
        pl.BlockSpec((1, two_half, S5_SLAB), lambda s: (s, 0, 0)),
        pl.BlockSpec((2, 1, SUBLANES, two_half), lambda s: (0, s, 0, 0)),
        pl.BlockSpec((1, S5_SLAB), lambda s: (0, s)),
    ]
    args = [u_tm, bm, cm, a8, dvec]
    if has_init:
        in_specs.append(pl.BlockSpec((2, 1, bp, two_half), lambda s: (0, s, 0, 0)))
        args.append(x0)
    out_specs = [pl.BlockSpec((rows, S5_SLAB), lambda s: (0, s))]
    out_shape = [jax.ShapeDtypeStruct((rows, mix), BF16)]
    if want_final:
        out_specs.append(pl.BlockSpec((2, 1, bp, two_half), lambda s: (0, s, 0, 0)))
        out_shape.append(jax.ShapeDtypeStruct((2, n_slab, bp, two_half), F32))
    res = pl.pallas_call(
        functools.partial(_s5_kernel, t_len=t_len, bp=bp, tc=tc, has_init=has_init, want_final=want_final),
        grid=(n_slab,),
        in_specs=in_specs,
        out_specs=out_specs,
        out_shape=out_shape,
        scratch_shapes=[
            pltpu.VMEM((tc * bp, two_half), F32),
            pltpu.VMEM((rows, S5_SLAB), F32),
            pltpu.VMEM((bp, two_half), F32),
        ],
        compiler_params=_cparams(("arbitrary",)),
        name="s5_ctx" if want_final else "s5_lat",
    )(*args)
    return (res[0], res[1]) if want_final else (res[0], None)


def _glu_kernel(y_ref, w_ref, o_ref):
    y = y_ref[...]
    z = _dot(y, w_ref[...])
    o_ref[...] = (y.astype(F32) * jax.nn.sigmoid(z)).astype(o_ref.dtype)


def _glu(y, w):
    n, mix = y.shape
    tm = _pick(n, (512, 256, 128))
    return pl.pallas_call(
        _glu_kernel,
        grid=(n // tm,),
        in_specs=[pl.BlockSpec((tm, mix), lambda i: (i, 0)), pl.BlockSpec((mix, mix), lambda i: (0, 0))],
        out_specs=pl.BlockSpec((tm, mix), lambda i: (i, 0)),
        out_shape=jax.ShapeDtypeStruct((n, mix), BF16),
        compiler_params=_cparams(("arbitrary",)),
        name="s5_glu",
    )(y, w)


def _head_norm_gate(o, normg_ref, gate, act):
    of = o * lax.rsqrt(jnp.mean(o * o, axis=-1, keepdims=True) + EPS)
    return of * normg_ref[...] * act(gate)


def _seq_geometry(geom, ctx):
    if ctx:
        return geom["batch"], geom["seq"], 0
    return geom["dec_batch"], geom["dec_seq"], geom["n_ctx"] // geom["dec_seq"]


def _col_spec(t_len, width, col0, blk0):
    base = col0 // width
    return pl.BlockSpec((t_len, width), lambda b, h: (blk0 + b, base + h))


def _launch_mixer(body, name, geom, ctx, width, in_specs, args, extra_out_specs, extra_out_shapes, scratch, **kw):
    nb, t_len, _ = _seq_geometry(geom, ctx)
    heads = geom["mix"] // width
    out_specs = [pl.BlockSpec((t_len, width), lambda b, h: (b, h))] + list(extra_out_specs)
    out_shapes = [jax.ShapeDtypeStruct((nb * t_len, geom["mix"]), BF16)] + list(extra_out_shapes)
    return pl.pallas_call(
        functools.partial(body, **kw),
        grid=(nb, heads),
        in_specs=list(in_specs),
        out_specs=out_specs,
        out_shape=out_shapes,
        scratch_shapes=scratch,
        compiler_params=_cparams(("arbitrary", "arbitrary")),
        name=name,
    )(*args)


def _gla_kernel(*refs, t_len, dk, has_init, want_final):
    it = iter(refs)
    q_ref, k_ref, v_ref, g_ref, zg_ref, wa_ref, ba_ref, normg_ref = (next(it) for _ in range(8))
    s0_ref = next(it) if has_init else None
    y_ref = next(it)
    sf_ref = next(it) if want_final else None
    of_ref, ob_ref, cumf_ref, cumb_ref, stf_ref, stb_ref = (next(it) for _ in range(6))

    cl = min(GLA_CHUNK, t_len)
    nc = t_len // cl
    blk = min(SEQ_CHUNK, t_len)
    scale = dk ** -0.5
    dirs = ((of_ref, cumf_ref, stf_ref), (ob_ref, cumb_ref, stb_ref))

    zg = zg_ref[...]
    ti = _iota2((blk, blk), 0)
    si = _iota2((blk, blk), 1)
    same_chunk = (ti // cl) == (si // cl)
    for d, (_, cum_ref, st_ref) in enumerate(dirs):
        la = _log_sigmoid(jnp.dot(zg, wa_ref[d, 0], precision=HIGHEST, preferred_element_type=F32)
                          + ba_ref[d, 0]) / GLA_TAU
        tri = jnp.where(same_chunk & ((si <= ti) if d == 0 else (si >= ti)), 1.0, 0.0)
        for p in range(t_len // blk):
            rows = slice(p * blk, (p + 1) * blk)
            cum_ref[rows, :] = jnp.dot(tri, la[rows], precision=HIGHEST, preferred_element_type=F32)
        st_ref[...] = s0_ref[0, d, 0].T if has_init else jnp.zeros_like(st_ref)

    tl = _iota2((cl, cl), 0)
    sl = _iota2((cl, cl), 1)
    keeps = (sl <= tl, sl >= tl)

    def chunk_body(c, carry):
        for d, (o_ref, cum_ref, st_ref) in enumerate(dirs):
            cc = c if d == 0 else nc - 1 - c
            r0 = pl.multiple_of(cc * cl, cl)
            cum = cum_ref[pl.ds(r0, cl), :]
            tot = cum[cl - 1:cl, :] if d == 0 else cum[0:1, :]
            q = q_ref[pl.ds(r0, cl), :].astype(F32) * scale
            k = k_ref[pl.ds(r0, cl), :].astype(F32)
            v = v_ref[pl.ds(r0, cl), :]
            qd = (q * jnp.exp(cum)).astype(BF16)
            kd = (k * jnp.exp(-cum)).astype(BF16)
            sc = jnp.where(keeps[d], _nt(qd, kd), 0.0)
            st = st_ref[...]
            o_ref[pl.ds(r0, cl), :] = _nt(qd, st.astype(BF16)) + _dot(sc.astype(BF16), v)
            kl = (k * jnp.exp(tot - cum)).astype(BF16)
            st_ref[...] = st * jnp.exp(tot) + _tn(v, kl)
        return carry

    lax.fori_loop(0, nc, chunk_body, 0)
    if want_final:
        sf_ref[0, 0, 0] = stf_ref[...].T
        sf_ref[0, 1, 0] = stb_ref[...].T
    y = _head_norm_gate(of_ref[...] + ob_ref[...], normg_ref, g_ref[...].astype(F32), _silu)
    y_ref[...] = y.astype(y_ref.dtype)


def _gla_mixer(z, zg, gp, s0, geom, cols, ctx):
    nb, t_len, blk0 = _seq_geometry(geom, ctx)
    mix, heads = geom["mix"], geom["gla_heads"]
    dk, dv = mix // 2 // heads, mix // heads
    has_init, want_final = s0 is not None, ctx
    in_specs = [
        _col_spec(t_len, dk, cols["gla_q"], blk0),
        _col_spec(t_len, dk, cols["gla_k"], blk0),
        _col_spec(t_len, dv, cols["gla_v"], blk0),
        _col_spec(t_len, dv, cols["gla_g"], blk0),
        pl.BlockSpec((t_len, LANES), lambda b, h: (blk0 + b, 0)),
        pl.BlockSpec((2, 1, LANES, dk), lambda b, h: (0, h, 0, 0)),
        pl.BlockSpec((2, 1, 1, dk), lambda b, h: (0, h, 0, 0)),
        pl.BlockSpec((1, dv), lambda b, h: (0, h)),
    ]
    args = [z, z, z, z, zg, gp["wa"], gp["ba"], gp["normg"]]
    if has_init:
        in_specs.append(pl.BlockSpec((1, 2, 1, dk, dv), lambda b, h: (b, 0, h, 0, 0)))
        args.append(s0)
    x_specs, x_shapes = [], []
    if want_final:
        x_specs.append(pl.BlockSpec((1, 2, 1, dk, dv), lambda b, h: (b, 0, h, 0, 0)))
        x_shapes.append(jax.ShapeDtypeStruct((nb, 2, heads, dk, dv), F32))
    res = _launch_mixer(_gla_kernel, "gla_ctx" if ctx else "gla_lat", geom, ctx, dv, in_specs, args,
                        x_specs, x_shapes,
                        [pltpu.VMEM((t_len, dv), F32)] * 2 + [pltpu.VMEM((t_len, dk), F32)] * 2
                        + [pltpu.VMEM((dv, dk), F32)] * 2,
                        t_len=t_len, dk=dk, has_init=has_init, want_final=want_final)
    return (res[0], res[1]) if want_final else (res[0], None)


def _ret_kernel(*refs, t_len, dh, has_init, want_final, rope):
    it = iter(refs)
    lg_ref = next(it)
    q_ref, k_ref, v_ref, g_ref, normg_ref = (next(it) for _ in range(5))
    cos_ref, sin_ref = (next(it), next(it)) if rope else (None, None)
    s0_ref = next(it) if has_init else None
    y_ref = next(it)
    sf_ref = next(it) if want_final else None
    o_ref = next(it)

    h = pl.program_id(1)
    n_heads = lg_ref.shape[0] // 2
    lgf = lg_ref[h]
    lgb = lg_ref[n_heads + h]
    cl = min(SEQ_CHUNK, t_len)
    nc = t_len // cl
    scale = dh ** -0.5
    half = dh // 2

    q = q_ref[...].astype(F32)
    k = k_ref[...].astype(F32) * scale
    if rope:
        cos = cos_ref[...]
        sin = sin_ref[...]

        def rot(x):
            x1, x2 = x[:, :half], x[:, half:]
            return jnp.concatenate([x1 * cos - x2 * sin, x1 * sin + x2 * cos], axis=-1)

        q, k = rot(q), rot(k)
    qb = q.astype(BF16)
    kb = k.astype(BF16)

    dt = (_iota2((cl, cl), 0) - _iota2((cl, cl), 1)).astype(F32)
    decay = (jnp.where(dt >= 0, jnp.exp(lgf * jnp.maximum(dt, 0.0)), 0.0)
             + jnp.where(dt <= 0, jnp.exp(lgb * jnp.maximum(-dt, 0.0)), 0.0))
    tcol = _iota2((cl, 1), 0).astype(F32)

    def rows(c):
        return slice(c * cl, (c + 1) * cl)

    s_f = s0_ref[0, 0, 0] if has_init else None
    for c in range(nc):
        qc, kc, vc = qb[rows(c)], kb[rows(c)], v_ref[rows(c), :]
        o = _dot((_nt(qc, kc) * decay).astype(BF16), vc)
        if s_f is not None:
            o = o + _dot((q[rows(c)] * jnp.exp(lgf * (tcol + 1.0))).astype(BF16), s_f.astype(BF16))
        o_ref[rows(c), :] = o
        if c < nc - 1 or want_final:
            upd = _tn((k[rows(c)] * jnp.exp(lgf * (cl - 1.0 - tcol))).astype(BF16), vc)
            s_f = upd if s_f is None else jnp.exp(lgf * cl) * s_f + upd
    if want_final:
        sf_ref[0, 0, 0] = s_f
    s_b = s0_ref[0, 1, 0] if has_init else None
    for c in range(nc - 1, -1, -1):
        vc = v_ref[rows(c), :]
        if s_b is not None:
            o_ref[rows(c), :] += _dot((q[rows(c)] * jnp.exp(lgb * (cl - tcol))).astype(BF16), s_b.astype(BF16))
        if c > 0 or want_final:
            upd = _tn((k[rows(c)] * jnp.exp(lgb * tcol)).astype(BF16), vc)
            s_b = upd if s_b is None else jnp.exp(lgb * cl) * s_b + upd
    if want_final:
        sf_ref[0, 1, 0] = s_b
    y = _head_norm_gate(o_ref[...], normg_ref, g_ref[...].astype(F32), _silu)
    y_ref[...] = y.astype(y_ref.dtype)


def _ret_mixer(z, lg, normg, rope_tabs, s0, geom, cols, ctx):
    nb, t_len, blk0 = _seq_geometry(geom, ctx)
    mix, heads = geom["mix"], geom["ret_heads"]
    dh = mix // heads
    has_init, want_final, rope = s0 is not None, ctx, rope_tabs is not None
    in_specs = [
        pl.BlockSpec(memory_space=pltpu.SMEM),
        _col_spec(t_len, dh, cols["ret_q"], blk0),
        _col_spec(t_len, dh, cols["ret_k"], blk0),
        _col_spec(t_len, dh, cols["ret_v"], blk0),
        _col_spec(t_len, dh, cols["ret_g"], blk0),
        pl.BlockSpec((1, dh), lambda b, h: (0, h)),
    ]
    args = [lg, z, z, z, z, normg]
    if rope:
        in_specs += [pl.BlockSpec((t_len, dh // 2), lambda b, h: (0, 0))] * 2
        args += list(rope_tabs)
    if has_init:
        in_specs.append(pl.BlockSpec((1, 2, 1, dh, dh), lambda b, h: (b, 0, h, 0, 0)))
        args.append(s0)
    x_specs, x_shapes = [], []
    if want_final:
        x_specs.append(pl.BlockSpec((1, 2, 1, dh, dh), lambda b, h: (b, 0, h, 0, 0)))
        x_shapes.append(jax.ShapeDtypeStruct((nb, 2, heads, dh, dh), F32))
    res = _launch_mixer(_ret_kernel, "ret_ctx" if ctx else "ret_lat", geom, ctx, dh, in_specs, args,
                        x_specs, x_shapes, [pltpu.VMEM((t_len, dh), F32)],
                        t_len=t_len, dh=dh, has_init=has_init, want_final=want_final, rope=rope)
    return (res[0], res[1]) if want_final else (res[0], None)


def _ml_kernel(*refs, t_len, dh, n_heads, layer, depth, gate_lane0, has_init, want_final):
    it = iter(refs)
    bias_ref = next(it)
    m0_ref = next(it) if has_init else None
    q_ref, k_ref, v_ref, og_ref, zg_ref, zgt_ref, normg_ref = (next(it) for _ in range(7))
    c0_ref, n0_ref = (next(it), next(it)) if has_init else (None, None)
    y_ref = next(it)
    cf_ref, nf_ref, mf_ref = (next(it), next(it), next(it)) if want_final else (None, None, None)
    o_ref = next(it)

    b = pl.program_id(0)
    h = pl.program_id(1)
    cl = min(SEQ_CHUNK, t_len)
    nc = t_len // cl
    scale = dh ** -0.5

    qb = q_ref[...]
    qf = qb.astype(F32)
    kf = k_ref[...].astype(F32) * scale
    kb = kf.astype(BF16)

    zg = zg_ref[...]
    zgt = zgt_ref[...]
    lane = _iota2((1, zg.shape[1]), 1)
    sub = _iota2((zgt.shape[0], 1), 0)
    ti = _iota2((cl, cl), 0)
    si = _iota2((cl, cl), 1)

    def rows(c):
        return slice(c * cl, (c + 1) * cl)

    for d in (0, 1):
        gi = d * 2 * n_heads + h
        gf = gi + n_heads
        bi = bias_ref[d * n_heads + h]
        bf = bias_ref[(2 + d) * n_heads + h]
        i_col = jnp.sum(jnp.where(lane == gate_lane0 + gi, zg, 0.0), axis=1, keepdims=True) + bi
        f_col = _log_sigmoid(jnp.sum(jnp.where(lane == gate_lane0 + gf, zg, 0.0), axis=1, keepdims=True) + bf)
        i_row = jnp.sum(jnp.where(sub == gi, zgt, 0.0), axis=0, keepdims=True) + bi
        f_row = _log_sigmoid(jnp.sum(jnp.where(sub == gf, zgt, 0.0), axis=0, keepdims=True) + bf)
        keep = (si <= ti) if d == 0 else (si >= ti)
        keep_t = (ti <= si) if d == 0 else (ti >= si)

        if has_init:
            c_st = c0_ref[0, d, 0]
            n_st = n0_ref[0, d, 0]
            m_st = jnp.full((1, 1), m0_ref[((b * depth + layer) * 2 + d) * n_heads + h], F32)
        else:
            c_st = None
            n_st = None
            m_st = jnp.zeros((1, 1), F32)

        order = range(nc) if d == 0 else range(nc - 1, -1, -1)
        for pos, c in enumerate(order):
            rc = rows(c)
            qc, kc, vc = qb[rc], kb[rc], v_ref[rc, :]
            ic = i_col[rc]
            ir, fr = i_row[:, rc], f_row[:, rc]
            f_cum_col = jnp.sum(jnp.where(keep, fr, 0.0), axis=1, keepdims=True)
            f_cum_row = jnp.sum(jnp.where(keep_t, f_col[rc], 0.0), axis=0, keepdims=True)
            dmat = jnp.where(keep, f_cum_col + (ir - f_cum_row), -jnp.inf)
            g = f_cum_col + m_st
            m_t = jnp.maximum(g, jnp.max(dmat, axis=1, keepdims=True))
            p = _nt(qc, kc) * jnp.exp(dmat - m_t)
            num = _dot(p.astype(BF16), vc)
            den = jnp.sum(p, axis=1, keepdims=True)
            if c_st is not None:
                w_state = jnp.exp(g - m_t)
                num = num + w_state * _dot(qc, c_st.astype(BF16))
                den = den + w_state * jnp.sum(qf[rc] * n_st, axis=1, keepdims=True)
            hh = num / jnp.maximum(jnp.abs(den), jnp.exp(-m_t))
            if d == 0:
                o_ref[rc, :] = hh
            else:
                o_ref[rc, :] += hh
            if pos < nc - 1 or want_final:
                f_last = jnp.sum(fr, axis=1, keepdims=True)
                src = f_last - f_cum_col + ic
                m_new = jnp.maximum(f_last + m_st, jnp.max(src, axis=0, keepdims=True))
                kw = kf[rc] * jnp.exp(src - m_new)
                upd_c = _tn(kw.astype(BF16), vc)
                upd_n = jnp.sum(kw, axis=0, keepdims=True)
                if c_st is not None:
                    w_keep = jnp.exp(f_last + m_st - m_new)
                    c_st = w_keep * c_st + upd_c
                    n_st = w_keep * n_st + upd_n
                else:
                    c_st, n_st = upd_c, upd_n
                m_st = m_new
        if want_final:
            cf_ref[0, d, 0] = c_st
            nf_ref[0, d, 0] = n_st
            mf_ref[0, d, 0] = jnp.broadcast_to(m_st, (1, mf_ref.shape[-1]))
    y = _head_norm_gate(o_ref[...], normg_ref, og_ref[...].astype(F32), jax.nn.sigmoid)
    y_ref[...] = y.astype(y_ref.dtype)


def _ml_mixer(z, zg, zgt, bias, normg, init, layer, depth, gate_lane0, geom, cols, ctx):
    nb, t_len, blk0 = _seq_geometry(geom, ctx)
    mix, heads = geom["mix"], geom["ml_heads"]
    dh = mix // heads
    has_init, want_final = init is not None, ctx
    smem = pl.BlockSpec(memory_space=pltpu.SMEM)
    in_specs, args = [smem], [bias]
    if has_init:
        in_specs.append(smem)
        args.append(init["m"])
    in_specs += [
        _col_spec(t_len, dh, cols["ml_q"], blk0),
        _col_spec(t_len, dh, cols["ml_k"], blk0),
        _col_spec(t_len, dh, cols["ml_v"], blk0),
        _col_spec(t_len, dh, cols["ml_o"], blk0),
        pl.BlockSpec((t_len, LANES), lambda b, h: (blk0 + b, 0)),
        pl.BlockSpec((zgt.shape[0], t_len), lambda b, h: (0, blk0 + b)),
        pl.BlockSpec((1, dh), lambda b, h: (0, h)),
    ]
    args += [z, z, z, z, zg, zgt, normg]
    if has_init:
        in_specs += [pl.BlockSpec((1, 2, 1, dh, dh), lambda b, h: (b, 0, h, 0, 0)),
                     pl.BlockSpec((1, 2, 1, 1, dh), lambda b, h: (b, 0, h, 0, 0))]
        args += [init["c"], init["n"]]
    x_specs, x_shapes = [], []
    if want_final:
        x_specs = [pl.BlockSpec((1, 2, 1, dh, dh), lambda b, h: (b, 0, h, 0, 0)),
                   pl.BlockSpec((1, 2, 1, 1, dh), lambda b, h: (b, 0, h, 0, 0)),
                   pl.BlockSpec((1, 2, 1, 1, LANES), lambda b, h: (b, 0, h, 0, 0))]
        x_shapes = [jax.ShapeDtypeStruct((nb, 2, heads, dh, dh), F32),
                    jax.ShapeDtypeStruct((nb, 2, heads, 1, dh), F32),
                    jax.ShapeDtypeStruct((nb, 2, heads, 1, LANES), F32)]
    res = _launch_mixer(_ml_kernel, "mlstm_ctx" if ctx else "mlstm_lat", geom, ctx, dh, in_specs, args,
                        x_specs, x_shapes, [pltpu.VMEM((t_len, dh), F32)],
                        t_len=t_len, dh=dh, n_heads=heads, layer=layer, depth=depth, gate_lane0=gate_lane0,
                        has_init=has_init, want_final=want_final)
    return (res[0], res[1:]) if want_final else (res[0], None)


def _merge_kernel(*refs, nbr, ctx_tiles):
    h_ref = refs[0]
    y_refs = refs[1:1 + 2 * nbr]
    m_refs = refs[1 + 2 * nbr:1 + 3 * nbr]
    wb_ref, o_ref = refs[1 + 3 * nbr], refs[2 + 3 * nbr]
    is_ctx = pl.program_id(0) < ctx_tiles
    h = h_ref[...]
    acc = None
    for i in range(nbr):
        y = jnp.where(is_ctx, y_refs[2 * i][...], y_refs[2 * i + 1][...])
        term = jax.nn.sigmoid(_dot(h, m_refs[i][...])) * _dot(y, wb_ref[i])
        acc = term if acc is None else acc + term
    o_ref[...] = acc.astype(o_ref.dtype)


def _merge(h, y_pairs, w_merge, w_branch, geom):
    n, d = h.shape
    mix = geom["mix"]
    nbr = len(y_pairs)
    tm = _pick(math.gcd(geom["n_ctx"], n - geom["n_ctx"]), (512, 256, 128))
    tn = _pick(d, (256, 128))
    nj = d // tn
    ctx_tiles = geom["n_ctx"] // tm
    in_specs = [pl.BlockSpec((tm, d), lambda i, j: (i, 0))]
    for _ in range(nbr):
        in_specs.append(pl.BlockSpec((tm, mix), lambda i, j: (jnp.minimum(i, ctx_tiles - 1), 0)))
        in_specs.append(pl.BlockSpec((tm, mix), lambda i, j: (jnp.maximum(i - ctx_tiles, 0), 0)))
    in_specs += [pl.BlockSpec((d, tn), lambda i, j, br=br: (0, br * nj + j)) for br in range(nbr)]
    in_specs += [pl.BlockSpec((nbr, mix, tn), lambda i, j: (0, 0, j))]
    flat = [y for pair in y_pairs for y in pair]
    return pl.pallas_call(
        functools.partial(_merge_kernel, nbr=nbr, ctx_tiles=ctx_tiles),
        grid=(n // tm, nj),
        in_specs=in_specs,
        out_specs=pl.BlockSpec((tm, tn), lambda i, j: (i, j)),
        out_shape=jax.ShapeDtypeStruct((n, d), BF16),
        compiler_params=_cparams(("arbitrary", "arbitrary")),
        name="merge",
    )(h, *flat, *([w_merge] * nbr), w_branch)


def _outproj_kernel(*refs, n_x, ctx_tiles):
    m_ref, w_ref = refs[:2]
    g_ref, o_ref = refs[2 + n_x:]
    x = _read_rows(refs[2:2 + n_x], ctx_tiles)
    o_ref[...] = x + g_ref[0, 0] * _dot(m_ref[...], w_ref[...])


def _outproj(merged, w_out, x, mod4, comp_gate, geom):
    n, d = merged.shape
    tm = _pick(math.gcd(geom["n_ctx"], n - geom["n_ctx"]), (512, 256, 128))
    tn = _pick(d, (512, 256, 128))
    n_ctx, dec_seq, ctx_row = geom["n_ctx"], geom["dec_seq"], geom["ctx_row"]
    ctx_tiles = n_ctx // tm

    def gate_map(i, j):
        start = i * tm
        return (jnp.where(start < n_ctx, ctx_row, (start - n_ctx) // dec_seq), comp_gate, 0, j)

    x_specs, x_args = _row_specs(x, (tm, tn), ctx_tiles, col=lambda i, j: j)
    return pl.pallas_call(
        functools.partial(_outproj_kernel, n_x=len(x_args), ctx_tiles=ctx_tiles),
        grid=(n // tm, d // tn),
        in_specs=[
            pl.BlockSpec((tm, d), lambda i, j: (i, 0)),
            pl.BlockSpec((d, tn), lambda i, j: (0, j)),
        ] + x_specs + [pl.BlockSpec((1, 1, 1, tn), gate_map)],
        out_specs=pl.BlockSpec((tm, tn), lambda i, j: (i, j)),
        out_shape=jax.ShapeDtypeStruct((n, d), F32),
        compiler_params=_cparams(("arbitrary", "arbitrary")),
        name="out_proj",
    )(merged, w_out, *x_args, mod4)


def _first_argmax(vals):
    best, idx = vals[0], jnp.zeros_like(vals[0])
    for j in range(1, len(vals)):
        better = vals[j] > best
        idx = jnp.where(better, float(j), idx)
        best = jnp.where(better, vals[j], best)
    return idx, best


def _pick_row(vals, idx):
    out = vals[0]
    for j in range(1, len(vals)):
        out = jnp.where(idx == float(j), vals[j], out)
    return out


def _router_kernel(x_ref, g_ref, sh_ref, sc_ref, wrt_ref, rb_ref, h_ref, eidx_ref, wgt_ref, rank_ref, cnt_ref,
                   carry_ref, *, n_exp, n_groups):
    i = pl.program_id(0)
    x = x_ref[...]
    y = x * lax.rsqrt(jnp.mean(x * x, axis=-1, keepdims=True) + EPS)
    h = (y * g_ref[...]) * (1.0 + sc_ref[0, 0]) + sh_ref[0, 0]
    h_ref[...] = _pack_bf16_pairs(h)
    tm = x.shape[0]
    per = n_exp // n_groups

    logits = lax.dot_general(wrt_ref[...], h, (((1,), (1,)), ((), ())), precision=HIGHEST,
                             preferred_element_type=F32)
    scores = jax.nn.sigmoid(logits)
    sel = scores + rb_ref[...]
    sel_rows = [sel[e:e + 1, :] for e in range(n_exp)]
    sc_rows = [scores[e:e + 1, :] for e in range(n_exp)]
    group_scores = []
    for g in range(n_groups):
        r = sel_rows[g * per:(g + 1) * per]
        pair = None
        for a in range(per):
            for bb in range(a + 1, per):
                s2 = r[a] + r[bb]
                pair = s2 if pair is None else jnp.maximum(pair, s2)
        group_scores.append(pair)
    g_best, _ = _first_argmax(group_scores)
    in_sel = [_pick_row([sel_rows[g * per + j] for g in range(n_groups)], g_best) for j in range(per)]
    in_sc = [_pick_row([sc_rows[g * per + j] for g in range(n_groups)], g_best) for j in range(per)]
    j1, _ = _first_argmax(in_sel)
    masked = [jnp.where(j1 == float(j), -jnp.inf, in_sel[j]) for j in range(per)]
    j2, _ = _first_argmax(masked)
    w1 = _pick_row(in_sc, j1)
    w2 = _pick_row(in_sc, j2)
    tot = w1 + w2
    e1 = (g_best * per + j1).astype(jnp.int32)
    e2 = (g_best * per + j2).astype(jnp.int32)
    eidx_ref[...] = jnp.concatenate([e1, e2], axis=0)
    wgt_ref[...] = jnp.concatenate([w1 / tot, w2 / tot], axis=0)

    @pl.when(i == 0)
    def _():
        carry_ref[...] = jnp.zeros_like(carry_ref)

    eio = _iota2((n_exp, tm), 0)
    hit1 = eio == e1
    hit2 = eio == e2
    onehot = jnp.where(hit1 | hit2, 1.0, 0.0)
    before = (_iota2((tm, tm), 0) < _iota2((tm, tm), 1)).astype(BF16)
    prefix = _dot(onehot.astype(BF16), before) + carry_ref[:, 0:1]
    r1 = jnp.sum(jnp.where(hit1, prefix, 0.0), axis=0, keepdims=True)
    r2 = jnp.sum(jnp.where(hit2, prefix, 0.0), axis=0, keepdims=True)
    rank_ref[...] = jnp.concatenate([r1, r2], axis=0).astype(jnp.int32)
    carry_ref[...] = carry_ref[...] + jnp.sum(onehot, axis=1, keepdims=True)
    cnt_ref[...] = carry_ref[...]


def _norm_router(x, g, mod4, comp_shift, comp_scale, w_router_t, router_bias, geom):
    n, d = x.shape
    n_exp = w_router_t.shape[0]
    tm = 256
    row2 = pl.BlockSpec((2, tm), lambda i: (0, i))
    return pl.pallas_call(
        functools.partial(_router_kernel, n_exp=n_exp, n_groups=N_EXPERT_GROUPS),
        grid=(n // tm,),
        in_specs=[
            pl.BlockSpec((tm, d), lambda i: (i, 0)),
            pl.BlockSpec((1, d), lambda i: (0, 0)),
            _mod_spec(d, tm, geom, comp_shift),
            _mod_spec(d, tm, geom, comp_scale),
            pl.BlockSpec((n_exp, d), lambda i: (0, 0)),
            pl.BlockSpec((n_exp, 1), lambda i: (0, 0)),
        ],
        out_specs=[pl.BlockSpec((tm, d // 2), lambda i: (i, 0)), row2, row2, row2,
                   pl.BlockSpec((n_exp, LANES), lambda i: (0, 0))],
        out_shape=[jax.ShapeDtypeStruct((n, d // 2), jnp.uint32),
                   jax.ShapeDtypeStruct((2, n), jnp.int32),
                   jax.ShapeDtypeStruct((2, n), F32),
                   jax.ShapeDtypeStruct((2, n), jnp.int32),
                   jax.ShapeDtypeStruct((n_exp, LANES), F32)],
        scratch_shapes=[pltpu.VMEM((n_exp, LANES), F32)],
        compiler_params=_cparams(("arbitrary",)),
        name="norm_router",
    )(x, g.reshape(1, d), mod4, mod4, w_router_t, router_bias.reshape(n_exp, 1))


def _row_copy(src_hbm, row, dst_vmem, r, sem):
    return pltpu.make_async_copy(src_hbm.at[pl.ds(row, 1), :], dst_vmem.at[pl.ds(r, 1), :], sem)


def _dispatch_kernel(src_ref, h_hbm, o_ref, sem):
    tg = o_ref.shape[0]

    def start(i, c):
        for u in range(2):
            r = 2 * i + u
            _row_copy(h_hbm, src_ref[0, 0, r], o_ref, r, sem).start(priority=u)
        return c

    def wait(r, c):
        _row_copy(h_hbm, 0, o_ref, r, sem).wait()
        return c

    lax.fori_loop(0, tg // 2, start, 0, unroll=8)
    lax.fori_loop(0, tg, wait, 0, unroll=16)


def _dispatch(h, src):
    p_rows = src.shape[0]
    dw = h.shape[1]
    tg = _pick(p_rows, (8 * ROUTE_TILE, 4 * ROUTE_TILE, 2 * ROUTE_TILE, ROUTE_TILE))
    return pl.pallas_call(
        _dispatch_kernel,
        grid=(p_rows // tg,),
        in_specs=[
            pl.BlockSpec((1, 1, tg), lambda i: (i, 0, 0), memory_space=pltpu.SMEM),
            pl.BlockSpec(memory_space=pl.ANY),
        ],
        out_specs=pl.BlockSpec((tg, dw), lambda i: (i, 0)),
        out_shape=jax.ShapeDtypeStruct((p_rows, dw), h.dtype),
        scratch_shapes=[pltpu.SemaphoreType.DMA(())],
        compiler_params=_cparams(("arbitrary",)),
        name="moe_dispatch",
    )(src.reshape(p_rows // tg, 1, tg), h)


def _new_expert(te_ref, t):
    return (t == 0) | (te_ref[t] != te_ref[jnp.maximum(t - 1, 0)])


def _ffn_up_kernel(te_ref, tv_ref, x_ref, wg_ref, wu_ref, o_ref, wgb_ref, wub_ref):
    t = pl.program_id(1)

    @pl.when(_new_expert(te_ref, t))
    def _():
        wgb_ref[...] = wg_ref[0].astype(BF16)
        wub_ref[...] = wu_ref[0].astype(BF16)

    @pl.when(tv_ref[t] == 1)
    def _():
        lo, hi = _unpack_bf16_pairs(x_ref[...])
        lo, hi = lo.astype(BF16), hi.astype(BF16)
        half = lo.shape[1]
        g = _dot(lo, wgb_ref[:half, :]) + _dot(hi, wgb_ref[half:, :])
        u = _dot(lo, wub_ref[:half, :]) + _dot(hi, wub_ref[half:, :])
        o_ref[...] = (_silu(g) * u).astype(o_ref.dtype)

    @pl.when(tv_ref[t] == 0)
    def _():
        o_ref[...] = jnp.zeros_like(o_ref)


def _ffn_up(xs, w_gate, w_up, layer, tile_expert, tile_valid):
    p_rows, dw = xs.shape
    d, f = w_gate.shape[2], w_gate.shape[3]
    tm = ROUTE_TILE
    tf = _pick(f, (512, 256, 128))
    w_spec = pl.BlockSpec((None, 1, d, tf), lambda j, t, te, tv: (layer, te[t], 0, j))
    grid_spec = pltpu.PrefetchScalarGridSpec(
        num_scalar_prefetch=2,
        grid=(f // tf, p_rows // tm),
        in_specs=[pl.BlockSpec((tm, dw), lambda j, t, te, tv: (t, 0)), w_spec, w_spec],
        out_specs=pl.BlockSpec((tm, tf), lambda j, t, te, tv: (t, j)),
        scratch_shapes=[pltpu.VMEM((d, tf), BF16), pltpu.VMEM((d, tf), BF16)],
    )
    return pl.pallas_call(
        _ffn_up_kernel,
        grid_spec=grid_spec,
        out_shape=jax.ShapeDtypeStruct((p_rows, f), BF16),
        compiler_params=_cparams(("arbitrary", "arbitrary")),
        name="moe_ffn_up",
    )(tile_expert, tile_valid, xs, w_gate, w_up)


def _ffn_down_kernel(te_ref, tv_ref, h_ref, wd_ref, o_ref, wdb_ref):
    t = pl.program_id(1)

    @pl.when(_new_expert(te_ref, t))
    def _():
        wdb_ref[...] = wd_ref[0].astype(BF16)

    @pl.when(tv_ref[t] == 1)
    def _():
        o_ref[...] = _pack_bf16_pairs(_dot(h_ref[...], wdb_ref[...]))

    @pl.when(tv_ref[t] == 0)
    def _():
        o_ref[...] = jnp.zeros_like(o_ref)


def _ffn_down_tile(d):
    return _pick(d, (2048, 1024, 512, 256))


def _ffn_down(hmid, w_down, layer, tile_expert, tile_valid):
    p_rows, f = hmid.shape
    d = w_down.shape[3]
    tm = ROUTE_TILE
    tn = _ffn_down_tile(d)
    grid_spec = pltpu.PrefetchScalarGridSpec(
        num_scalar_prefetch=2,
        grid=(d // tn, p_rows // tm),
        in_specs=[
            pl.BlockSpec((tm, f), lambda j, t, te, tv: (t, 0)),
            pl.BlockSpec((None, 1, f, tn), lambda j, t, te, tv: (layer, te[t], 0, j)),
        ],
        out_specs=pl.BlockSpec((tm, tn // 2), lambda j, t, te, tv: (t, j)),
        scratch_shapes=[pltpu.VMEM((f, tn), BF16)],
    )
    return pl.pallas_call(
        _ffn_down_kernel,
        grid_spec=grid_spec,
        out_shape=jax.ShapeDtypeStruct((p_rows, d // 2), jnp.uint32),
        compiler_params=_cparams(("arbitrary", "arbitrary")),
        name="moe_ffn_down",
    )(tile_expert, tile_valid, hmid, w_down)


def _combine_kernel(*refs, final_norm, ctx_tiles, tn):
    pos_ref, x_ref, wt_ref, g_ref, fg_ref, ys_hbm = refs[:6]
    out_refs = refs[6:-2]
    buf_ref, sem = refs[-2:]
    tm = x_ref.shape[0]

    def start(r, c):
        for s in range(2):
            _row_copy(ys_hbm, pos_ref[0, s, r], buf_ref.at[s], r, sem).start(priority=s)
        return c

    def wait(r, c):
        for s in range(2):
            _row_copy(ys_hbm, 0, buf_ref.at[s], r, sem).wait()
        return c

    lax.fori_loop(0, tm, start, 0, unroll=4)
    lax.fori_loop(0, tm, wait, 0, unroll=4)
    w = wt_ref[...]
    lane = _iota2((1, w.shape[1]), 1)
    w0 = jnp.sum(jnp.where(lane == 0, w, 0.0), axis=1, keepdims=True)
    w1 = jnp.sum(jnp.where(lane == 1, w, 0.0), axis=1, keepdims=True)
    pieces = []
    for j in range(2 * buf_ref.shape[2] // tn):
        seg = slice(j * tn // 2, (j + 1) * tn // 2)
        lo0, hi0 = _unpack_bf16_pairs(buf_ref[0, :, seg])
        lo1, hi1 = _unpack_bf16_pairs(buf_ref[1, :, seg])
        pieces += [w0 * lo0 + w1 * lo1, w0 * hi0 + w1 * hi1]
    x = x_ref[...] + g_ref[0, 0] * jnp.concatenate(pieces, axis=1)
    if final_norm:
        x = (x * lax.rsqrt(jnp.mean(x * x, axis=-1, keepdims=True) + EPS)) * fg_ref[...]
    if len(out_refs) == 1:
        out_refs[0][...] = x
    else:
        is_ctx = pl.program_id(0) < ctx_tiles

        @pl.when(is_ctx)
        def _():
            out_refs[0][...] = x

        @pl.when(jnp.logical_not(is_ctx))
        def _():
            out_refs[1][...] = x


def _combine(x, ys, pos, wgt_cols, mod4, comp_gate, final_g, geom, final_norm, split_out):
    n, d = x.shape
    tm = pos.shape[2]
    ctx_tiles = geom["n_ctx"] // tm
    if split_out:
        out_specs = [pl.BlockSpec((tm, d), lambda i: (jnp.minimum(i, ctx_tiles - 1), 0)),
                     pl.BlockSpec((tm, d), lambda i: (jnp.maximum(i - ctx_tiles, 0), 0))]
        out_shape = [jax.ShapeDtypeStruct((geom["n_ctx"], d), F32),
                     jax.ShapeDtypeStruct((n - geom["n_ctx"], d), F32)]
    else:
        out_specs = pl.BlockSpec((tm, d), lambda i: (i, 0))
        out_shape = jax.ShapeDtypeStruct((n, d), F32)
    return pl.pallas_call(
        functools.partial(_combine_kernel, final_norm=final_norm, ctx_tiles=ctx_tiles, tn=_ffn_down_tile(d)),
        grid=(n // tm,),
        in_specs=[
            pl.BlockSpec((1, 2, tm), lambda i: (i, 0, 0), memory_space=pltpu.SMEM),
            pl.BlockSpec((tm, d), lambda i: (i, 0)),
            pl.BlockSpec((tm, wgt_cols.shape[1]), lambda i: (i, 0)),
            _mod_spec(d, tm, geom, comp_gate),
            pl.BlockSpec((1, d), lambda i: (0, 0)),
            pl.BlockSpec(memory_space=pl.ANY),
        ],
        out_specs=out_specs,
        out_shape=out_shape,
        scratch_shapes=[pltpu.VMEM((2, tm, d // 2), jnp.uint32), pltpu.SemaphoreType.DMA(())],
        compiler_params=_cparams(("arbitrary",)),
        name="moe_combine",
    )(pos, x, wgt_cols, mod4, final_g.reshape(1, d), ys)


def _route_plan(eidx, rank, counts, n_exp, combine_tile):
    n = eidx.shape[1]
    tile = ROUTE_TILE
    n_tiles = (2 * n) // tile + n_exp
    cnt = counts.astype(jnp.int32)
    padded = ((cnt + tile - 1) // tile) * tile
    e_ids = jnp.arange(n_exp, dtype=jnp.int32)
    ends = jnp.sum(jnp.where(e_ids[None, :] <= e_ids[:, None], padded[None, :], 0), axis=1)
    offs = ends - padded
    pos = jnp.sum(jnp.where(eidx[:, :, None] == e_ids, offs, 0), axis=-1) + rank
    tile_start = jnp.arange(n_tiles, dtype=jnp.int32) * tile
    tile_expert = jnp.minimum(jnp.sum((ends[None, :] <= tile_start[:, None]).astype(jnp.int32), axis=1), n_exp - 1)
    tile_valid = (tile_start < ends[-1]).astype(jnp.int32)
    tok = jnp.broadcast_to(jnp.arange(n, dtype=jnp.int32), (2, n))
    src = jnp.zeros((n_tiles * tile,), jnp.int32).at[pos.reshape(-1)].set(tok.reshape(-1))
    pos_tiles = pos.reshape(2, n // combine_tile, combine_tile).transpose(1, 0, 2)
    return pos_tiles, src, tile_expert, tile_valid


def _s5_params(lam_re, lam_im, log_step, b_re, b_im, c_re, c_im):
    n_dir, groups, p = lam_re.shape
    gch = b_re.shape[-1]
    gps = S5_SLAB // gch
    n_slab = groups // gps
    step = jnp.exp(log_step)[..., None]
    mag = jnp.exp(lam_re * step)
    a_re = mag * jnp.cos(lam_im * step)
    a_im = mag * jnp.sin(lam_im * step)
    den = lam_re * lam_re + lam_im * lam_im
    z_re = ((a_re - 1.0) * lam_re + a_im * lam_im) / den
    z_im = (a_im * lam_re - (a_re - 1.0) * lam_im) / den
    bb_re = z_re[..., None] * b_re - z_im[..., None] * b_im
    bb_im = z_re[..., None] * b_im + z_im[..., None] * b_re
    eye = jnp.eye(gps, dtype=F32)
    bbs = jnp.stack([bb_re, bb_im], axis=2).reshape(n_dir, n_slab, gps, 2, p, gch)
    bm = jnp.einsum("dsgrpc,gh->dsgcrhp", bbs, eye).reshape(n_dir, n_slab, S5_SLAB, 2 * gps * p)
    cs = jnp.stack([c_re, -c_im], axis=0).reshape(2, n_slab, gps, gch, p)
    cm = jnp.einsum("rsgcp,gh->srgphc", cs, eye).reshape(n_slab, 2 * gps * p, S5_SLAB)
    a = jnp.stack([a_re, a_im], axis=2).reshape(n_dir, n_slab, gps, 2, p).transpose(0, 1, 3, 2, 4)
    a = a.reshape(n_dir, n_slab, 1, 2 * gps * p)
    a8 = jnp.broadcast_to(a, (n_dir, n_slab, SUBLANES, 2 * gps * p))
    return bm.astype(BF16), cm.astype(BF16), a8


def _axial_rope(t, dh):
    rows = t // GRID_W
    row = jnp.repeat(jnp.arange(rows, dtype=F32), GRID_W)
    col = (jnp.arange(rows * GRID_W) % GRID_W).astype(F32)
    n_freq = dh // 4
    inv = ROPE_BASE ** (-jnp.arange(n_freq, dtype=F32) / n_freq)
    ang = jnp.concatenate([row[:, None] * inv, col[:, None] * inv], axis=-1)
    return jnp.cos(ang), jnp.sin(ang)


def _to_time_major(u, nb, t_len, bp):
    u = u.reshape(nb, t_len, -1).transpose(1, 0, 2)
    if bp != nb:
        u = jnp.pad(u, ((0, 0), (0, bp - nb), (0, 0)))
    return u.reshape(t_len * bp, -1)


def _from_time_major(y, nb, t_len, bp):
    return y.reshape(t_len, bp, -1)[:, :nb].transpose(1, 0, 2).reshape(nb * t_len, -1)


def _s5_state_to_slabs(s_re, s_im, n_slab, bp):
    nb = s_re.shape[0]
    st = jnp.concatenate([s_re.reshape(nb, 2, n_slab, -1), s_im.reshape(nb, 2, n_slab, -1)], axis=-1)
    st = st.transpose(1, 2, 0, 3)
    return jnp.pad(st, ((0, 0), (0, 0), (0, bp - nb), (0, 0)))


def _s5_slabs_to_state(xf, nb, groups, p):
    half = xf.shape[-1] // 2
    re = xf[:, :, :nb, :half].transpose(2, 0, 1, 3).reshape(nb, 2, groups, p)
    im = xf[:, :, :nb, half:].transpose(2, 0, 1, 3).reshape(nb, 2, groups, p)
    return re, im


def _round_up(x, m):
    return (x + m - 1) // m * m


def kernel(x_prompt, x_sample, c, c_ctx, state_s5_re, state_s5_im, state_gla, state_ml_c, state_ml_n, state_ml_m, state_ret, w_ada, b_ada, norm1_g, norm2_g, w_in, s5_lambda_re, s5_lambda_im, s5_log_step, s5_b_re, s5_b_im, s5_c_re, s5_c_im, s5_d, s5_w_glu, gla_w_a, gla_b_a, gla_norm_g, ml_i_bias, ml_f_bias, ml_norm_g, ret_decay_logit, ret_norm_g, w_branch, w_out, w_router, router_bias, w_exp_gate, w_exp_up, w_exp_down, final_g):
    batch, seq, d = x_prompt.shape
    dec_batch, dec_seq, _ = x_sample.shape
    depth = w_in.shape[0]
    mix = s5_d.shape[-1]
    n_exp = w_router.shape[1]
    gla_heads = state_gla.shape[3]
    ml_heads = ml_i_bias.shape[-1]
    ret_heads = ret_decay_logit.shape[-1]
    gla_rank = gla_w_a.shape[2]
    groups, p_state = s5_lambda_re.shape[2], s5_lambda_re.shape[3]
    n_ctx, n_lat = batch * seq, dec_batch * dec_seq
    n = n_ctx + n_lat
    assert dec_batch + 1 <= SUBLANES and n_ctx % dec_seq == 0 and dec_seq % seq == 0
    geom = dict(batch=batch, seq=seq, dec_batch=dec_batch, dec_seq=dec_seq, n_ctx=n_ctx, n=n, ctx_row=dec_batch,
                mix=mix, gla_heads=gla_heads, ml_heads=ml_heads, ret_heads=ret_heads)
    n_slab = mix // S5_SLAB
    bp_ctx, bp_lat = _round_up(batch, SUBLANES), _round_up(dec_batch, SUBLANES)

    half = mix // 2
    widths = [("s5_u", mix), ("gla_q", half), ("gla_k", half), ("gla_v", mix), ("gla_g", mix), ("gla_r", gla_rank),
              ("ml_q", mix), ("ml_k", mix), ("ml_v", mix), ("ml_o", mix), ("ml_if", 4 * ml_heads),
              ("ret_q", mix), ("ret_k", mix), ("ret_v", mix), ("ret_g", mix), ("merge", 4 * d)]
    src_off, o = {}, 0
    for name, w in widths:
        src_off[name] = (o, w)
        o += w
    main_names = [nm for nm, _ in widths if nm not in ("gla_r", "ml_if", "merge")]
    cols, o = {}, 0
    for nm in main_names:
        cols[nm] = o
        o += src_off[nm][1]

    n_main = o
    moves = [(0, src_off[nm][0], cols[nm], src_off[nm][1]) for nm in main_names]
    moves += [(1, src_off["gla_r"][0], 0, gla_rank), (1, src_off["ml_if"][0], gla_rank, 4 * ml_heads),
              (2, src_off["merge"][0], 0, 4 * d)]
    merged_moves = []
    for mv in moves:
        if merged_moves and merged_moves[-1][0] == mv[0] and merged_moves[-1][1] + merged_moves[-1][3] == mv[1] \
                and merged_moves[-1][2] + merged_moves[-1][3] == mv[2]:
            merged_moves[-1] = merged_moves[-1][:3] + (merged_moves[-1][3] + mv[3],)
        else:
            merged_moves.append(mv)
    moves = tuple(merged_moves)

    x = (x_prompt.reshape(n_ctx, d), x_sample.reshape(n_lat, d))
    c8 = jnp.zeros((SUBLANES, d), F32).at[:dec_batch].set(c).at[dec_batch].set(c_ctx)
    mod = _ada(c8, w_ada, b_ada)
    rope_tabs = _axial_rope(dec_seq, mix // ret_heads)
    w_router_t = w_router.T

    ctx_states = []
    for l in range(depth):
        mod4 = mod[l].reshape(SUBLANES, N_MOD, 1, d)
        w_main, w_gate, w_merge = _repack_proj(w_in, l, moves, n_main, 4 * d)

        h = _norm_mod(x, norm1_g[l], mod4, 0, 1, geom, BF16)
        z = _matmul(h, w_main, BF16, _pick(n, (1024, 512, 256)), _pick(w_main.shape[1], (512, 256, 128)), "in_proj")
        zg = _matmul(h, w_gate, F32, _pick(n, (1024, 512, 256)), LANES, "gate_proj")
        zgt = zg[:, gla_rank:gla_rank + 4 * ml_heads].T

        bm, cm, a8 = _s5_params(s5_lambda_re[l], s5_lambda_im[l], s5_log_step[l], s5_b_re[l], s5_b_im[l],
                                s5_c_re[l], s5_c_im[l])
        dvec = s5_d[l].reshape(1, mix)
        u = z[:, :mix]
        y_c, xf = _s5_mixer(_to_time_major(u[:n_ctx], batch, seq, bp_ctx), bm, cm, a8, dvec, None, seq, bp_ctx, True)
        x0 = _s5_state_to_slabs(state_s5_re[:, l], state_s5_im[:, l], n_slab, bp_lat)
        y_l, _ = _s5_mixer(_to_time_major(u[n_ctx:], dec_batch, dec_seq, bp_lat), bm, cm, a8, dvec, x0, dec_seq,
                           bp_lat, False)
        w_glu = s5_w_glu[l].astype(BF16)
        y_s5 = (_glu(_from_time_major(y_c, batch, seq, bp_ctx), w_glu),
                _glu(_from_time_major(y_l, dec_batch, dec_seq, bp_lat), w_glu))
        s5_re_l, s5_im_l = _s5_slabs_to_state(xf, batch, groups, p_state)

        dk = half // gla_heads
        wa = gla_w_a[l].reshape(2, gla_rank, gla_heads, dk).transpose(0, 2, 1, 3)
        wa = jnp.pad(wa, ((0, 0), (0, 0), (0, LANES - gla_rank), (0, 0)))
        ba = gla_b_a[l].reshape(2, gla_heads, 1, dk)
        gp = dict(wa=wa, ba=ba, normg=gla_norm_g[l].reshape(1, mix))
        y_gla_c, gla_l = _gla_mixer(z, zg, gp, None, geom, cols, True)
        y_gla_l, _ = _gla_mixer(z, zg, gp, state_gla[:, l], geom, cols, False)

        ml_bias = jnp.stack([ml_i_bias[l], ml_f_bias[l]], axis=0).reshape(-1)
        ml_ng = ml_norm_g[l].reshape(1, mix)
        y_ml_c, ml_l = _ml_mixer(z, zg, zgt, ml_bias, ml_ng, None, l, depth, gla_rank, geom, cols, True)
        ml_init = dict(c=state_ml_c[:, l], n=state_ml_n[:, l][:, :, :, None, :], m=state_ml_m.reshape(-1))
        y_ml_l, _ = _ml_mixer(z, zg, zgt, ml_bias, ml_ng, ml_init, l, depth, gla_rank, geom, cols, False)

        lg = jax.nn.log_sigmoid(ret_decay_logit[l]).reshape(-1)
        ret_ng = ret_norm_g[l].reshape(1, mix)
        y_ret_c, ret_l = _ret_mixer(z, lg, ret_ng, None, None, geom, cols, True)
        y_ret_l, _ = _ret_mixer(z, lg, ret_ng, rope_tabs, state_ret[:, l], geom, cols, False)

        y_pairs = [y_s5, (y_gla_c, y_gla_l), (y_ml_c, y_ml_l), (y_ret_c, y_ret_l)]
        merged = _merge(h, y_pairs, w_merge, w_branch[l].astype(BF16), geom)
        x = _outproj(merged, w_out[l].astype(BF16), x, mod4, 2, geom)
        ctx_states.append((s5_re_l, s5_im_l, gla_l, ml_l[0], ml_l[1][:, :, :, 0, :], ml_l[2][:, :, :, 0, 0], ret_l))

        h2, eidx, wgt, rank, counts = _norm_router(x, norm2_g[l], mod4, 3, 4, w_router_t, router_bias, geom)
        last = l == depth - 1
        combine_tile = ROUTE_TILE if last else _pick(math.gcd(n_ctx, n_lat), (2 * ROUTE_TILE, ROUTE_TILE))
        pos_tiles, src, tile_expert, tile_valid = _route_plan(eidx, rank, counts[:, 0], n_exp, combine_tile)
        xs = _dispatch(h2, src)
        hmid = _ffn_up(xs, w_exp_gate, w_exp_up, l, tile_expert, tile_valid)
        ys = _ffn_down(hmid, w_exp_down, l, tile_expert, tile_valid)
        x = _combine(x, ys, pos_tiles, wgt.T, mod4, 5, final_g, geom, final_norm=last, split_out=last)

    y_prompt = x[0].reshape(batch, seq, d)
    y_sample = x[1].reshape(dec_batch, dec_seq, d)
    stacked = [jnp.stack([st[i] for st in ctx_states], axis=1) for i in range(7)]
    return (y_prompt, y_sample, *stacked)
```

```python
import functools
import math

import jax
import jax.numpy as jnp
from jax import lax
from jax.experimental import pallas as pl
from jax.experimental.pallas import tpu as pltpu

F32 = jnp.float32
BF16 = jnp.bfloat16
HIGHEST = lax.Precision.HIGHEST

EPS = 1e-6
GRID_W = 64
ROPE_BASE = 10000.0
GLA_TAU = 16.0
N_EXPERT_GROUPS = 4
N_MOD = 6

LANES = 128
SUBLANES = 8
S5_SLAB = LANES
GLA_CHUNK = 64
SEQ_CHUNK = 256
VMEM_LIMIT_BYTES = 56 * 1024 * 1024
ROUTE_TILE = 256


def _cparams(sem):
    return pltpu.CompilerParams(dimension_semantics=sem, vmem_limit_bytes=VMEM_LIMIT_BYTES)


def _pick(n, cands):
    for c in cands:
        if c <= n and n % c == 0:
            return c
    return n


def _nt(a, b):
    return lax.dot_general(a, b, (((1,), (1,)), ((), ())), preferred_element_type=F32)


def _tn(a, b):
    return lax.dot_general(a, b, (((0,), (0,)), ((), ())), preferred_element_type=F32)


def _dot(a, b):
    return jnp.dot(a, b, preferred_element_type=F32)


def _log_sigmoid(x):
    return jnp.minimum(x, 0.0) - jnp.log1p(jnp.exp(-jnp.abs(x)))


def _silu(x):
    return x * jax.nn.sigmoid(x)


def _iota2(shape, dim):
    return lax.broadcasted_iota(jnp.int32, shape, dim)


HI16 = 0xFFFF0000


def _pack_bf16_pairs(x):
    c = x.shape[1] // 2
    bits = lax.bitcast_convert_type(x.astype(BF16).astype(F32), jnp.uint32)
    return (bits[:, :c] >> 16) | (bits[:, c:] & jnp.uint32(HI16))


def _unpack_bf16_pairs(w):
    lo = lax.bitcast_convert_type(w << 16, F32)
    hi = lax.bitcast_convert_type(w & jnp.uint32(HI16), F32)
    return lo, hi


def _ada_kernel(c_ref, w_ref, b_ref, o_ref):
    s = _silu(c_ref[...]).astype(BF16)
    o_ref[0] = _dot(s, w_ref[0].astype(BF16)) + b_ref[0]


def _ada(c8, w_ada, b_ada):
    depth, d, n = w_ada.shape
    tn = _pick(n, (512, 256, 128))
    return pl.pallas_call(
        _ada_kernel,
        grid=(depth, n // tn),
        in_specs=[
            pl.BlockSpec((SUBLANES, d), lambda l, j: (0, 0)),
            pl.BlockSpec((1, d, tn), lambda l, j: (l, 0, j)),
            pl.BlockSpec((1, 1, tn), lambda l, j: (l, 0, j)),
        ],
        out_specs=pl.BlockSpec((1, SUBLANES, tn), lambda l, j: (l, 0, j)),
        out_shape=jax.ShapeDtypeStruct((depth, SUBLANES, n), F32),
        compiler_params=_cparams(("arbitrary", "arbitrary")),
        name="ada_mod",
    )(c8, w_ada, b_ada.reshape(depth, 1, n))


def _mod_spec(d, tm, geom, comp):
    n_ctx, dec_seq, ctx_row = geom["n_ctx"], geom["dec_seq"], geom["ctx_row"]

    def index_map(i, *_):
        start = i * tm
        row = jnp.where(start < n_ctx, ctx_row, (start - n_ctx) // dec_seq)
        return (row, comp, 0, 0)

    return pl.BlockSpec((1, 1, 1, d), index_map)


def _row_specs(x, block, ctx_tiles, col=lambda *g: 0):
    if not isinstance(x, tuple):
        return [pl.BlockSpec(block, lambda i, *g: (i, col(i, *g)))], [x]
    return ([pl.BlockSpec(block, lambda i, *g: (jnp.minimum(i, ctx_tiles - 1),
                                                jnp.where(i < ctx_tiles, col(i, *g), 0))),
             pl.BlockSpec(block, lambda i, *g: (jnp.maximum(i - ctx_tiles, 0),
                                                jnp.where(i >= ctx_tiles, col(i, *g), 0)))], list(x))


def _read_rows(refs, ctx_tiles):
    if len(refs) == 1:
        return refs[0][...]
    return jnp.where(pl.program_id(0) < ctx_tiles, refs[0][...], refs[1][...])


def _norm_kernel(*refs, n_x, ctx_tiles):
    g_ref, sh_ref, sc_ref, o_ref = refs[n_x:]
    x = _read_rows(refs[:n_x], ctx_tiles)
    y = x * lax.rsqrt(jnp.mean(x * x, axis=-1, keepdims=True) + EPS)
    h = (y * g_ref[...]) * (1.0 + sc_ref[0, 0]) + sh_ref[0, 0]
    o_ref[...] = h.astype(o_ref.dtype)


def _norm_mod(x, g, mod4, comp_shift, comp_scale, geom, out_dtype):
    d = g.shape[0]
    n = geom["n"]
    tm = 256
    ctx_tiles = geom["n_ctx"] // tm
    x_specs, x_args = _row_specs(x, (tm, d), ctx_tiles)
    return pl.pallas_call(
        functools.partial(_norm_kernel, n_x=len(x_args), ctx_tiles=ctx_tiles),
        grid=(n // tm,),
        in_specs=x_specs + [
            pl.BlockSpec((1, d), lambda i: (0, 0)),
            _mod_spec(d, tm, geom, comp_shift),
            _mod_spec(d, tm, geom, comp_scale),
        ],
        out_specs=pl.BlockSpec((tm, d), lambda i: (i, 0)),
        out_shape=jax.ShapeDtypeStruct((n, d), out_dtype),
        compiler_params=_cparams(("arbitrary",)),
        name="norm_mod",
    )(*x_args, g.reshape(1, d), mod4, mod4)


def _mm_kernel(x_ref, w_ref, o_ref):
    o_ref[...] = _dot(x_ref[...], w_ref[...]).astype(o_ref.dtype)


def _matmul(x, w, out_dtype, tm, tn, name):
    m, k = x.shape
    n = w.shape[1]
    return pl.pallas_call(
        _mm_kernel,
        grid=(m // tm, n // tn),
        in_specs=[
            pl.BlockSpec((tm, k), lambda i, j: (i, 0)),
            pl.BlockSpec((k, tn), lambda i, j: (0, j)),
        ],
        out_specs=pl.BlockSpec((tm, tn), lambda i, j: (i, j)),
        out_shape=jax.ShapeDtypeStruct((m, n), out_dtype),
        compiler_params=_cparams(("arbitrary", "arbitrary")),
        name=name,
    )(x, w)


def _repack_kernel(starts_ref, wt_ref, o_ref):
    del starts_ref
    o_ref[...] = wt_ref[...].T.astype(o_ref.dtype)


def _repack_proj(w_in_t, layer, starts, tn):
    d = w_in_t.shape[2]
    grid_spec = pltpu.PrefetchScalarGridSpec(
        num_scalar_prefetch=1,
        grid=(len(starts),),
        in_specs=[pl.BlockSpec((None, pl.Element(tn), pl.Element(d)),
                               lambda j, st: (layer, pl.multiple_of(st[j], 2 * SUBLANES), 0))],
        out_specs=pl.BlockSpec((d, tn), lambda j, st: (0, j)),
    )
    return pl.pallas_call(
        _repack_kernel,
        grid_spec=grid_spec,
        out_shape=jax.ShapeDtypeStruct((d, len(starts) * tn), BF16),
        compiler_params=_cparams(("arbitrary",)),
        name="repack_proj",
    )(jnp.asarray(starts, jnp.int32), w_in_t)


def _s5_kernel(*refs, t_len, bp, tc, has_init, want_final):
    it = iter(refs)
    u_ref, bm_ref, cm_ref, a_ref, d_ref = (next(it) for _ in range(5))
    x0_ref = next(it) if has_init else None
    y_ref = next(it)
    xf_ref = next(it) if want_final else None
    bu_ref, yacc_ref, st_ref = next(it), next(it), next(it)

    nc = t_len // tc
    rc = tc * bp
    half = bu_ref.shape[1] // 2
    for d in (0, 1):
        if has_init:
            st_ref[...] = x0_ref[d, 0]
        else:
            st_ref[...] = jnp.zeros_like(st_ref)
        bm = bm_ref[d, 0]
        ar = a_ref[d, 0, :, :half]
        ai = a_ref[d, 0, :, half:]

        def chunk_body(c, carry, d=d, bm=bm, ar=ar, ai=ai):
            cc = c if d == 0 else nc - 1 - c
            r0 = pl.multiple_of(cc * rc, rc)
            bu_ref[...] = _dot(u_ref[pl.ds(r0, rc), :], bm)
            for rt in range(bp // SUBLANES):
                rows = slice(rt * SUBLANES, (rt + 1) * SUBLANES)

                def step(t, s, rt=rt):
                    sr, si = s
                    tt = t if d == 0 else tc - 1 - t
                    row = pl.multiple_of(tt * bp + rt * SUBLANES, SUBLANES)
                    b = bu_ref[pl.ds(row, SUBLANES), :]
                    xr = ar * sr - ai * si + b[:, :half]
                    xi = ar * si + ai * sr + b[:, half:]
                    bu_ref[pl.ds(row, SUBLANES), :] = jnp.concatenate([xr, xi], axis=-1)
                    return xr, xi

                sr, si = lax.fori_loop(0, tc, step, (st_ref[rows, :half], st_ref[rows, half:]))
                st_ref[rows, :] = jnp.concatenate([sr, si], axis=-1)
            y = _dot(bu_ref[...].astype(BF16), cm_ref[0])
            if d == 0:
                yacc_ref[pl.ds(r0, rc), :] = y
            else:
                yacc_ref[pl.ds(r0, rc), :] += y
            return carry

        lax.fori_loop(0, nc, chunk_body, 0)
        if want_final:
            xf_ref[d, 0] = st_ref[...]
    y = yacc_ref[...] + d_ref[...] * u_ref[...].astype(F32)
    y_ref[...] = jax.nn.gelu(y).astype(y_ref.dtype)


def _s5_mixer(u_tm, bm, cm, a8, dvec, x0, t_len, bp, want_final):
    rows, mix = u_tm.shape
    n_slab = mix // S5_SLAB
    two_half = bm.shape[-1]
    tc = _pick(t_len, tuple(max(1, 1024 // bp) >> s for s in range(6)))
    has_init = x0 is not None
    in_specs = [
        pl.BlockSpec((rows, S5_SLAB), lambda s: (0, s)),
        pl.BlockSpec((2, 1, S5_SLAB, two_half), lambda s: (0, s, 0, 0)),
        pl.BlockSpec((1, two_half, S5_SLAB), lambda s: (s, 0, 0)),
        pl.BlockSpec((2, 1, SUBLANES, two_half), lambda s: (0, s, 0, 0)),
        pl.BlockSpec((1, S5_SLAB), lambda s: (0, s)),
    ]
    args = [u_tm, bm, cm, a8, dvec]
    if has_init:
        in_specs.append(pl.BlockSpec((2, 1, bp, two_half), lambda s: (0, s, 0, 0)))
        args.append(x0)
    out_specs = [pl.BlockSpec((rows, S5_SLAB), lambda s: (0, s))]
    out_shape = [jax.ShapeDtypeStruct((rows, mix), BF16)]
    if want_final:
        out_specs.append(pl.BlockSpec((2, 1, bp, two_half), lambda s: (0, s, 0, 0)))
        out_shape.append(jax.ShapeDtypeStruct((2, n_slab, bp, two_half), F32))
    res = pl.pallas_call(
        functools.partial(_s5_kernel, t_len=t_len, bp=bp, tc=tc, has_init=has_init, want_final=want_final),
        grid=(n_slab,),
        in_specs=in_specs,
        out_specs=out_specs,
        out_shape=out_shape,
        scratch_shapes=[
            pltpu.VMEM((tc * bp, two_half), F32),
            pltpu.VMEM((rows, S5_SLAB), F32),
            pltpu.VMEM((bp, two_half), F32),
        ],
        compiler_params=_cparams(("arbitrary",)),
        name="s5_ctx" if want_final else "s5_lat",
    )(*args)
    return (res[0], res[1]) if want_final else (res[0], None)


def _glu_kernel(y_ref, w_ref, o_ref):
    y = y_ref[...]
    z = _dot(y, w_ref[...])
    o_ref[...] = (y.astype(F32) * jax.nn.sigmoid(z)).astype(o_ref.dtype)


def _glu(y, w):
    n, mix = y.shape
    tm = _pick(n, (512, 256, 128))
    return pl.pallas_call(
        _glu_kernel,
        grid=(n // tm,),
        in_specs=[pl.BlockSpec((tm, mix), lambda i: (i, 0)), pl.BlockSpec((mix, mix), lambda i: (0, 0))],
        out_specs=pl.BlockSpec((tm, mix), lambda i: (i, 0)),
        out_shape=jax.ShapeDtypeStruct((n, mix), BF16),
        compiler_params=_cparams(("arbitrary",)),
        name="s5_glu",
    )(y, w)


def _head_norm_gate(o, normg_ref, gate, act):
    of = o * lax.rsqrt(jnp.mean(o * o, axis=-1, keepdims=True) + EPS)
    return of * normg_ref[...] * act(gate)


def _seq_geometry(geom, ctx):
    if ctx:
        return geom["batch"], geom["seq"], 0
    return geom["dec_batch"], geom["dec_seq"], geom["n_ctx"] // geom["dec_seq"]


def _col_spec(t_len, width, col0, blk0):
    base = col0 // width
    return pl.BlockSpec((t_len, width), lambda b, h: (blk0 + b, base + h))


def _launch_mixer(body, name, geom, ctx, width, in_specs, args, extra_out_specs, extra_out_shapes, scratch, **kw):
    nb, t_len, _ = _seq_geometry(geom, ctx)
    heads = geom["mix"] // width
    out_specs = [pl.BlockSpec((t_len, width), lambda b, h: (b, h))] + list(extra_out_specs)
    out_shapes = [jax.ShapeDtypeStruct((nb * t_len, geom["mix"]), BF16)] + list(extra_out_shapes)
    return pl.pallas_call(
        functools.partial(body, **kw),
        grid=(nb, heads),
        in_specs=list(in_specs),
        out_specs=out_specs,
        out_shape=out_shapes,
        scratch_shapes=scratch,
        compiler_params=_cparams(("arbitrary", "arbitrary")),
        name=name,
    )(*args)


def _gla_kernel(*refs, t_len, dk, has_init, want_final):
    it = iter(refs)
    q_ref, k_ref, v_ref, g_ref, zg_ref, wa_ref, ba_ref, normg_ref = (next(it) for _ in range(8))
    s0_ref = next(it) if has_init else None
    y_ref = next(it)
    sf_ref = next(it) if want_final else None
    of_ref, ob_ref, cumf_ref, cumb_ref, stf_ref, stb_ref = (next(it) for _ in range(6))

    cl = min(GLA_CHUNK, t_len)
    nc = t_len // cl
    blk = min(SEQ_CHUNK, t_len)
    scale = dk ** -0.5
    dirs = ((of_ref, cumf_ref, stf_ref), (ob_ref, cumb_ref, stb_ref))

    zg = zg_ref[...]
    ti = _iota2((blk, blk), 0)
    si = _iota2((blk, blk), 1)
    same_chunk = (ti // cl) == (si // cl)
    for d, (_, cum_ref, st_ref) in enumerate(dirs):
        la = _log_sigmoid(jnp.dot(zg, wa_ref[d, 0], precision=HIGHEST, preferred_element_type=F32)
                          + ba_ref[d, 0]) / GLA_TAU
        tri = jnp.where(same_chunk & ((si <= ti) if d == 0 else (si >= ti)), 1.0, 0.0)
        for p in range(t_len // blk):
            rows = slice(p * blk, (p + 1) * blk)
            cum_ref[rows, :] = jnp.dot(tri, la[rows], precision=HIGHEST, preferred_element_type=F32)
        st_ref[...] = s0_ref[0, d, 0].T if has_init else jnp.zeros_like(st_ref)

    tl = _iota2((cl, cl), 0)
    sl = _iota2((cl, cl), 1)
    keeps = (sl <= tl, sl >= tl)

    def chunk_body(c, carry):
        for d, (o_ref, cum_ref, st_ref) in enumerate(dirs):
            cc = c if d == 0 else nc - 1 - c
            r0 = pl.multiple_of(cc * cl, cl)
            cum = cum_ref[pl.ds(r0, cl), :]
            tot = cum[cl - 1:cl, :] if d == 0 else cum[0:1, :]
            q = q_ref[pl.ds(r0, cl), :].astype(F32) * scale
            k = k_ref[pl.ds(r0, cl), :].astype(F32)
            v = v_ref[pl.ds(r0, cl), :]
            qd = (q * jnp.exp(cum)).astype(BF16)
            kd = (k * jnp.exp(-cum)).astype(BF16)
            sc = jnp.where(keeps[d], _nt(qd, kd), 0.0)
            st = st_ref[...]
            o_ref[pl.ds(r0, cl), :] = _nt(qd, st.astype(BF16)) + _dot(sc.astype(BF16), v)
            kl = (k * jnp.exp(tot - cum)).astype(BF16)
            st_ref[...] = st * jnp.exp(tot) + _tn(v, kl)
        return carry

    lax.fori_loop(0, nc, chunk_body, 0)
    if want_final:
        sf_ref[0, 0, 0] = stf_ref[...].T
        sf_ref[0, 1, 0] = stb_ref[...].T
    y = _head_norm_gate(of_ref[...] + ob_ref[...], normg_ref, g_ref[...].astype(F32), _silu)
    y_ref[...] = y.astype(y_ref.dtype)


def _gla_mixer(z, zg, gp, s0, geom, cols, ctx):
    nb, t_len, blk0 = _seq_geometry(geom, ctx)
    mix, heads = geom["mix"], geom["gla_heads"]
    dk, dv = mix // 2 // heads, mix // heads
    has_init, want_final = s0 is not None, ctx
    in_specs = [
        _col_spec(t_len, dk, cols["gla_q"], blk0),
        _col_spec(t_len, dk, cols["gla_k"], blk0),
        _col_spec(t_len, dv, cols["gla_v"], blk0),
        _col_spec(t_len, dv, cols["gla_g"], blk0),
        pl.BlockSpec((t_len, LANES), lambda b, h: (blk0 + b, 0)),
        pl.BlockSpec((2, 1, LANES, dk), lambda b, h: (0, h, 0, 0)),
        pl.BlockSpec((2, 1, 1, dk), lambda b, h: (0, h, 0, 0)),
        pl.BlockSpec((1, dv), lambda b, h: (0, h)),
    ]
    args = [z, z, z, z, zg, gp["wa"], gp["ba"], gp["normg"]]
    if has_init:
        in_specs.append(pl.BlockSpec((1, 2, 1, dk, dv), lambda b, h: (b, 0, h, 0, 0)))
        args.append(s0)
    x_specs, x_shapes = [], []
    if want_final:
        x_specs.append(pl.BlockSpec((1, 2, 1, dk, dv), lambda b, h: (b, 0, h, 0, 0)))
        x_shapes.append(jax.ShapeDtypeStruct((nb, 2, heads, dk, dv), F32))
    res = _launch_mixer(_gla_kernel, "gla_ctx" if ctx else "gla_lat", geom, ctx, dv, in_specs, args,
                        x_specs, x_shapes,
                        [pltpu.VMEM((t_len, dv), F32)] * 2 + [pltpu.VMEM((t_len, dk), F32)] * 2
                        + [pltpu.VMEM((dv, dk), F32)] * 2,
                        t_len=t_len, dk=dk, has_init=has_init, want_final=want_final)
    return (res[0], res[1]) if want_final else (res[0], None)


def _ret_kernel(*refs, t_len, dh, has_init, want_final, rope):
    it = iter(refs)
    lg_ref = next(it)
    q_ref, k_ref, v_ref, g_ref, normg_ref = (next(it) for _ in range(5))
    cos_ref, sin_ref = (next(it), next(it)) if rope else (None, None)
    s0_ref = next(it) if has_init else None
    y_ref = next(it)
    sf_ref = next(it) if want_final else None
    o_ref = next(it)

    h = pl.program_id(1)
    n_heads = lg_ref.shape[0] // 2
    lgf = lg_ref[h]
    lgb = lg_ref[n_heads + h]
    cl = min(SEQ_CHUNK, t_len)
    nc = t_len // cl
    scale = dh ** -0.5
    half = dh // 2

    q = q_ref[...].astype(F32)
    k = k_ref[...].astype(F32) * scale
    if rope:
        cos = cos_ref[...]
        sin = sin_ref[...]

        def rot(x):
            x1, x2 = x[:, :half], x[:, half:]
            return jnp.concatenate([x1 * cos - x2 * sin, x1 * sin + x2 * cos], axis=-1)

        q, k = rot(q), rot(k)
    qb = q.astype(BF16)
    kb = k.astype(BF16)

    dt = (_iota2((cl, cl), 0) - _iota2((cl, cl), 1)).astype(F32)
    decay = (jnp.where(dt >= 0, jnp.exp(lgf * jnp.maximum(dt, 0.0)), 0.0)
             + jnp.where(dt <= 0, jnp.exp(lgb * jnp.maximum(-dt, 0.0)), 0.0))
    tcol = _iota2((cl, 1), 0).astype(F32)

    def rows(c):
        return slice(c * cl, (c + 1) * cl)

    s_f = s0_ref[0, 0, 0] if has_init else None
    for c in range(nc):
        qc, kc, vc = qb[rows(c)], kb[rows(c)], v_ref[rows(c), :]
        o = _dot((_nt(qc, kc) * decay).astype(BF16), vc)
        if s_f is not None:
            o = o + _dot((q[rows(c)] * jnp.exp(lgf * (tcol + 1.0))).astype(BF16), s_f.astype(BF16))
        o_ref[rows(c), :] = o
        if c < nc - 1 or want_final:
            upd = _tn((k[rows(c)] * jnp.exp(lgf * (cl - 1.0 - tcol))).astype(BF16), vc)
            s_f = upd if s_f is None else jnp.exp(lgf * cl) * s_f + upd
    if want_final:
        sf_ref[0, 0, 0] = s_f
    s_b = s0_ref[0, 1, 0] if has_init else None
    for c in range(nc - 1, -1, -1):
        vc = v_ref[rows(c), :]
        if s_b is not None:
            o_ref[rows(c), :] += _dot((q[rows(c)] * jnp.exp(lgb * (cl - tcol))).astype(BF16), s_b.astype(BF16))
        if c > 0 or want_final:
            upd = _tn((k[rows(c)] * jnp.exp(lgb * tcol)).astype(BF16), vc)
            s_b = upd if s_b is None else jnp.exp(lgb * cl) * s_b + upd
    if want_final:
        sf_ref[0, 1, 0] = s_b
    y = _head_norm_gate(o_ref[...], normg_ref, g_ref[...].astype(F32), _silu)
    y_ref[...] = y.astype(y_ref.dtype)


def _ret_mixer(z, lg, normg, rope_tabs, s0, geom, cols, ctx):
    nb, t_len, blk0 = _seq_geometry(geom, ctx)
    mix, heads = geom["mix"], geom["ret_heads"]
    dh = mix // heads
    has_init, want_final, rope = s0 is not None, ctx, rope_tabs is not None
    in_specs = [
        pl.BlockSpec(memory_space=pltpu.SMEM),
        _col_spec(t_len, dh, cols["ret_q"], blk0),
        _col_spec(t_len, dh, cols["ret_k"], blk0),
        _col_spec(t_len, dh, cols["ret_v"], blk0),
        _col_spec(t_len, dh, cols["ret_g"], blk0),
        pl.BlockSpec((1, dh), lambda b, h: (0, h)),
    ]
    args = [lg, z, z, z, z, normg]
    if rope:
        in_specs += [pl.BlockSpec((t_len, dh // 2), lambda b, h: (0, 0))] * 2
        args += list(rope_tabs)
    if has_init:
        in_specs.append(pl.BlockSpec((1, 2, 1, dh, dh), lambda b, h: (b, 0, h, 0, 0)))
        args.append(s0)
    x_specs, x_shapes = [], []
    if want_final:
        x_specs.append(pl.BlockSpec((1, 2, 1, dh, dh), lambda b, h: (b, 0, h, 0, 0)))
        x_shapes.append(jax.ShapeDtypeStruct((nb, 2, heads, dh, dh), F32))
    res = _launch_mixer(_ret_kernel, "ret_ctx" if ctx else "ret_lat", geom, ctx, dh, in_specs, args,
                        x_specs, x_shapes, [pltpu.VMEM((t_len, dh), F32)],
                        t_len=t_len, dh=dh, has_init=has_init, want_final=want_final, rope=rope)
    return (res[0], res[1]) if want_final else (res[0], None)


def _ml_kernel(*refs, t_len, dh, n_heads, layer, depth, gate_lane0, has_init, want_final):
    it = iter(refs)
    bias_ref = next(it)
    m0_ref = next(it) if has_init else None
    q_ref, k_ref, v_ref, og_ref, zg_ref, zgt_ref, normg_ref = (next(it) for _ in range(7))
    c0_ref, n0_ref = (next(it), next(it)) if has_init else (None, None)
    y_ref = next(it)
    cf_ref, nf_ref, mf_ref = (next(it), next(it), next(it)) if want_final else (None, None, None)
    o_ref = next(it)

    b = pl.program_id(0)
    h = pl.program_id(1)
    cl = min(SEQ_CHUNK, t_len)
    nc = t_len // cl
    scale = dh ** -0.5

    qb = q_ref[...]
    qf = qb.astype(F32)
    kf = k_ref[...].astype(F32) * scale
    kb = kf.astype(BF16)

    zg = zg_ref[...]
    zgt = zgt_ref[...]
    lane = _iota2((1, zg.shape[1]), 1)
    sub = _iota2((zgt.shape[0], 1), 0)
    ti = _iota2((cl, cl), 0)
    si = _iota2((cl, cl), 1)

    def rows(c):
        return slice(c * cl, (c + 1) * cl)

    for d in (0, 1):
        gi = d * 2 * n_heads + h
        gf = gi + n_heads
        bi = bias_ref[d * n_heads + h]
        bf = bias_ref[(2 + d) * n_heads + h]
        i_col = jnp.sum(jnp.where(lane == gate_lane0 + gi, zg, 0.0), axis=1, keepdims=True) + bi
        f_col = _log_sigmoid(jnp.sum(jnp.where(lane == gate_lane0 + gf, zg, 0.0), axis=1, keepdims=True) + bf)
        i_row = jnp.sum(jnp.where(sub == gi, zgt, 0.0), axis=0, keepdims=True) + bi
        f_row = _log_sigmoid(jnp.sum(jnp.where(sub == gf, zgt, 0.0), axis=0, keepdims=True) + bf)
        keep = (si <= ti) if d == 0 else (si >= ti)
        keep_t = (ti <= si) if d == 0 else (ti >= si)

        if has_init:
            c_st = c0_ref[0, d, 0]
            n_st = n0_ref[0, d, 0]
            m_st = jnp.full((1, 1), m0_ref[((b * depth + layer) * 2 + d) * n_heads + h], F32)
        else:
            c_st = None
            n_st = None
            m_st = jnp.zeros((1, 1), F32)

        order = range(nc) if d == 0 else range(nc - 1, -1, -1)
        for pos, c in enumerate(order):
            rc = rows(c)
            qc, kc, vc = qb[rc], kb[rc], v_ref[rc, :]
            ic = i_col[rc]
            ir, fr = i_row[:, rc], f_row[:, rc]
            f_cum_col = jnp.sum(jnp.where(keep, fr, 0.0), axis=1, keepdims=True)
            f_cum_row = jnp.sum(jnp.where(keep_t, f_col[rc], 0.0), axis=0, keepdims=True)
            dmat = jnp.where(keep, f_cum_col + (ir - f_cum_row), -jnp.inf)
            g = f_cum_col + m_st
            m_t = jnp.maximum(g, jnp.max(dmat, axis=1, keepdims=True))
            p = _nt(qc, kc) * jnp.exp(dmat - m_t)
            num = _dot(p.astype(BF16), vc)
            den = jnp.sum(p, axis=1, keepdims=True)
            if c_st is not None:
                w_state = jnp.exp(g - m_t)
                num = num + w_state * _dot(qc, c_st.astype(BF16))
                den = den + w_state * jnp.sum(qf[rc] * n_st, axis=1, keepdims=True)
            hh = num / jnp.maximum(jnp.abs(den), jnp.exp(-m_t))
            if d == 0:
                o_ref[rc, :] = hh
            else:
                o_ref[rc, :] += hh
            if pos < nc - 1 or want_final:
                f_last = jnp.sum(fr, axis=1, keepdims=True)
                src = f_last - f_cum_col + ic
                m_new = jnp.maximum(f_last + m_st, jnp.max(src, axis=0, keepdims=True))
                kw = kf[rc] * jnp.exp(src - m_new)
                upd_c = _tn(kw.astype(BF16), vc)
                upd_n = jnp.sum(kw, axis=0, keepdims=True)
                if c_st is not None:
                    w_keep = jnp.exp(f_last + m_st - m_new)
                    c_st = w_keep * c_st + upd_c
                    n_st = w_keep * n_st + upd_n
                else:
                    c_st, n_st = upd_c, upd_n
                m_st = m_new
        if want_final:
            cf_ref[0, d, 0] = c_st
            nf_ref[0, d, 0] = n_st
            mf_ref[0, d, 0] = jnp.broadcast_to(m_st, (1, mf_ref.shape[-1]))
    y = _head_norm_gate(o_ref[...], normg_ref, og_ref[...].astype(F32), jax.nn.sigmoid)
    y_ref[...] = y.astype(y_ref.dtype)


def _ml_mixer(z, zg, zgt, bias, normg, init, layer, depth, gate_lane0, geom, cols, ctx):
    nb, t_len, blk0 = _seq_geometry(geom, ctx)
    mix, heads = geom["mix"], geom["ml_heads"]
    dh = mix // heads
    has_init, want_final = init is not None, ctx
    smem = pl.BlockSpec(memory_space=pltpu.SMEM)
    in_specs, args = [smem], [bias]
    if has_init:
        in_specs.append(smem)
        args.append(init["m"])
    in_specs += [
        _col_spec(t_len, dh, cols["ml_q"], blk0),
        _col_spec(t_len, dh, cols["ml_k"], blk0),
        _col_spec(t_len, dh, cols["ml_v"], blk0),
        _col_spec(t_len, dh, cols["ml_o"], blk0),
        pl.BlockSpec((t_len, LANES), lambda b, h: (blk0 + b, 0)),
        pl.BlockSpec((zgt.shape[0], t_len), lambda b, h: (0, blk0 + b)),
        pl.BlockSpec((1, dh), lambda b, h: (0, h)),
    ]
    args += [z, z, z, z, zg, zgt, normg]
    if has_init:
        in_specs += [pl.BlockSpec((1, 2, 1, dh, dh), lambda b, h: (b, 0, h, 0, 0)),
                     pl.BlockSpec((1, 2, 1, 1, dh), lambda b, h: (b, 0, h, 0, 0))]
        args += [init["c"], init["n"]]
    x_specs, x_shapes = [], []
    if want_final:
        x_specs = [pl.BlockSpec((1, 2, 1, dh, dh), lambda b, h: (b, 0, h, 0, 0)),
                   pl.BlockSpec((1, 2, 1, 1, dh), lambda b, h: (b, 0, h, 0, 0)),
                   pl.BlockSpec((1, 2, 1, 1, LANES), lambda b, h: (b, 0, h, 0, 0))]
        x_shapes = [jax.ShapeDtypeStruct((nb, 2, heads, dh, dh), F32),
                    jax.ShapeDtypeStruct((nb, 2, heads, 1, dh), F32),
                    jax.ShapeDtypeStruct((nb, 2, heads, 1, LANES), F32)]
    res = _launch_mixer(_ml_kernel, "mlstm_ctx" if ctx else "mlstm_lat", geom, ctx, dh, in_specs, args,
                        x_specs, x_shapes, [pltpu.VMEM((t_len, dh), F32)],
                        t_len=t_len, dh=dh, n_heads=heads, layer=layer, depth=depth, gate_lane0=gate_lane0,
                        has_init=has_init, want_final=want_final)
    return (res[0], res[1:]) if want_final else (res[0], None)


def _merge_kernel(*refs, nbr, ctx_tiles):
    h_ref = refs[0]
    y_refs = refs[1:1 + 2 * nbr]
    m_refs = refs[1 + 2 * nbr:1 + 3 * nbr]
    wb_ref, o_ref = refs[1 + 3 * nbr], refs[2 + 3 * nbr]
    is_ctx = pl.program_id(0) < ctx_tiles
    h = h_ref[...]
    acc = None
    for i in range(nbr):
        y = jnp.where(is_ctx, y_refs[2 * i][...], y_refs[2 * i + 1][...])
        term = jax.nn.sigmoid(_dot(h, m_refs[i][...])) * _dot(y, wb_ref[i])
        acc = term if acc is None else acc + term
    o_ref[...] = acc.astype(o_ref.dtype)


def _merge(h, y_pairs, w_merge, w_branch, geom):
    n, d = h.shape
    mix = geom["mix"]
    nbr = len(y_pairs)
    tm = _pick(math.gcd(geom["n_ctx"], n - geom["n_ctx"]), (512, 256, 128))
    tn = _pick(d, (256, 128))
    nj = d // tn
    ctx_tiles = geom["n_ctx"] // tm
    in_specs = [pl.BlockSpec((tm, d), lambda i, j: (i, 0))]
    for _ in range(nbr):
        in_specs.append(pl.BlockSpec((tm, mix), lambda i, j: (jnp.minimum(i, ctx_tiles - 1), 0)))
        in_specs.append(pl.BlockSpec((tm, mix), lambda i, j: (jnp.maximum(i - ctx_tiles, 0), 0)))
    in_specs += [pl.BlockSpec((d, tn), lambda i, j, br=br: (0, br * nj + j)) for br in range(nbr)]
    in_specs += [pl.BlockSpec((nbr, mix, tn), lambda i, j: (0, 0, j))]
    flat = [y for pair in y_pairs for y in pair]
    return pl.pallas_call(
        functools.partial(_merge_kernel, nbr=nbr, ctx_tiles=ctx_tiles),
        grid=(n // tm, nj),
        in_specs=in_specs,
        out_specs=pl.BlockSpec((tm, tn), lambda i, j: (i, j)),
        out_shape=jax.ShapeDtypeStruct((n, d), BF16),
        compiler_params=_cparams(("arbitrary", "arbitrary")),
        name="merge",
    )(h, *flat, *([w_merge] * nbr), w_branch)


def _outproj_kernel(*refs, n_x, ctx_tiles):
    m_ref, w_ref = refs[:2]
    g_ref, o_ref = refs[2 + n_x:]
    x = _read_rows(refs[2:2 + n_x], ctx_tiles)
    o_ref[...] = x + g_ref[0, 0] * _dot(m_ref[...], w_ref[...])


def _outproj(merged, w_out, x, mod4, comp_gate, geom):
    n, d = merged.shape
    tm = _pick(math.gcd(geom["n_ctx"], geom["dec_seq"]), (1024, 512, 256, 128))
    tn = _pick(d, (512, 256, 128))
    n_ctx, dec_seq, ctx_row = geom["n_ctx"], geom["dec_seq"], geom["ctx_row"]
    ctx_tiles = n_ctx // tm

    def gate_map(i, j):
        start = i * tm
        return (jnp.where(start < n_ctx, ctx_row, (start - n_ctx) // dec_seq), comp_gate, 0, j)

    x_specs, x_args = _row_specs(x, (tm, tn), ctx_tiles, col=lambda i, j: j)
    return pl.pallas_call(
        functools.partial(_outproj_kernel, n_x=len(x_args), ctx_tiles=ctx_tiles),
        grid=(n // tm, d // tn),
        in_specs=[
            pl.BlockSpec((tm, d), lambda i, j: (i, 0)),
            pl.BlockSpec((d, tn), lambda i, j: (0, j)),
        ] + x_specs + [pl.BlockSpec((1, 1, 1, tn), gate_map)],
        out_specs=pl.BlockSpec((tm, tn), lambda i, j: (i, j)),
        out_shape=jax.ShapeDtypeStruct((n, d), F32),
        compiler_params=_cparams(("arbitrary", "arbitrary")),
        name="out_proj",
    )(merged, w_out, *x_args, mod4)


def _first_argmax(vals):
    best, idx = vals[0], jnp.zeros_like(vals[0])
    for j in range(1, len(vals)):
        better = vals[j] > best
        idx = jnp.where(better, float(j), idx)
        best = jnp.where(better, vals[j], best)
    return idx, best


def _pick_row(vals, idx):
    out = vals[0]
    for j in range(1, len(vals)):
        out = jnp.where(idx == float(j), vals[j], out)
    return out


def _router_kernel(x_ref, g_ref, sh_ref, sc_ref, wrt_ref, rb_ref, h_ref, eidx_ref, wgt_ref, rank_ref, cnt_ref,
                   carry_ref, *, n_exp, n_groups):
    i = pl.program_id(0)
    x = x_ref[...]
    y = x * lax.rsqrt(jnp.mean(x * x, axis=-1, keepdims=True) + EPS)
    h = (y * g_ref[...]) * (1.0 + sc_ref[0, 0]) + sh_ref[0, 0]
    h_ref[...] = _pack_bf16_pairs(h)
    tm = x.shape[0]
    per = n_exp // n_groups

    logits = lax.dot_general(wrt_ref[...], h, (((1,), (1,)), ((), ())), precision=HIGHEST,
                             preferred_element_type=F32)
    scores = jax.nn.sigmoid(logits)
    sel = scores + rb_ref[...]
    sel_rows = [sel[e:e + 1, :] for e in range(n_exp)]
    sc_rows = [scores[e:e + 1, :] for e in range(n_exp)]
    group_scores = []
    for g in range(n_groups):
        r = sel_rows[g * per:(g + 1) * per]
        pair = None
        for a in range(per):
            for bb in range(a + 1, per):
                s2 = r[a] + r[bb]
                pair = s2 if pair is None else jnp.maximum(pair, s2)
        group_scores.append(pair)
    g_best, _ = _first_argmax(group_scores)
    in_sel = [_pick_row([sel_rows[g * per + j] for g in range(n_groups)], g_best) for j in range(per)]
    in_sc = [_pick_row([sc_rows[g * per + j] for g in range(n_groups)], g_best) for j in range(per)]
    j1, _ = _first_argmax(in_sel)
    masked = [jnp.where(j1 == float(j), -jnp.inf, in_sel[j]) for j in range(per)]
    j2, _ = _first_argmax(masked)
    w1 = _pick_row(in_sc, j1)
    w2 = _pick_row(in_sc, j2)
    tot = w1 + w2
    e1 = (g_best * per + j1).astype(jnp.int32)
    e2 = (g_best * per + j2).astype(jnp.int32)
    eidx_ref[...] = jnp.concatenate([e1, e2], axis=0)
    wgt_ref[...] = jnp.concatenate([w1 / tot, w2 / tot], axis=0)

    @pl.when(i == 0)
    def _():
        carry_ref[...] = jnp.zeros_like(carry_ref)

    eio = _iota2((n_exp, tm), 0)
    hit1 = eio == e1
    hit2 = eio == e2
    onehot = jnp.where(hit1 | hit2, 1.0, 0.0)
    before = (_iota2((tm, tm), 0) < _iota2((tm, tm), 1)).astype(BF16)
    prefix = _dot(onehot.astype(BF16), before) + carry_ref[:, 0:1]
    r1 = jnp.sum(jnp.where(hit1, prefix, 0.0), axis=0, keepdims=True)
    r2 = jnp.sum(jnp.where(hit2, prefix, 0.0), axis=0, keepdims=True)
    rank_ref[...] = jnp.concatenate([r1, r2], axis=0).astype(jnp.int32)
    carry_ref[...] = carry_ref[...] + jnp.sum(onehot, axis=1, keepdims=True)
    cnt_ref[...] = carry_ref[...]


def _norm_router(x, g, mod4, comp_shift, comp_scale, w_router_t, router_bias, geom):
    n, d = x.shape
    n_exp = w_router_t.shape[0]
    tm = 256
    row2 = pl.BlockSpec((2, tm), lambda i: (0, i))
    return pl.pallas_call(
        functools.partial(_router_kernel, n_exp=n_exp, n_groups=N_EXPERT_GROUPS),
        grid=(n // tm,),
        in_specs=[
            pl.BlockSpec((tm, d), lambda i: (i, 0)),
            pl.BlockSpec((1, d), lambda i: (0, 0)),
            _mod_spec(d, tm, geom, comp_shift),
            _mod_spec(d, tm, geom, comp_scale),
            pl.BlockSpec((n_exp, d), lambda i: (0, 0)),
            pl.BlockSpec((n_exp, 1), lambda i: (0, 0)),
        ],
        out_specs=[pl.BlockSpec((tm, d // 2), lambda i: (i, 0)), row2, row2, row2,
                   pl.BlockSpec((n_exp, LANES), lambda i: (0, 0))],
        out_shape=[jax.ShapeDtypeStruct((n, d // 2), jnp.uint32),
                   jax.ShapeDtypeStruct((2, n), jnp.int32),
                   jax.ShapeDtypeStruct((2, n), F32),
                   jax.ShapeDtypeStruct((2, n), jnp.int32),
                   jax.ShapeDtypeStruct((n_exp, LANES), F32)],
        scratch_shapes=[pltpu.VMEM((n_exp, LANES), F32)],
        compiler_params=_cparams(("arbitrary",)),
        name="norm_router",
    )(x, g.reshape(1, d), mod4, mod4, w_router_t, router_bias.reshape(n_exp, 1))


def _row_copy(src_hbm, row, dst_vmem, r, sem):
    return pltpu.make_async_copy(src_hbm.at[pl.ds(row, 1), :], dst_vmem.at[pl.ds(r, 1), :], sem)


def _dispatch_kernel(src_ref, h_hbm, o_ref, sem):
    tg = o_ref.shape[0]

    def start(i, c):
        for u in range(2):
            r = 2 * i + u
            _row_copy(h_hbm, src_ref[0, 0, r], o_ref, r, sem).start(priority=u)
        return c

    def wait(r, c):
        _row_copy(h_hbm, 0, o_ref, r, sem).wait()
        return c

    lax.fori_loop(0, tg // 2, start, 0, unroll=8)
    lax.fori_loop(0, tg, wait, 0, unroll=16)


def _dispatch(h, src):
    p_rows = src.shape[0]
    dw = h.shape[1]
    tg = _pick(p_rows, (8 * ROUTE_TILE, 4 * ROUTE_TILE, 2 * ROUTE_TILE, ROUTE_TILE))
    return pl.pallas_call(
        _dispatch_kernel,
        grid=(p_rows // tg,),
        in_specs=[
            pl.BlockSpec((1, 1, tg), lambda i: (i, 0, 0), memory_space=pltpu.SMEM),
            pl.BlockSpec(memory_space=pl.ANY),
        ],
        out_specs=pl.BlockSpec((tg, dw), lambda i: (i, 0)),
        out_shape=jax.ShapeDtypeStruct((p_rows, dw), h.dtype),
        scratch_shapes=[pltpu.SemaphoreType.DMA(())],
        compiler_params=_cparams(("arbitrary",)),
        name="moe_dispatch",
    )(src.reshape(p_rows // tg, 1, tg), h)


def _new_expert(te_ref, t):
    return (t == 0) | (te_ref[t] != te_ref[jnp.maximum(t - 1, 0)])


def _ffn_up_kernel(te_ref, tv_ref, x_ref, wg_ref, wu_ref, o_ref, wgb_ref, wub_ref):
    t = pl.program_id(1)

    @pl.when(_new_expert(te_ref, t))
    def _():
        wgb_ref[...] = wg_ref[0].astype(BF16)
        wub_ref[...] = wu_ref[0].astype(BF16)

    @pl.when(tv_ref[t] == 1)
    def _():
        lo, hi = _unpack_bf16_pairs(x_ref[...])
        lo, hi = lo.astype(BF16), hi.astype(BF16)
        half = lo.shape[1]
        g = _dot(lo, wgb_ref[:half, :]) + _dot(hi, wgb_ref[half:, :])
        u = _dot(lo, wub_ref[:half, :]) + _dot(hi, wub_ref[half:, :])
        o_ref[...] = (_silu(g) * u).astype(o_ref.dtype)

    @pl.when(tv_ref[t] == 0)
    def _():
        o_ref[...] = jnp.zeros_like(o_ref)


def _ffn_up(xs, w_gate, w_up, layer, tile_expert, tile_valid):
    p_rows, dw = xs.shape
    d, f = w_gate.shape[2], w_gate.shape[3]
    tm = ROUTE_TILE
    tf = _pick(f, (512, 256, 128))
    w_spec = pl.BlockSpec((None, 1, d, tf), lambda j, t, te, tv: (layer, te[t], 0, j))
    grid_spec = pltpu.PrefetchScalarGridSpec(
        num_scalar_prefetch=2,
        grid=(f // tf, p_rows // tm),
        in_specs=[pl.BlockSpec((tm, dw), lambda j, t, te, tv: (t, 0)), w_spec, w_spec],
        out_specs=pl.BlockSpec((tm, tf), lambda j, t, te, tv: (t, j)),
        scratch_shapes=[pltpu.VMEM((d, tf), BF16), pltpu.VMEM((d, tf), BF16)],
    )
    return pl.pallas_call(
        _ffn_up_kernel,
        grid_spec=grid_spec,
        out_shape=jax.ShapeDtypeStruct((p_rows, f), BF16),
        compiler_params=_cparams(("arbitrary", "arbitrary")),
        name="moe_ffn_up",
    )(tile_expert, tile_valid, xs, w_gate, w_up)


def _ffn_down_kernel(te_ref, tv_ref, h_ref, wd_ref, o_ref, wdb_ref):
    t = pl.program_id(1)

    @pl.when(_new_expert(te_ref, t))
    def _():
        wdb_ref[...] = wd_ref[0].astype(BF16)

    @pl.when(tv_ref[t] == 1)
    def _():
        o_ref[...] = _pack_bf16_pairs(_dot(h_ref[...], wdb_ref[...]))

    @pl.when(tv_ref[t] == 0)
    def _():
        o_ref[...] = jnp.zeros_like(o_ref)


def _ffn_down_tile(d):
    return _pick(d, (2048, 1024, 512, 256))


def _ffn_down(hmid, w_down, layer, tile_expert, tile_valid):
    p_rows, f = hmid.shape
    d = w_down.shape[3]
    tm = ROUTE_TILE
    tn = _ffn_down_tile(d)
    grid_spec = pltpu.PrefetchScalarGridSpec(
        num_scalar_prefetch=2,
        grid=(d // tn, p_rows // tm),
        in_specs=[
            pl.BlockSpec((tm, f), lambda j, t, te, tv: (t, 0)),
            pl.BlockSpec((None, 1, f, tn), lambda j, t, te, tv: (layer, te[t], 0, j)),
        ],
        out_specs=pl.BlockSpec((tm, tn // 2), lambda j, t, te, tv: (t, j)),
        scratch_shapes=[pltpu.VMEM((f, tn), BF16)],
    )
    return pl.pallas_call(
        _ffn_down_kernel,
        grid_spec=grid_spec,
        out_shape=jax.ShapeDtypeStruct((p_rows, d // 2), jnp.uint32),
        compiler_params=_cparams(("arbitrary", "arbitrary")),
        name="moe_ffn_down",
    )(tile_expert, tile_valid, hmid, w_down)


def _combine_kernel(*refs, final_norm, ctx_tiles, tn):
    pos_ref, x_ref, wt_ref, g_ref, fg_ref, ys_hbm = refs[:6]
    out_refs = refs[6:-2]
    buf_ref, sem = refs[-2:]
    tm = x_ref.shape[0]

    def start(r, c):
        for s in range(2):
            _row_copy(ys_hbm, pos_ref[0, s, r], buf_ref.at[s], r, sem).start(priority=s)
        return c

    def wait(r, c):
        for s in range(2):
            _row_copy(ys_hbm, 0, buf_ref.at[s], r, sem).wait()
        return c

    lax.fori_loop(0, tm, start, 0, unroll=4)
    lax.fori_loop(0, tm, wait, 0, unroll=4)
    w = wt_ref[...]
    lane = _iota2((1, w.shape[1]), 1)
    w0 = jnp.sum(jnp.where(lane == 0, w, 0.0), axis=1, keepdims=True)
    w1 = jnp.sum(jnp.where(lane == 1, w, 0.0), axis=1, keepdims=True)
    pieces = []
    for j in range(2 * buf_ref.shape[2] // tn):
        seg = slice(j * tn // 2, (j + 1) * tn // 2)
        lo0, hi0 = _unpack_bf16_pairs(buf_ref[0, :, seg])
        lo1, hi1 = _unpack_bf16_pairs(buf_ref[1, :, seg])
        pieces += [w0 * lo0 + w1 * lo1, w0 * hi0 + w1 * hi1]
    x = x_ref[...] + g_ref[0, 0] * jnp.concatenate(pieces, axis=1)
    if final_norm:
        x = (x * lax.rsqrt(jnp.mean(x * x, axis=-1, keepdims=True) + EPS)) * fg_ref[...]
    if len(out_refs) == 1:
        out_refs[0][...] = x
    else:
        is_ctx = pl.program_id(0) < ctx_tiles

        @pl.when(is_ctx)
        def _():
            out_refs[0][...] = x

        @pl.when(jnp.logical_not(is_ctx))
        def _():
            out_refs[1][...] = x


def _combine(x, ys, pos, wgt_cols, mod4, comp_gate, final_g, geom, final_norm, split_out):
    n, d = x.shape
    tm = pos.shape[2]
    ctx_tiles = geom["n_ctx"] // tm
    if split_out:
        out_specs = [pl.BlockSpec((tm, d), lambda i: (jnp.minimum(i, ctx_tiles - 1), 0)),
                     pl.BlockSpec((tm, d), lambda i: (jnp.maximum(i - ctx_tiles, 0), 0))]
        out_shape = [jax.ShapeDtypeStruct((geom["n_ctx"], d), F32),
                     jax.ShapeDtypeStruct((n - geom["n_ctx"], d), F32)]
    else:
        out_specs = pl.BlockSpec((tm, d), lambda i: (i, 0))
        out_shape = jax.ShapeDtypeStruct((n, d), F32)
    return pl.pallas_call(
        functools.partial(_combine_kernel, final_norm=final_norm, ctx_tiles=ctx_tiles, tn=_ffn_down_tile(d)),
        grid=(n // tm,),
        in_specs=[
            pl.BlockSpec((1, 2, tm), lambda i: (i, 0, 0), memory_space=pltpu.SMEM),
            pl.BlockSpec((tm, d), lambda i: (i, 0)),
            pl.BlockSpec((tm, wgt_cols.shape[1]), lambda i: (i, 0)),
            _mod_spec(d, tm, geom, comp_gate),
            pl.BlockSpec((1, d), lambda i: (0, 0)),
            pl.BlockSpec(memory_space=pl.ANY),
        ],
        out_specs=out_specs,
        out_shape=out_shape,
        scratch_shapes=[pltpu.VMEM((2, tm, d // 2), jnp.uint32), pltpu.SemaphoreType.DMA(())],
        compiler_params=_cparams(("arbitrary",)),
        name="moe_combine",
    )(pos, x, wgt_cols, mod4, final_g.reshape(1, d), ys)


def _route_plan(eidx, rank, counts, n_exp, combine_tile):
    n = eidx.shape[1]
    tile = ROUTE_TILE
    n_tiles = (2 * n) // tile + n_exp
    cnt = counts.astype(jnp.int32)
    padded = ((cnt + tile - 1) // tile) * tile
    e_ids = jnp.arange(n_exp, dtype=jnp.int32)
    ends = jnp.sum(jnp.where(e_ids[None, :] <= e_ids[:, None], padded[None, :], 0), axis=1)
    offs = ends - padded
    pos = jnp.sum(jnp.where(eidx[:, :, None] == e_ids, offs, 0), axis=-1) + rank
    tile_start = jnp.arange(n_tiles, dtype=jnp.int32) * tile
    tile_expert = jnp.minimum(jnp.sum((ends[None, :] <= tile_start[:, None]).astype(jnp.int32), axis=1), n_exp - 1)
    tile_valid = (tile_start < ends[-1]).astype(jnp.int32)
    tok = jnp.broadcast_to(jnp.arange(n, dtype=jnp.int32), (2, n))
    src = jnp.zeros((n_tiles * tile,), jnp.int32).at[pos.reshape(-1)].set(tok.reshape(-1))
    pos_tiles = pos.reshape(2, n // combine_tile, combine_tile).transpose(1, 0, 2)
    return pos_tiles, src, tile_expert, tile_valid


def _s5_params(lam_re, lam_im, log_step, b_re, b_im, c_re, c_im):
    n_dir, groups, p = lam_re.shape
    gch = b_re.shape[-1]
    gps = S5_SLAB // gch
    n_slab = groups // gps
    step = jnp.exp(log_step)[..., None]
    mag = jnp.exp(lam_re * step)
    a_re = mag * jnp.cos(lam_im * step)
    a_im = mag * jnp.sin(lam_im * step)
    den = lam_re * lam_re + lam_im * lam_im
    z_re = ((a_re - 1.0) * lam_re + a_im * lam_im) / den
    z_im = (a_im * lam_re - (a_re - 1.0) * lam_im) / den
    bb_re = z_re[..., None] * b_re - z_im[..., None] * b_im
    bb_im = z_re[..., None] * b_im + z_im[..., None] * b_re
    eye = jnp.eye(gps, dtype=F32)
    bbs = jnp.stack([bb_re, bb_im], axis=2).reshape(n_dir, n_slab, gps, 2, p, gch)
    bm = jnp.einsum("dsgrpc,gh->dsgcrhp", bbs, eye).reshape(n_dir, n_slab, S5_SLAB, 2 * gps * p)
    cs = jnp.stack([c_re, -c_im], axis=0).reshape(2, n_slab, gps, gch, p)
    cm = jnp.einsum("rsgcp,gh->srgphc", cs, eye).reshape(n_slab, 2 * gps * p, S5_SLAB)
    a = jnp.stack([a_re, a_im], axis=2).reshape(n_dir, n_slab, gps, 2, p).transpose(0, 1, 3, 2, 4)
    a = a.reshape(n_dir, n_slab, 1, 2 * gps * p)
    a8 = jnp.broadcast_to(a, (n_dir, n_slab, SUBLANES, 2 * gps * p))
    return bm.astype(BF16), cm.astype(BF16), a8


def _axial_rope(t, dh):
    rows = t // GRID_W
    row = jnp.repeat(jnp.arange(rows, dtype=F32), GRID_W)
    col = (jnp.arange(rows * GRID_W) % GRID_W).astype(F32)
    n_freq = dh // 4
    inv = ROPE_BASE ** (-jnp.arange(n_freq, dtype=F32) / n_freq)
    ang = jnp.concatenate([row[:, None] * inv, col[:, None] * inv], axis=-1)
    return jnp.cos(ang), jnp.sin(ang)


def _to_time_major(u, nb, t_len, bp):
    u = u.reshape(nb, t_len, -1).transpose(1, 0, 2)
    if bp != nb:
        u = jnp.pad(u, ((0, 0), (0, bp - nb), (0, 0)))
    return u.reshape(t_len * bp, -1)


def _from_time_major(y, nb, t_len, bp):
    return y.reshape(t_len, bp, -1)[:, :nb].transpose(1, 0, 2).reshape(nb * t_len, -1)


def _s5_state_to_slabs(s_re, s_im, n_slab, bp):
    nb = s_re.shape[0]
    st = jnp.concatenate([s_re.reshape(nb, 2, n_slab, -1), s_im.reshape(nb, 2, n_slab, -1)], axis=-1)
    st = st.transpose(1, 2, 0, 3)
    return jnp.pad(st, ((0, 0), (0, 0), (0, bp - nb), (0, 0)))


def _s5_slabs_to_state(xf, nb, groups, p):
    half = xf.shape[-1] // 2
    re = xf[:, :, :nb, :half].transpose(2, 0, 1, 3).reshape(nb, 2, groups, p)
    im = xf[:, :, :nb, half:].transpose(2, 0, 1, 3).reshape(nb, 2, groups, p)
    return re, im


def _round_up(x, m):
    return (x + m - 1) // m * m


def kernel(x_prompt, x_sample, c, c_ctx, state_s5_re, state_s5_im, state_gla, state_ml_c, state_ml_n, state_ml_m, state_ret, w_ada, b_ada, norm1_g, norm2_g, w_in, s5_lambda_re, s5_lambda_im, s5_log_step, s5_b_re, s5_b_im, s5_c_re, s5_c_im, s5_d, s5_w_glu, gla_w_a, gla_b_a, gla_norm_g, ml_i_bias, ml_f_bias, ml_norm_g, ret_decay_logit, ret_norm_g, w_branch, w_out, w_router, router_bias, w_exp_gate, w_exp_up, w_exp_down, final_g):
    batch, seq, d = x_prompt.shape
    dec_batch, dec_seq, _ = x_sample.shape
    depth = w_in.shape[0]
    mix = s5_d.shape[-1]
    n_exp = w_router.shape[1]
    gla_heads = state_gla.shape[3]
    ml_heads = ml_i_bias.shape[-1]
    ret_heads = ret_decay_logit.shape[-1]
    gla_rank = gla_w_a.shape[2]
    groups, p_state = s5_lambda_re.shape[2], s5_lambda_re.shape[3]
    n_ctx, n_lat = batch * seq, dec_batch * dec_seq
    n = n_ctx + n_lat
    assert dec_batch + 1 <= SUBLANES and n_ctx % dec_seq == 0 and dec_seq % seq == 0
    geom = dict(batch=batch, seq=seq, dec_batch=dec_batch, dec_seq=dec_seq, n_ctx=n_ctx, n=n, ctx_row=dec_batch,
                mix=mix, gla_heads=gla_heads, ml_heads=ml_heads, ret_heads=ret_heads)
    n_slab = mix // S5_SLAB
    bp_ctx, bp_lat = _round_up(batch, SUBLANES), _round_up(dec_batch, SUBLANES)

    half = mix // 2
    widths = [("s5_u", mix), ("gla_q", half), ("gla_k", half), ("gla_v", mix), ("gla_g", mix), ("gla_r", gla_rank),
              ("ml_q", mix), ("ml_k", mix), ("ml_v", mix), ("ml_o", mix), ("ml_if", 4 * ml_heads),
              ("ret_q", mix), ("ret_k", mix), ("ret_v", mix), ("ret_g", mix), ("merge", 4 * d)]
    src_off, o = {}, 0
    for name, w in widths:
        src_off[name] = (o, w)
        o += w
    main_names = [nm for nm, _ in widths if nm not in ("gla_r", "ml_if", "merge")]
    cols, o = {}, 0
    for nm in main_names:
        cols[nm] = o
        o += src_off[nm][1]

    repack_tn = _pick(math.gcd(half, 4 * d), (512, 256, 128))
    main_starts = [src_off[nm][0] + off for nm in main_names for off in range(0, src_off[nm][1], repack_tn)]
    merge_starts = [src_off["merge"][0] + off for off in range(0, 4 * d, repack_tn)]
    w_in_t = jnp.swapaxes(w_in, 1, 2)

    def gate_cols(wl):
        g = jnp.concatenate([wl[:, src_off[nm][0]:src_off[nm][0] + src_off[nm][1]] for nm in ("gla_r", "ml_if")], 1)
        return jnp.pad(g, ((0, 0), (0, LANES - g.shape[1]))).astype(BF16)

    x = (x_prompt.reshape(n_ctx, d), x_sample.reshape(n_lat, d))
    c8 = jnp.zeros((SUBLANES, d), F32).at[:dec_batch].set(c).at[dec_batch].set(c_ctx)
    mod = _ada(c8, w_ada, b_ada)
    rope_tabs = _axial_rope(dec_seq, mix // ret_heads)
    w_router_t = w_router.T

    ctx_states = []
    for l in range(depth):
        mod4 = mod[l].reshape(SUBLANES, N_MOD, 1, d)
        w_main = _repack_proj(w_in_t, l, main_starts, repack_tn)
        w_merge = _repack_proj(w_in_t, l, merge_starts, repack_tn)
        w_gate = gate_cols(w_in[l])

        h = _norm_mod(x, norm1_g[l], mod4, 0, 1, geom, BF16)
        z = _matmul(h, w_main, BF16, _pick(n, (1024, 512, 256)), _pick(w_main.shape[1], (512, 256, 128)), "in_proj")
        zg = _matmul(h, w_gate, F32, _pick(n, (1024, 512, 256)), LANES, "gate_proj")
        zgt = zg[:, gla_rank:gla_rank + 4 * ml_heads].T

        bm, cm, a8 = _s5_params(s5_lambda_re[l], s5_lambda_im[l], s5_log_step[l], s5_b_re[l], s5_b_im[l],
                                s5_c_re[l], s5_c_im[l])
        dvec = s5_d[l].reshape(1, mix)
        u = z[:, :mix]
        y_c, xf = _s5_mixer(_to_time_major(u[:n_ctx], batch, seq, bp_ctx), bm, cm, a8, dvec, None, seq, bp_ctx, True)
        x0 = _s5_state_to_slabs(state_s5_re[:, l], state_s5_im[:, l], n_slab, bp_lat)
        y_l, _ = _s5_mixer(_to_time_major(u[n_ctx:], dec_batch, dec_seq, bp_lat), bm, cm, a8, dvec, x0, dec_seq,
                           bp_lat, False)
        w_glu = s5_w_glu[l].astype(BF16)
        y_s5 = (_glu(_from_time_major(y_c, batch, seq, bp_ctx), w_glu),
                _glu(_from_time_major(y_l, dec_batch, dec_seq, bp_lat), w_glu))
        s5_re_l, s5_im_l = _s5_slabs_to_state(xf, batch, groups, p_state)

        dk = half // gla_heads
        wa = gla_w_a[l].reshape(2, gla_rank, gla_heads, dk).transpose(0, 2, 1, 3)
        wa = jnp.pad(wa, ((0, 0), (0, 0), (0, LANES - gla_rank), (0, 0)))
        ba = gla_b_a[l].reshape(2, gla_heads, 1, dk)
        gp = dict(wa=wa, ba=ba, normg=gla_norm_g[l].reshape(1, mix))
        y_gla_c, gla_l = _gla_mixer(z, zg, gp, None, geom, cols, True)
        y_gla_l, _ = _gla_mixer(z, zg, gp, state_gla[:, l], geom, cols, False)

        ml_bias = jnp.stack([ml_i_bias[l], ml_f_bias[l]], axis=0).reshape(-1)
        ml_ng = ml_norm_g[l].reshape(1, mix)
        y_ml_c, ml_l = _ml_mixer(z, zg, zgt, ml_bias, ml_ng, None, l, depth, gla_rank, geom, cols, True)
        ml_init = dict(c=state_ml_c[:, l], n=state_ml_n[:, l][:, :, :, None, :], m=state_ml_m.reshape(-1))
        y_ml_l, _ = _ml_mixer(z, zg, zgt, ml_bias, ml_ng, ml_init, l, depth, gla_rank, geom, cols, False)

        lg = jax.nn.log_sigmoid(ret_decay_logit[l]).reshape(-1)
        ret_ng = ret_norm_g[l].reshape(1, mix)
        y_ret_c, ret_l = _ret_mixer(z, lg, ret_ng, None, None, geom, cols, True)
        y_ret_l, _ = _ret_mixer(z, lg, ret_ng, rope_tabs, state_ret[:, l], geom, cols, False)

        y_pairs = [y_s5, (y_gla_c, y_gla_l), (y_ml_c, y_ml_l), (y_ret_c, y_ret_l)]
        merged = _merge(h, y_pairs, w_merge, w_branch[l].astype(BF16), geom)
        x = _outproj(merged, w_out[l].astype(BF16), x, mod4, 2, geom)
        ctx_states.append((s5_re_l, s5_im_l, gla_l, ml_l[0], ml_l[1][:, :, :, 0, :], ml_l[2][:, :, :, 0, 0], ret_l))

        h2, eidx, wgt, rank, counts = _norm_router(x, norm2_g[l], mod4, 3, 4, w_router_t, router_bias, geom)
        last = l == depth - 1
        combine_tile = ROUTE_TILE if last else _pick(math.gcd(n_ctx, n_lat), (2 * ROUTE_TILE, ROUTE_TILE))
        pos_tiles, src, tile_expert, tile_valid = _route_plan(eidx, rank, counts[:, 0], n_exp, combine_tile)
        xs = _dispatch(h2, src)
        hmid = _ffn_up(xs, w_exp_gate, w_exp_up, l, tile_expert, tile_valid)
        ys = _ffn_down(hmid, w_exp_down, l, tile_expert, tile_valid)
        x = _combine(x, ys, pos_tiles, wgt.T, mod4, 5, final_g, geom, final_norm=last, split_out=last)

    y_prompt = x[0].reshape(batch, seq, d)
    y_sample = x[1].reshape(dec_batch, dec_seq, d)
    stacked = [jnp.stack([st[i] for st in ctx_states], axis=1) for i in range(7)]
    return (y_prompt, y_sample, *stacked)
```

```python
import functools
import math

import jax
import jax.numpy as jnp
from jax import lax
from jax.experimental import pallas as pl
from jax.experimental.pallas import tpu as pltpu

F32 = jnp.float32
BF16 = jnp.bfloat16
HIGHEST = lax.Precision.HIGHEST

EPS = 1e-6
GRID_W = 64
ROPE_BASE = 10000.0
GLA_TAU = 16.0
N_EXPERT_GROUPS = 4
N_MOD = 6

LANES = 128
SUBLANES = 8
S5_SLAB = LANES
GLA_CHUNK = 64
SEQ_CHUNK = 256
VMEM_LIMIT_BYTES = 56 * 1024 * 1024
ROUTE_TILE = 256


def _cparams(sem):
    return pltpu.CompilerParams(dimension_semantics=sem, vmem_limit_bytes=VMEM_LIMIT_BYTES)


def _pick(n, cands):
    for c in cands:
        if c <= n and n % c == 0:
            return c
    return n


def _nt(a, b):
    return lax.dot_general(a, b, (((1,), (1,)), ((), ())), preferred_element_type=F32)


def _tn(a, b):
    return lax.dot_general(a, b, (((0,), (0,)), ((), ())), preferred_element_type=F32)


def _dot(a, b):
    return jnp.dot(a, b, preferred_element_type=F32)


def _log_sigmoid(x):
    return jnp.minimum(x, 0.0) - jnp.log1p(jnp.exp(-jnp.abs(x)))


def _silu(x):
    return x * jax.nn.sigmoid(x)


def _iota2(shape, dim):
    return lax.broadcasted_iota(jnp.int32, shape, dim)


HI16 = 0xFFFF0000


def _pack_bf16_pairs(x):
    c = x.shape[1] // 2
    bits = lax.bitcast_convert_type(x.astype(BF16).astype(F32), jnp.uint32)
    return (bits[:, :c] >> 16) | (bits[:, c:] & jnp.uint32(HI16))


def _unpack_bf16_pairs(w):
    lo = lax.bitcast_convert_type(w << 16, F32)
    hi = lax.bitcast_convert_type(w & jnp.uint32(HI16), F32)
    return lo, hi


def _ada_kernel(c_ref, w_ref, b_ref, o_ref):
    s = _silu(c_ref[...]).astype(BF16)
    o_ref[0] = _dot(s, w_ref[0].astype(BF16)) + b_ref[0]


def _ada(c8, w_ada, b_ada):
    depth, d, n = w_ada.shape
    tn = _pick(n, (512, 256, 128))
    return pl.pallas_call(
        _ada_kernel,
        grid=(depth, n // tn),
        in_specs=[
            pl.BlockSpec((SUBLANES, d), lambda l, j: (0, 0)),
            pl.BlockSpec((1, d, tn), lambda l, j: (l, 0, j)),
            pl.BlockSpec((1, 1, tn), lambda l, j: (l, 0, j)),
        ],
        out_specs=pl.BlockSpec((1, SUBLANES, tn), lambda l, j: (l, 0, j)),
        out_shape=jax.ShapeDtypeStruct((depth, SUBLANES, n), F32),
        compiler_params=_cparams(("arbitrary", "arbitrary")),
        name="ada_mod",
    )(c8, w_ada, b_ada.reshape(depth, 1, n))


def _mod_spec(d, tm, geom, comp):
    n_ctx, dec_seq, ctx_row = geom["n_ctx"], geom["dec_seq"], geom["ctx_row"]

    def index_map(i, *_):
        start = i * tm
        row = jnp.where(start < n_ctx, ctx_row, (start - n_ctx) // dec_seq)
        return (row, comp, 0, 0)

    return pl.BlockSpec((1, 1, 1, d), index_map)


def _row_specs(x, block, ctx_tiles, col=lambda *g: 0):
    if not isinstance(x, tuple):
        return [pl.BlockSpec(block, lambda i, *g: (i, col(i, *g)))], [x]
    return ([pl.BlockSpec(block, lambda i, *g: (jnp.minimum(i, ctx_tiles - 1),
                                                jnp.where(i < ctx_tiles, col(i, *g), 0))),
             pl.BlockSpec(block, lambda i, *g: (jnp.maximum(i - ctx_tiles, 0),
                                                jnp.where(i >= ctx_tiles, col(i, *g), 0)))], list(x))


def _read_rows(refs, ctx_tiles):
    if len(refs) == 1:
        return refs[0][...]
    return jnp.where(pl.program_id(0) < ctx_tiles, refs[0][...], refs[1][...])


def _norm_kernel(*refs, n_x, ctx_tiles):
    g_ref, sh_ref, sc_ref, o_ref = refs[n_x:]
    x = _read_rows(refs[:n_x], ctx_tiles)
    y = x * lax.rsqrt(jnp.mean(x * x, axis=-1, keepdims=True) + EPS)
    h = (y * g_ref[...]) * (1.0 + sc_ref[0, 0]) + sh_ref[0, 0]
    o_ref[...] = h.astype(o_ref.dtype)


def _norm_mod(x, g, mod4, comp_shift, comp_scale, geom, out_dtype):
    d = g.shape[0]
    n = geom["n"]
    tm = 256
    ctx_tiles = geom["n_ctx"] // tm
    x_specs, x_args = _row_specs(x, (tm, d), ctx_tiles)
    return pl.pallas_call(
        functools.partial(_norm_kernel, n_x=len(x_args), ctx_tiles=ctx_tiles),
        grid=(n // tm,),
        in_specs=x_specs + [
            pl.BlockSpec((1, d), lambda i: (0, 0)),
            _mod_spec(d, tm, geom, comp_shift),
            _mod_spec(d, tm, geom, comp_scale),
        ],
        out_specs=pl.BlockSpec((tm, d), lambda i: (i, 0)),
        out_shape=jax.ShapeDtypeStruct((n, d), out_dtype),
        compiler_params=_cparams(("arbitrary",)),
        name="norm_mod",
    )(*x_args, g.reshape(1, d), mod4, mod4)


def _mm_kernel(x_ref, w_ref, o_ref):
    x = x_ref[...]
    o_ref[...] = _dot(x, w_ref[...].astype(x.dtype)).astype(o_ref.dtype)


def _matmul(x, w, out_dtype, tm, tn, name):
    m, k = x.shape
    n = w.shape[1]
    return pl.pallas_call(
        _mm_kernel,
        grid=(m // tm, n // tn),
        in_specs=[
            pl.BlockSpec((tm, k), lambda i, j: (i, 0)),
            pl.BlockSpec((k, tn), lambda i, j: (0, j)),
        ],
        out_specs=pl.BlockSpec((tm, tn), lambda i, j: (i, j)),
        out_shape=jax.ShapeDtypeStruct((m, n), out_dtype),
        compiler_params=_cparams(("arbitrary", "arbitrary")),
        name=name,
    )(x, w)


def _repack_kernel(starts_ref, wt_ref, o_ref):
    del starts_ref
    o_ref[...] = wt_ref[...].T.astype(o_ref.dtype)


def _repack_proj(w_in_t, layer, starts, tn):
    d = w_in_t.shape[2]
    grid_spec = pltpu.PrefetchScalarGridSpec(
        num_scalar_prefetch=1,
        grid=(len(starts),),
        in_specs=[pl.BlockSpec((None, pl.Element(tn), pl.Element(d)),
                               lambda j, st: (layer, pl.multiple_of(st[j], 2 * SUBLANES), 0))],
        out_specs=pl.BlockSpec((d, tn), lambda j, st: (0, j)),
    )
    return pl.pallas_call(
        _repack_kernel,
        grid_spec=grid_spec,
        out_shape=jax.ShapeDtypeStruct((d, len(starts) * tn), BF16),
        compiler_params=_cparams(("arbitrary",)),
        name="repack_proj",
    )(jnp.asarray(starts, jnp.int32), w_in_t)


def _s5_kernel(*refs, t_len, bp, tc, has_init, want_final):
    it = iter(refs)
    u_ref, bm_ref, cm_ref, a_ref, d_ref = (next(it) for _ in range(5))
    x0_ref = next(it) if has_init else None
    y_ref = next(it)
    xf_ref = next(it) if want_final else None
    bu_ref, yacc_ref, st_ref = next(it), next(it), next(it)

    nc = t_len // tc
    rc = tc * bp
    half = bu_ref.shape[1] // 2
    for d in (0, 1):
        if has_init:
            st_ref[...] = x0_ref[d, 0]
        else:
            st_ref[...] = jnp.zeros_like(st_ref)
        bm = bm_ref[d, 0]
        ar = a_ref[d, 0, :, :half]
        ai = a_ref[d, 0, :, half:]

        def chunk_body(c, carry, d=d, bm=bm, ar=ar, ai=ai):
            cc = c if d == 0 else nc - 1 - c
            r0 = pl.multiple_of(cc * rc, rc)
            bu_ref[...] = _dot(u_ref[pl.ds(r0, rc), :], bm)
            for rt in range(bp // SUBLANES):
                rows = slice(rt * SUBLANES, (rt + 1) * SUBLANES)

                def step(t, s, rt=rt):
                    sr, si = s
                    tt = t if d == 0 else tc - 1 - t
                    row = pl.multiple_of(tt * bp + rt * SUBLANES, SUBLANES)
                    b = bu_ref[pl.ds(row, SUBLANES), :]
                    xr = ar * sr - ai * si + b[:, :half]
                    xi = ar * si + ai * sr + b[:, half:]
                    bu_ref[pl.ds(row, SUBLANES), :] = jnp.concatenate([xr, xi], axis=-1)
                    return xr, xi

                sr, si = lax.fori_loop(0, tc, step, (st_ref[rows, :half], st_ref[rows, half:]),
                                       unroll=math.gcd(tc, 4))
                st_ref[rows, :] = jnp.concatenate([sr, si], axis=-1)
            y = _dot(bu_ref[...].astype(BF16), cm_ref[0])
            if d == 0:
                yacc_ref[pl.ds(r0, rc), :] = y
            else:
                yacc_ref[pl.ds(r0, rc), :] += y
            return carry

        lax.fori_loop(0, nc, chunk_body, 0)
        if want_final:
            xf_ref[d, 0] = st_ref[...]
    y = yacc_ref[...] + d_ref[...] * u_ref[...].astype(F32)
    y_ref[...] = jax.nn.gelu(y).astype(y_ref.dtype)


def _s5_mixer(u_tm, bm, cm, a8, dvec, x0, t_len, bp, want_final):
    rows, mix = u_tm.shape
    n_slab = mix // S5_SLAB
    two_half = bm.shape[-1]
    tc = _pick(t_len, tuple(max(1, 1024 // bp) >> s for s in range(6)))
    has_init = x0 is not None
    in_specs = [
        pl.BlockSpec((rows, S5_SLAB), lambda s: (0, s)),
        pl.BlockSpec((2, 1, S5_SLAB, two_half), lambda s: (0, s, 0, 0)),
        pl.BlockSpec((1, two_half, S5_SLAB), lambda s: (s, 0, 0)),
        pl.BlockSpec((2, 1, SUBLANES, two_half), lambda s: (0, s, 0, 0)),
        pl.BlockSpec((1, S5_SLAB), lambda s: (0, s)),
    ]
    args = [u_tm, bm, cm, a8, dvec]
    if has_init:
        in_specs.append(pl.BlockSpec((2, 1, bp, two_half), lambda s: (0, s, 0, 0)))
        args.append(x0)
    out_specs = [pl.BlockSpec((rows, S5_SLAB), lambda s: (0, s))]
    out_shape = [jax.ShapeDtypeStruct((rows, mix), BF16)]
    if want_final:
        out_specs.append(pl.BlockSpec((2, 1, bp, two_half), lambda s: (0, s, 0, 0)))
        out_shape.append(jax.ShapeDtypeStruct((2, n_slab, bp, two_half), F32))
    res = pl.pallas_call(
        functools.partial(_s5_kernel, t_len=t_len, bp=bp, tc=tc, has_init=has_init, want_final=want_final),
        grid=(n_slab,),
        in_specs=in_specs,
        out_specs=out_specs,
        out_shape=out_shape,
        scratch_shapes=[
            pltpu.VMEM((tc * bp, two_half), F32),
            pltpu.VMEM((rows, S5_SLAB), F32),
            pltpu.VMEM((bp, two_half), F32),
        ],
        compiler_params=_cparams(("arbitrary",)),
        name="s5_ctx" if want_final else "s5_lat",
    )(*args)
    return (res[0], res[1]) if want_final else (res[0], None)


def _glu_kernel(y_ref, w_ref, o_ref):
    y = y_ref[...]
    z = _dot(y, w_ref[...])
    o_ref[...] = (y.astype(F32) * jax.nn.sigmoid(z)).astype(o_ref.dtype)


def _glu(y, w):
    n, mix = y.shape
    tm = _pick(n, (512, 256, 128))
    return pl.pallas_call(
        _glu_kernel,
        grid=(n // tm,),
        in_specs=[pl.BlockSpec((tm, mix), lambda i: (i, 0)), pl.BlockSpec((mix, mix), lambda i: (0, 0))],
        out_specs=pl.BlockSpec((tm, mix), lambda i: (i, 0)),
        out_shape=jax.ShapeDtypeStruct((n, mix), BF16),
        compiler_params=_cparams(("arbitrary",)),
        name="s5_glu",
    )(y, w)


def _head_norm_gate(o, normg_ref, gate, act):
    of = o * lax.rsqrt(jnp.mean(o * o, axis=-1, keepdims=True) + EPS)
    return of * normg_ref[...] * act(gate)


def _seq_geometry(geom, ctx):
    if ctx:
        return geom["batch"], geom["seq"], 0
    return geom["dec_batch"], geom["dec_seq"], geom["n_ctx"] // geom["dec_seq"]


def _col_spec(t_len, width, col0, blk0):
    base = col0 // width
    return pl.BlockSpec((t_len, width), lambda b, h: (blk0 + b, base + h))


def _launch_mixer(body, name, geom, ctx, width, in_specs, args, extra_out_specs, extra_out_shapes, scratch, **kw):
    nb, t_len, _ = _seq_geometry(geom, ctx)
    heads = geom["mix"] // width
    out_specs = [pl.BlockSpec((t_len, width), lambda b, h: (b, h))] + list(extra_out_specs)
    out_shapes = [jax.ShapeDtypeStruct((nb * t_len, geom["mix"]), BF16)] + list(extra_out_shapes)
    return pl.pallas_call(
        functools.partial(body, **kw),
        grid=(nb, heads),
        in_specs=list(in_specs),
        out_specs=out_specs,
        out_shape=out_shapes,
        scratch_shapes=scratch,
        compiler_params=_cparams(("arbitrary", "arbitrary")),
        name=name,
    )(*args)


def _gla_kernel(*refs, t_len, dk, has_init, want_final):
    it = iter(refs)
    q_ref, k_ref, v_ref, g_ref, zg_ref, wa_ref, ba_ref, normg_ref = (next(it) for _ in range(8))
    s0_ref = next(it) if has_init else None
    y_ref = next(it)
    sf_ref = next(it) if want_final else None
    of_ref, ob_ref, cumf_ref, cumb_ref, stf_ref, stb_ref = (next(it) for _ in range(6))

    cl = min(GLA_CHUNK, t_len)
    nc = t_len // cl
    blk = min(SEQ_CHUNK, t_len)
    scale = dk ** -0.5
    dirs = ((of_ref, cumf_ref, stf_ref), (ob_ref, cumb_ref, stb_ref))

    zg = zg_ref[...]
    ti = _iota2((blk, blk), 0)
    si = _iota2((blk, blk), 1)
    same_chunk = (ti // cl) == (si // cl)
    for d, (_, cum_ref, st_ref) in enumerate(dirs):
        la = _log_sigmoid(jnp.dot(zg, wa_ref[d, 0], precision=HIGHEST, preferred_element_type=F32)
                          + ba_ref[d, 0]) / GLA_TAU
        tri = jnp.where(same_chunk & ((si <= ti) if d == 0 else (si >= ti)), 1.0, 0.0)
        for p in range(t_len // blk):
            rows = slice(p * blk, (p + 1) * blk)
            cum_ref[rows, :] = jnp.dot(tri, la[rows], precision=HIGHEST, preferred_element_type=F32)
        st_ref[...] = s0_ref[0, d, 0].T if has_init else jnp.zeros_like(st_ref)

    tl = _iota2((cl, cl), 0)
    sl = _iota2((cl, cl), 1)
    keeps = (sl <= tl, sl >= tl)

    def chunk_body(c, carry):
        for d, (o_ref, cum_ref, st_ref) in enumerate(dirs):
            cc = c if d == 0 else nc - 1 - c
            r0 = pl.multiple_of(cc * cl, cl)
            cum = cum_ref[pl.ds(r0, cl), :]
            tot = cum[cl - 1:cl, :] if d == 0 else cum[0:1, :]
            q = q_ref[pl.ds(r0, cl), :].astype(F32) * scale
            k = k_ref[pl.ds(r0, cl), :].astype(F32)
            v = v_ref[pl.ds(r0, cl), :]
            qd = (q * jnp.exp(cum)).astype(BF16)
            kd = (k * jnp.exp(-cum)).astype(BF16)
            sc = jnp.where(keeps[d], _nt(qd, kd), 0.0)
            st = st_ref[...]
            o_ref[pl.ds(r0, cl), :] = _nt(qd, st.astype(BF16)) + _dot(sc.astype(BF16), v)
            kl = (k * jnp.exp(tot - cum)).astype(BF16)
            st_ref[...] = st * jnp.exp(tot) + _tn(v, kl)
        return carry

    lax.fori_loop(0, nc, chunk_body, 0)
    if want_final:
        sf_ref[0, 0, 0] = stf_ref[...].T
        sf_ref[0, 1, 0] = stb_ref[...].T
    y = _head_norm_gate(of_ref[...] + ob_ref[...], normg_ref, g_ref[...].astype(F32), _silu)
    y_ref[...] = y.astype(y_ref.dtype)


def _gla_mixer(z, zg, gp, s0, geom, cols, ctx):
    nb, t_len, blk0 = _seq_geometry(geom, ctx)
    mix, heads = geom["mix"], geom["gla_heads"]
    dk, dv = mix // 2 // heads, mix // heads
    has_init, want_final = s0 is not None, ctx
    in_specs = [
        _col_spec(t_len, dk, cols["gla_q"], blk0),
        _col_spec(t_len, dk, cols["gla_k"], blk0),
        _col_spec(t_len, dv, cols["gla_v"], blk0),
        _col_spec(t_len, dv, cols["gla_g"], blk0),
        pl.BlockSpec((t_len, LANES), lambda b, h: (blk0 + b, 0)),
        pl.BlockSpec((2, 1, LANES, dk), lambda b, h: (0, h, 0, 0)),
        pl.BlockSpec((2, 1, 1, dk), lambda b, h: (0, h, 0, 0)),
        pl.BlockSpec((1, dv), lambda b, h: (0, h)),
    ]
    args = [z, z, z, z, zg, gp["wa"], gp["ba"], gp["normg"]]
    if has_init:
        in_specs.append(pl.BlockSpec((1, 2, 1, dk, dv), lambda b, h: (b, 0, h, 0, 0)))
        args.append(s0)
    x_specs, x_shapes = [], []
    if want_final:
        x_specs.append(pl.BlockSpec((1, 2, 1, dk, dv), lambda b, h: (b, 0, h, 0, 0)))
        x_shapes.append(jax.ShapeDtypeStruct((nb, 2, heads, dk, dv), F32))
    res = _launch_mixer(_gla_kernel, "gla_ctx" if ctx else "gla_lat", geom, ctx, dv, in_specs, args,
                        x_specs, x_shapes,
                        [pltpu.VMEM((t_len, dv), F32)] * 2 + [pltpu.VMEM((t_len, dk), F32)] * 2
                        + [pltpu.VMEM((dv, dk), F32)] * 2,
                        t_len=t_len, dk=dk, has_init=has_init, want_final=want_final)
    return (res[0], res[1]) if want_final else (res[0], None)


def _ret_kernel(*refs, t_len, dh, has_init, want_final, rope):
    it = iter(refs)
    lg_ref = next(it)
    q_ref, k_ref, v_ref, g_ref, normg_ref = (next(it) for _ in range(5))
    cos_ref, sin_ref = (next(it), next(it)) if rope else (None, None)
    s0_ref = next(it) if has_init else None
    y_ref = next(it)
    sf_ref = next(it) if want_final else None
    o_ref = next(it)

    h = pl.program_id(1)
    n_heads = lg_ref.shape[0] // 2
    lgf = lg_ref[h]
    lgb = lg_ref[n_heads + h]
    cl = min(SEQ_CHUNK, t_len)
    nc = t_len // cl
    scale = dh ** -0.5
    half = dh // 2

    q = q_ref[...].astype(F32)
    k = k_ref[...].astype(F32) * scale
    if rope:
        cos = cos_ref[...]
        sin = sin_ref[...]

        def rot(x):
            x1, x2 = x[:, :half], x[:, half:]
            return jnp.concatenate([x1 * cos - x2 * sin, x1 * sin + x2 * cos], axis=-1)

        q, k = rot(q), rot(k)
    qb = q.astype(BF16)
    kb = k.astype(BF16)

    dt = (_iota2((cl, cl), 0) - _iota2((cl, cl), 1)).astype(F32)
    decay = (jnp.where(dt >= 0, jnp.exp(lgf * jnp.maximum(dt, 0.0)), 0.0)
             + jnp.where(dt <= 0, jnp.exp(lgb * jnp.maximum(-dt, 0.0)), 0.0))
    tcol = _iota2((cl, 1), 0).astype(F32)

    def rows(c):
        return slice(c * cl, (c + 1) * cl)

    s_f = s0_ref[0, 0, 0] if has_init else None
    for c in range(nc):
        qc, kc, vc = qb[rows(c)], kb[rows(c)], v_ref[rows(c), :]
        o = _dot((_nt(qc, kc) * decay).astype(BF16), vc)
        if s_f is not None:
            o = o + _dot((q[rows(c)] * jnp.exp(lgf * (tcol + 1.0))).astype(BF16), s_f.astype(BF16))
        o_ref[rows(c), :] = o
        if c < nc - 1 or want_final:
            upd = _tn((k[rows(c)] * jnp.exp(lgf * (cl - 1.0 - tcol))).astype(BF16), vc)
            s_f = upd if s_f is None else jnp.exp(lgf * cl) * s_f + upd
    if want_final:
        sf_ref[0, 0, 0] = s_f
    s_b = s0_ref[0, 1, 0] if has_init else None
    for c in range(nc - 1, -1, -1):
        vc = v_ref[rows(c), :]
        if s_b is not None:
            o_ref[rows(c), :] += _dot((q[rows(c)] * jnp.exp(lgb * (cl - tcol))).astype(BF16), s_b.astype(BF16))
        if c > 0 or want_final:
            upd = _tn((k[rows(c)] * jnp.exp(lgb * tcol)).astype(BF16), vc)
            s_b = upd if s_b is None else jnp.exp(lgb * cl) * s_b + upd
    if want_final:
        sf_ref[0, 1, 0] = s_b
    y = _head_norm_gate(o_ref[...], normg_ref, g_ref[...].astype(F32), _silu)
    y_ref[...] = y.astype(y_ref.dtype)


def _ret_mixer(z, lg, normg, rope_tabs, s0, geom, cols, ctx):
    nb, t_len, blk0 = _seq_geometry(geom, ctx)
    mix, heads = geom["mix"], geom["ret_heads"]
    dh = mix // heads
    has_init, want_final, rope = s0 is not None, ctx, rope_tabs is not None
    in_specs = [
        pl.BlockSpec(memory_space=pltpu.SMEM),
        _col_spec(t_len, dh, cols["ret_q"], blk0),
        _col_spec(t_len, dh, cols["ret_k"], blk0),
        _col_spec(t_len, dh, cols["ret_v"], blk0),
        _col_spec(t_len, dh, cols["ret_g"], blk0),
        pl.BlockSpec((1, dh), lambda b, h: (0, h)),
    ]
    args = [lg, z, z, z, z, normg]
    if rope:
        in_specs += [pl.BlockSpec((t_len, dh // 2), lambda b, h: (0, 0))] * 2
        args += list(rope_tabs)
    if has_init:
        in_specs.append(pl.BlockSpec((1, 2, 1, dh, dh), lambda b, h: (b, 0, h, 0, 0)))
        args.append(s0)
    x_specs, x_shapes = [], []
    if want_final:
        x_specs.append(pl.BlockSpec((1, 2, 1, dh, dh), lambda b, h: (b, 0, h, 0, 0)))
        x_shapes.append(jax.ShapeDtypeStruct((nb, 2, heads, dh, dh), F32))
    res = _launch_mixer(_ret_kernel, "ret_ctx" if ctx else "ret_lat", geom, ctx, dh, in_specs, args,
                        x_specs, x_shapes, [pltpu.VMEM((t_len, dh), F32)],
                        t_len=t_len, dh=dh, has_init=has_init, want_final=want_final, rope=rope)
    return (res[0], res[1]) if want_final else (res[0], None)


def _ml_kernel(*refs, t_len, dh, n_heads, layer, depth, gate_lane0, has_init, want_final):
    it = iter(refs)
    bias_ref = next(it)
    m0_ref = next(it) if has_init else None
    q_ref, k_ref, v_ref, og_ref, zg_ref, zgt_ref, normg_ref = (next(it) for _ in range(7))
    c0_ref, n0_ref = (next(it), next(it)) if has_init else (None, None)
    y_ref = next(it)
    cf_ref, nf_ref, mf_ref = (next(it), next(it), next(it)) if want_final else (None, None, None)
    o_ref = next(it)

    b = pl.program_id(0)
    h = pl.program_id(1)
    cl = min(SEQ_CHUNK, t_len)
    nc = t_len // cl
    scale = dh ** -0.5

    qb = q_ref[...]
    qf = qb.astype(F32)
    kf = k_ref[...].astype(F32) * scale
    kb = kf.astype(BF16)

    zg = zg_ref[...]
    zgt = zgt_ref[...]
    lane = _iota2((1, zg.shape[1]), 1)
    sub = _iota2((zgt.shape[0], 1), 0)
    ti = _iota2((cl, cl), 0)
    si = _iota2((cl, cl), 1)

    def rows(c):
        return slice(c * cl, (c + 1) * cl)

    for d in (0, 1):
        gi = d * 2 * n_heads + h
        gf = gi + n_heads
        bi = bias_ref[d * n_heads + h]
        bf = bias_ref[(2 + d) * n_heads + h]
        i_col = jnp.sum(jnp.where(lane == gate_lane0 + gi, zg, 0.0), axis=1, keepdims=True) + bi
        f_col = _log_sigmoid(jnp.sum(jnp.where(lane == gate_lane0 + gf, zg, 0.0), axis=1, keepdims=True) + bf)
        i_row = jnp.sum(jnp.where(sub == gi, zgt, 0.0), axis=0, keepdims=True) + bi
        f_row = _log_sigmoid(jnp.sum(jnp.where(sub == gf, zgt, 0.0), axis=0, keepdims=True) + bf)
        keep = (si <= ti) if d == 0 else (si >= ti)
        keep_t = (ti <= si) if d == 0 else (ti >= si)

        if has_init:
            c_st = c0_ref[0, d, 0]
            n_st = n0_ref[0, d, 0]
            m_st = jnp.full((1, 1), m0_ref[((b * depth + layer) * 2 + d) * n_heads + h], F32)
        else:
            c_st = None
            n_st = None
            m_st = jnp.zeros((1, 1), F32)

        order = range(nc) if d == 0 else range(nc - 1, -1, -1)
        for pos, c in enumerate(order):
            rc = rows(c)
            qc, kc, vc = qb[rc], kb[rc], v_ref[rc, :]
            ic = i_col[rc]
            ir, fr = i_row[:, rc], f_row[:, rc]
            f_cum_col = jnp.sum(jnp.where(keep, fr, 0.0), axis=1, keepdims=True)
            f_cum_row = jnp.sum(jnp.where(keep_t, f_col[rc], 0.0), axis=0, keepdims=True)
            dmat = jnp.where(keep, f_cum_col + (ir - f_cum_row), -jnp.inf)
            g = f_cum_col + m_st
            m_t = jnp.maximum(g, jnp.max(dmat, axis=1, keepdims=True))
            p = _nt(qc, kc) * jnp.exp(dmat - m_t)
            num = _dot(p.astype(BF16), vc)
            den = jnp.sum(p, axis=1, keepdims=True)
            if c_st is not None:
                w_state = jnp.exp(g - m_t)
                num = num + w_state * _dot(qc, c_st.astype(BF16))
                den = den + w_state * jnp.sum(qf[rc] * n_st, axis=1, keepdims=True)
            hh = num / jnp.maximum(jnp.abs(den), jnp.exp(-m_t))
            if d == 0:
                o_ref[rc, :] = hh
            else:
                o_ref[rc, :] += hh
            if pos < nc - 1 or want_final:
                f_last = jnp.sum(fr, axis=1, keepdims=True)
                src = f_last - f_cum_col + ic
                m_new = jnp.maximum(f_last + m_st, jnp.max(src, axis=0, keepdims=True))
                kw = kf[rc] * jnp.exp(src - m_new)
                upd_c = _tn(kw.astype(BF16), vc)
                upd_n = jnp.sum(kw, axis=0, keepdims=True)
                if c_st is not None:
                    w_keep = jnp.exp(f_last + m_st - m_new)
                    c_st = w_keep * c_st + upd_c
                    n_st = w_keep * n_st + upd_n
                else:
                    c_st, n_st = upd_c, upd_n
                m_st = m_new
        if want_final:
            cf_ref[0, d, 0] = c_st
            nf_ref[0, d, 0] = n_st
            mf_ref[0, d, 0] = jnp.broadcast_to(m_st, (1, mf_ref.shape[-1]))
    y = _head_norm_gate(o_ref[...], normg_ref, og_ref[...].astype(F32), jax.nn.sigmoid)
    y_ref[...] = y.astype(y_ref.dtype)


def _ml_mixer(z, zg, zgt, bias, normg, init, layer, depth, gate_lane0, geom, cols, ctx):
    nb, t_len, blk0 = _seq_geometry(geom, ctx)
    mix, heads = geom["mix"], geom["ml_heads"]
    dh = mix // heads
    has_init, want_final = init is not None, ctx
    smem = pl.BlockSpec(memory_space=pltpu.SMEM)
    in_specs, args = [smem], [bias]
    if has_init:
        in_specs.append(smem)
        args.append(init["m"])
    in_specs += [
        _col_spec(t_len, dh, cols["ml_q"], blk0),
        _col_spec(t_len, dh, cols["ml_k"], blk0),
        _col_spec(t_len, dh, cols["ml_v"], blk0),
        _col_spec(t_len, dh, cols["ml_o"], blk0),
        pl.BlockSpec((t_len, LANES), lambda b, h: (blk0 + b, 0)),
        pl.BlockSpec((zgt.shape[0], t_len), lambda b, h: (0, blk0 + b)),
        pl.BlockSpec((1, dh), lambda b, h: (0, h)),
    ]
    args += [z, z, z, z, zg, zgt, normg]
    if has_init:
        in_specs += [pl.BlockSpec((1, 2, 1, dh, dh), lambda b, h: (b, 0, h, 0, 0)),
                     pl.BlockSpec((1, 2, 1, 1, dh), lambda b, h: (b, 0, h, 0, 0))]
        args += [init["c"], init["n"]]
    x_specs, x_shapes = [], []
    if want_final:
        x_specs = [pl.BlockSpec((1, 2, 1, dh, dh), lambda b, h: (b, 0, h, 0, 0)),
                   pl.BlockSpec((1, 2, 1, 1, dh), lambda b, h: (b, 0, h, 0, 0)),
                   pl.BlockSpec((1, 2, 1, 1, LANES), lambda b, h: (b, 0, h, 0, 0))]
        x_shapes = [jax.ShapeDtypeStruct((nb, 2, heads, dh, dh), F32),
                    jax.ShapeDtypeStruct((nb, 2, heads, 1, dh), F32),
                    jax.ShapeDtypeStruct((nb, 2, heads, 1, LANES), F32)]
    res = _launch_mixer(_ml_kernel, "mlstm_ctx" if ctx else "mlstm_lat", geom, ctx, dh, in_specs, args,
                        x_specs, x_shapes, [pltpu.VMEM((t_len, dh), F32)],
                        t_len=t_len, dh=dh, n_heads=heads, layer=layer, depth=depth, gate_lane0=gate_lane0,
                        has_init=has_init, want_final=want_final)
    return (res[0], res[1:]) if want_final else (res[0], None)


def _merge_kernel(*refs, nbr, ctx_tiles):
    h_ref = refs[0]
    y_refs = refs[1:1 + 2 * nbr]
    m_refs = refs[1 + 2 * nbr:1 + 3 * nbr]
    wb_ref, o_ref = refs[1 + 3 * nbr], refs[2 + 3 * nbr]
    is_ctx = pl.program_id(0) < ctx_tiles
    h = h_ref[...]
    acc = None
    for i in range(nbr):
        y = jnp.where(is_ctx, y_refs[2 * i][...], y_refs[2 * i + 1][...])
        term = jax.nn.sigmoid(_dot(h, m_refs[i][...])) * _dot(y, wb_ref[i].astype(y.dtype))
        acc = term if acc is None else acc + term
    o_ref[...] = acc.astype(o_ref.dtype)


def _merge(h, y_pairs, w_merge, w_branch, layer, geom):
    n, d = h.shape
    mix = geom["mix"]
    nbr = len(y_pairs)
    tm = _pick(math.gcd(geom["n_ctx"], n - geom["n_ctx"]), (512, 256, 128))
    tn = _pick(d, (256, 128))
    nj = d // tn
    ctx_tiles = geom["n_ctx"] // tm
    in_specs = [pl.BlockSpec((tm, d), lambda i, j: (i, 0))]
    for _ in range(nbr):
        in_specs.append(pl.BlockSpec((tm, mix), lambda i, j: (jnp.minimum(i, ctx_tiles - 1), 0)))
        in_specs.append(pl.BlockSpec((tm, mix), lambda i, j: (jnp.maximum(i - ctx_tiles, 0), 0)))
    in_specs += [pl.BlockSpec((d, tn), lambda i, j, br=br: (0, br * nj + j)) for br in range(nbr)]
    in_specs += [pl.BlockSpec((None, nbr, mix, tn), lambda i, j: (layer, 0, 0, j))]
    flat = [y for pair in y_pairs for y in pair]
    return pl.pallas_call(
        functools.partial(_merge_kernel, nbr=nbr, ctx_tiles=ctx_tiles),
        grid=(n // tm, nj),
        in_specs=in_specs,
        out_specs=pl.BlockSpec((tm, tn), lambda i, j: (i, j)),
        out_shape=jax.ShapeDtypeStruct((n, d), BF16),
        compiler_params=_cparams(("arbitrary", "arbitrary")),
        name="merge",
    )(h, *flat, *([w_merge] * nbr), w_branch)


def _outproj_kernel(*refs, n_x, ctx_tiles):
    m_ref, w_ref = refs[:2]
    g_ref, o_ref = refs[2 + n_x:]
    x = _read_rows(refs[2:2 + n_x], ctx_tiles)
    m = m_ref[...]
    o_ref[...] = x + g_ref[0, 0] * _dot(m, w_ref[...].astype(m.dtype))


def _outproj(merged, w_out, layer, x, mod4, comp_gate, geom):
    n, d = merged.shape
    tm = _pick(math.gcd(geom["n_ctx"], geom["dec_seq"]), (1024, 512, 256, 128))
    tn = _pick(d, (512, 256, 128))
    n_ctx, dec_seq, ctx_row = geom["n_ctx"], geom["dec_seq"], geom["ctx_row"]
    ctx_tiles = n_ctx // tm

    def gate_map(i, j):
        start = i * tm
        return (jnp.where(start < n_ctx, ctx_row, (start - n_ctx) // dec_seq), comp_gate, 0, j)

    x_specs, x_args = _row_specs(x, (tm, tn), ctx_tiles, col=lambda i, j: j)
    return pl.pallas_call(
        functools.partial(_outproj_kernel, n_x=len(x_args), ctx_tiles=ctx_tiles),
        grid=(n // tm, d // tn),
        in_specs=[
            pl.BlockSpec((tm, d), lambda i, j: (i, 0)),
            pl.BlockSpec((None, d, tn), lambda i, j: (layer, 0, j)),
        ] + x_specs + [pl.BlockSpec((1, 1, 1, tn), gate_map)],
        out_specs=pl.BlockSpec((tm, tn), lambda i, j: (i, j)),
        out_shape=jax.ShapeDtypeStruct((n, d), F32),
        compiler_params=_cparams(("arbitrary", "arbitrary")),
        name="out_proj",
    )(merged, w_out, *x_args, mod4)


def _first_argmax(vals):
    best, idx = vals[0], jnp.zeros_like(vals[0])
    for j in range(1, len(vals)):
        better = vals[j] > best
        idx = jnp.where(better, float(j), idx)
        best = jnp.where(better, vals[j], best)
    return idx, best


def _pick_row(vals, idx):
    out = vals[0]
    for j in range(1, len(vals)):
        out = jnp.where(idx == float(j), vals[j], out)
    return out


def _router_kernel(x_ref, g_ref, sh_ref, sc_ref, wrt_ref, rb_ref, h_ref, eidx_ref, wgt_ref, rank_ref, cnt_ref,
                   carry_ref, *, n_exp, n_groups):
    i = pl.program_id(0)
    x = x_ref[...]
    y = x * lax.rsqrt(jnp.mean(x * x, axis=-1, keepdims=True) + EPS)
    h = (y * g_ref[...]) * (1.0 + sc_ref[0, 0]) + sh_ref[0, 0]
    h_ref[...] = _pack_bf16_pairs(h)
    tm = x.shape[0]
    per = n_exp // n_groups

    logits = lax.dot_general(wrt_ref[...], h, (((1,), (1,)), ((), ())), precision=HIGHEST,
                             preferred_element_type=F32)
    scores = jax.nn.sigmoid(logits)
    sel = scores + rb_ref[...]
    sel_rows = [sel[e:e + 1, :] for e in range(n_exp)]
    sc_rows = [scores[e:e + 1, :] for e in range(n_exp)]
    group_scores = []
    for g in range(n_groups):
        r = sel_rows[g * per:(g + 1) * per]
        pair = None
        for a in range(per):
            for bb in range(a + 1, per):
                s2 = r[a] + r[bb]
                pair = s2 if pair is None else jnp.maximum(pair, s2)
        group_scores.append(pair)
    g_best, _ = _first_argmax(group_scores)
    in_sel = [_pick_row([sel_rows[g * per + j] for g in range(n_groups)], g_best) for j in range(per)]
    in_sc = [_pick_row([sc_rows[g * per + j] for g in range(n_groups)], g_best) for j in range(per)]
    j1, _ = _first_argmax(in_sel)
    masked = [jnp.where(j1 == float(j), -jnp.inf, in_sel[j]) for j in range(per)]
    j2, _ = _first_argmax(masked)
    w1 = _pick_row(in_sc, j1)
    w2 = _pick_row(in_sc, j2)
    tot = w1 + w2
    e1 = (g_best * per + j1).astype(jnp.int32)
    e2 = (g_best * per + j2).astype(jnp.int32)
    eidx_ref[...] = jnp.concatenate([e1, e2], axis=0)
    wgt_ref[...] = jnp.concatenate([w1 / tot, w2 / tot], axis=0)

    @pl.when(i == 0)
    def _():
        carry_ref[...] = jnp.zeros_like(carry_ref)

    eio = _iota2((n_exp, tm), 0)
    hit1 = eio == e1
    hit2 = eio == e2
    onehot = jnp.where(hit1 | hit2, 1.0, 0.0)
    before = (_iota2((tm, tm), 0) < _iota2((tm, tm), 1)).astype(BF16)
    prefix = _dot(onehot.astype(BF16), before) + carry_ref[:, 0:1]
    r1 = jnp.sum(jnp.where(hit1, prefix, 0.0), axis=0, keepdims=True)
    r2 = jnp.sum(jnp.where(hit2, prefix, 0.0), axis=0, keepdims=True)
    rank_ref[...] = jnp.concatenate([r1, r2], axis=0).astype(jnp.int32)
    carry_ref[...] = carry_ref[...] + jnp.sum(onehot, axis=1, keepdims=True)
    cnt_ref[...] = carry_ref[...]


def _norm_router(x, g, mod4, comp_shift, comp_scale, w_router_t, router_bias, geom):
    n, d = x.shape
    n_exp = w_router_t.shape[0]
    tm = 256
    row2 = pl.BlockSpec((2, tm), lambda i: (0, i))
    return pl.pallas_call(
        functools.partial(_router_kernel, n_exp=n_exp, n_groups=N_EXPERT_GROUPS),
        grid=(n // tm,),
        in_specs=[
            pl.BlockSpec((tm, d), lambda i: (i, 0)),
            pl.BlockSpec((1, d), lambda i: (0, 0)),
            _mod_spec(d, tm, geom, comp_shift),
            _mod_spec(d, tm, geom, comp_scale),
            pl.BlockSpec((n_exp, d), lambda i: (0, 0)),
            pl.BlockSpec((n_exp, 1), lambda i: (0, 0)),
        ],
        out_specs=[pl.BlockSpec((tm, d // 2), lambda i: (i, 0)), row2, row2, row2,
                   pl.BlockSpec((n_exp, LANES), lambda i: (0, 0))],
        out_shape=[jax.ShapeDtypeStruct((n, d // 2), jnp.uint32),
                   jax.ShapeDtypeStruct((2, n), jnp.int32),
                   jax.ShapeDtypeStruct((2, n), F32),
                   jax.ShapeDtypeStruct((2, n), jnp.int32),
                   jax.ShapeDtypeStruct((n_exp, LANES), F32)],
        scratch_shapes=[pltpu.VMEM((n_exp, LANES), F32)],
        compiler_params=_cparams(("arbitrary",)),
        name="norm_router",
    )(x, g.reshape(1, d), mod4, mod4, w_router_t, router_bias.reshape(n_exp, 1))


def _row_copy(src_hbm, row, dst_vmem, r, sem):
    return pltpu.make_async_copy(src_hbm.at[pl.ds(row, 1), :], dst_vmem.at[pl.ds(r, 1), :], sem)


def _dispatch_kernel(src_ref, h_hbm, o_ref, sem):
    tg = o_ref.shape[0]

    def start(i, c):
        for u in range(2):
            r = 2 * i + u
            _row_copy(h_hbm, src_ref[0, 0, r], o_ref, r, sem).start(priority=u)
        return c

    def wait(r, c):
        _row_copy(h_hbm, 0, o_ref, r, sem).wait()
        return c

    lax.fori_loop(0, tg // 2, start, 0, unroll=8)
    lax.fori_loop(0, tg, wait, 0, unroll=16)


def _dispatch(h, src):
    p_rows = src.shape[0]
    dw = h.shape[1]
    tg = _pick(p_rows, (8 * ROUTE_TILE, 4 * ROUTE_TILE, 2 * ROUTE_TILE, ROUTE_TILE))
    return pl.pallas_call(
        _dispatch_kernel,
        grid=(p_rows // tg,),
        in_specs=[
            pl.BlockSpec((1, 1, tg), lambda i: (i, 0, 0), memory_space=pltpu.SMEM),
            pl.BlockSpec(memory_space=pl.ANY),
        ],
        out_specs=pl.BlockSpec((tg, dw), lambda i: (i, 0)),
        out_shape=jax.ShapeDtypeStruct((p_rows, dw), h.dtype),
        scratch_shapes=[pltpu.SemaphoreType.DMA(())],
        compiler_params=_cparams(("arbitrary",)),
        name="moe_dispatch",
    )(src.reshape(p_rows // tg, 1, tg), h)


def _new_expert(te_ref, t):
    return (t == 0) | (te_ref[t] != te_ref[jnp.maximum(t - 1, 0)])


def _ffn_up_kernel(te_ref, tv_ref, x_ref, wg_ref, wu_ref, o_ref, wgb_ref, wub_ref):
    t = pl.program_id(1)

    @pl.when(_new_expert(te_ref, t))
    def _():
        wgb_ref[...] = wg_ref[0].astype(BF16)
        wub_ref[...] = wu_ref[0].astype(BF16)

    @pl.when(tv_ref[t] == 1)
    def _():
        lo, hi = _unpack_bf16_pairs(x_ref[...])
        lo, hi = lo.astype(BF16), hi.astype(BF16)
        half = lo.shape[1]
        g = _dot(lo, wgb_ref[:half, :]) + _dot(hi, wgb_ref[half:, :])
        u = _dot(lo, wub_ref[:half, :]) + _dot(hi, wub_ref[half:, :])
        o_ref[...] = (_silu(g) * u).astype(o_ref.dtype)

    @pl.when(tv_ref[t] == 0)
    def _():
        o_ref[...] = jnp.zeros_like(o_ref)


def _ffn_up(xs, w_gate, w_up, layer, tile_expert, tile_valid):
    p_rows, dw = xs.shape
    d, f = w_gate.shape[2], w_gate.shape[3]
    tm = ROUTE_TILE
    tf = _pick(f, (512, 256, 128))
    w_spec = pl.BlockSpec((None, 1, d, tf), lambda j, t, te, tv: (layer, te[t], 0, j))
    grid_spec = pltpu.PrefetchScalarGridSpec(
        num_scalar_prefetch=2,
        grid=(f // tf, p_rows // tm),
        in_specs=[pl.BlockSpec((tm, dw), lambda j, t, te, tv: (t, 0)), w_spec, w_spec],
        out_specs=pl.BlockSpec((tm, tf), lambda j, t, te, tv: (t, j)),
        scratch_shapes=[pltpu.VMEM((d, tf), BF16), pltpu.VMEM((d, tf), BF16)],
    )
    return pl.pallas_call(
        _ffn_up_kernel,
        grid_spec=grid_spec,
        out_shape=jax.ShapeDtypeStruct((p_rows, f), BF16),
        compiler_params=_cparams(("arbitrary", "arbitrary")),
        name="moe_ffn_up",
    )(tile_expert, tile_valid, xs, w_gate, w_up)


def _ffn_down_kernel(te_ref, tv_ref, h_ref, wd_ref, o_ref, wdb_ref):
    t = pl.program_id(1)

    @pl.when(_new_expert(te_ref, t))
    def _():
        wdb_ref[...] = wd_ref[0].astype(BF16)

    @pl.when(tv_ref[t] == 1)
    def _():
        o_ref[...] = _pack_bf16_pairs(_dot(h_ref[...], wdb_ref[...]))

    @pl.when(tv_ref[t] == 0)
    def _():
        o_ref[...] = jnp.zeros_like(o_ref)


def _ffn_down_tile(d):
    return _pick(d, (2048, 1024, 512, 256))


def _ffn_down(hmid, w_down, layer, tile_expert, tile_valid):
    p_rows, f = hmid.shape
    d = w_down.shape[3]
    tm = ROUTE_TILE
    tn = _ffn_down_tile(d)
    grid_spec = pltpu.PrefetchScalarGridSpec(
        num_scalar_prefetch=2,
        grid=(d // tn, p_rows // tm),
        in_specs=[
            pl.BlockSpec((tm, f), lambda j, t, te, tv: (t, 0)),
            pl.BlockSpec((None, 1, f, tn), lambda j, t, te, tv: (layer, te[t], 0, j)),
        ],
        out_specs=pl.BlockSpec((tm, tn // 2), lambda j, t, te, tv: (t, j)),
        scratch_shapes=[pltpu.VMEM((f, tn), BF16)],
    )
    return pl.pallas_call(
        _ffn_down_kernel,
        grid_spec=grid_spec,
        out_shape=jax.ShapeDtypeStruct((p_rows, d // 2), jnp.uint32),
        compiler_params=_cparams(("arbitrary", "arbitrary")),
        name="moe_ffn_down",
    )(tile_expert, tile_valid, hmid, w_down)


def _combine_kernel(*refs, final_norm, ctx_tiles, tn):
    pos_ref, x_ref, wt_ref, g_ref, fg_ref, ys_hbm = refs[:6]
    out_refs = refs[6:-2]
    buf_ref, sem = refs[-2:]
    tm = x_ref.shape[0]

    def start(r, c):
        for s in range(2):
            _row_copy(ys_hbm, pos_ref[0, s, r], buf_ref.at[s], r, sem).start(priority=s)
        return c

    def wait(r, c):
        for s in range(2):
            _row_copy(ys_hbm, 0, buf_ref.at[s], r, sem).wait()
        return c

    lax.fori_loop(0, tm, start, 0, unroll=4)
    lax.fori_loop(0, tm, wait, 0, unroll=4)

    d = x_ref.shape[1]
    rg = 2 * SUBLANES
    cw = min(2 * LANES, tn // 2)
    gate = g_ref[0, 0]
    lane = _iota2((1, wt_ref.shape[1]), 1)
    pieces = [(j * tn // 2 + p, j * tn + p, j * tn + tn // 2 + p)
              for j in range(d // tn) for p in range(0, tn // 2, cw)]

    def fill(o_ref):
        def rows_body(gi, c):
            rows = pl.ds(pl.multiple_of(gi * rg, rg), rg)
            w = wt_ref[rows, :]
            w0 = jnp.sum(jnp.where(lane == 0, w, 0.0), axis=1, keepdims=True)
            w1 = jnp.sum(jnp.where(lane == 1, w, 0.0), axis=1, keepdims=True)
            ss = jnp.zeros((rg, 1), F32)
            for p0, c_lo, c_hi in pieces:
                lo0, hi0 = _unpack_bf16_pairs(buf_ref[0, rows, p0:p0 + cw])
                lo1, hi1 = _unpack_bf16_pairs(buf_ref[1, rows, p0:p0 + cw])
                for col, moe in ((c_lo, w0 * lo0 + w1 * lo1), (c_hi, w0 * hi0 + w1 * hi1)):
                    xv = x_ref[rows, col:col + cw] + gate[:, col:col + cw] * moe
                    o_ref[rows, col:col + cw] = xv
                    if final_norm:
                        ss = ss + jnp.sum(xv * xv, axis=1, keepdims=True)
            if final_norm:
                scale = lax.rsqrt(ss / d + EPS)
                for col in range(0, d, cw):
                    o_ref[rows, col:col + cw] = (o_ref[rows, col:col + cw] * scale) * fg_ref[:, col:col + cw]
            return c

        lax.fori_loop(0, tm // rg, rows_body, 0)

    if len(out_refs) == 1:
        fill(out_refs[0])
    else:
        is_ctx = pl.program_id(0) < ctx_tiles
        pl.when(is_ctx)(lambda: fill(out_refs[0]))
        pl.when(jnp.logical_not(is_ctx))(lambda: fill(out_refs[1]))


def _combine(x, ys, pos, wgt_cols, mod4, comp_gate, final_g, geom, final_norm, split_out):
    n, d = x.shape
    tm = pos.shape[2]
    ctx_tiles = geom["n_ctx"] // tm
    if split_out:
        out_specs = [pl.BlockSpec((tm, d), lambda i: (jnp.minimum(i, ctx_tiles - 1), 0)),
                     pl.BlockSpec((tm, d), lambda i: (jnp.maximum(i - ctx_tiles, 0), 0))]
        out_shape = [jax.ShapeDtypeStruct((geom["n_ctx"], d), F32),
                     jax.ShapeDtypeStruct((n - geom["n_ctx"], d), F32)]
    else:
        out_specs = pl.BlockSpec((tm, d), lambda i: (i, 0))
        out_shape = jax.ShapeDtypeStruct((n, d), F32)
    return pl.pallas_call(
        functools.partial(_combine_kernel, final_norm=final_norm, ctx_tiles=ctx_tiles, tn=_ffn_down_tile(d)),
        grid=(n // tm,),
        in_specs=[
            pl.BlockSpec((1, 2, tm), lambda i: (i, 0, 0), memory_space=pltpu.SMEM),
            pl.BlockSpec((tm, d), lambda i: (i, 0)),
            pl.BlockSpec((tm, wgt_cols.shape[1]), lambda i: (i, 0)),
            _mod_spec(d, tm, geom, comp_gate),
            pl.BlockSpec((1, d), lambda i: (0, 0)),
            pl.BlockSpec(memory_space=pl.ANY),
        ],
        out_specs=out_specs,
        out_shape=out_shape,
        scratch_shapes=[pltpu.VMEM((2, tm, d // 2), jnp.uint32), pltpu.SemaphoreType.DMA(())],
        compiler_params=_cparams(("arbitrary",)),
        name="moe_combine",
    )(pos, x, wgt_cols, mod4, final_g.reshape(1, d), ys)


def _route_plan(eidx, rank, counts, n_exp, combine_tile):
    n = eidx.shape[1]
    tile = ROUTE_TILE
    n_tiles = (2 * n) // tile + n_exp
    cnt = counts.astype(jnp.int32)
    padded = ((cnt + tile - 1) // tile) * tile
    e_ids = jnp.arange(n_exp, dtype=jnp.int32)
    ends = jnp.sum(jnp.where(e_ids[None, :] <= e_ids[:, None], padded[None, :], 0), axis=1)
    offs = ends - padded
    pos = jnp.sum(jnp.where(eidx[:, :, None] == e_ids, offs, 0), axis=-1) + rank
    tile_start = jnp.arange(n_tiles, dtype=jnp.int32) * tile
    tile_expert = jnp.minimum(jnp.sum((ends[None, :] <= tile_start[:, None]).astype(jnp.int32), axis=1), n_exp - 1)
    tile_valid = (tile_start < ends[-1]).astype(jnp.int32)
    tok = jnp.broadcast_to(jnp.arange(n, dtype=jnp.int32), (2, n))
    src = jnp.zeros((n_tiles * tile,), jnp.int32).at[pos.reshape(-1)].set(tok.reshape(-1))
    pos_tiles = pos.reshape(2, n // combine_tile, combine_tile).transpose(1, 0, 2)
    return pos_tiles, src, tile_expert, tile_valid


def _s5_params(lam_re, lam_im, log_step, b_re, b_im, c_re, c_im):
    n_dir, groups, p = lam_re.shape
    gch = b_re.shape[-1]
    gps = S5_SLAB // gch
    n_slab = groups // gps
    step = jnp.exp(log_step)[..., None]
    mag = jnp.exp(lam_re * step)
    a_re = mag * jnp.cos(lam_im * step)
    a_im = mag * jnp.sin(lam_im * step)
    den = lam_re * lam_re + lam_im * lam_im
    z_re = ((a_re - 1.0) * lam_re + a_im * lam_im) / den
    z_im = (a_im * lam_re - (a_re - 1.0) * lam_im) / den
    bb_re = z_re[..., None] * b_re - z_im[..., None] * b_im
    bb_im = z_re[..., None] * b_im + z_im[..., None] * b_re
    eye = jnp.eye(gps, dtype=F32)
    bbs = jnp.stack([bb_re, bb_im], axis=2).reshape(n_dir, n_slab, gps, 2, p, gch)
    bm = jnp.einsum("dsgrpc,gh->dsgcrhp", bbs, eye).reshape(n_dir, n_slab, S5_SLAB, 2 * gps * p)
    cs = jnp.stack([c_re, -c_im], axis=0).reshape(2, n_slab, gps, gch, p)
    cm = jnp.einsum("rsgcp,gh->srgphc", cs, eye).reshape(n_slab, 2 * gps * p, S5_SLAB)
    a = jnp.stack([a_re, a_im], axis=2).reshape(n_dir, n_slab, gps, 2, p).transpose(0, 1, 3, 2, 4)
    a = a.reshape(n_dir, n_slab, 1, 2 * gps * p)
    a8 = jnp.broadcast_to(a, (n_dir, n_slab, SUBLANES, 2 * gps * p))
    return bm.astype(BF16), cm.astype(BF16), a8


def _axial_rope(t, dh):
    rows = t // GRID_W
    row = jnp.repeat(jnp.arange(rows, dtype=F32), GRID_W)
    col = (jnp.arange(rows * GRID_W) % GRID_W).astype(F32)
    n_freq = dh // 4
    inv = ROPE_BASE ** (-jnp.arange(n_freq, dtype=F32) / n_freq)
    ang = jnp.concatenate([row[:, None] * inv, col[:, None] * inv], axis=-1)
    return jnp.cos(ang), jnp.sin(ang)


def _to_time_major(u, nb, t_len, bp):
    u = u.reshape(nb, t_len, -1).transpose(1, 0, 2)
    if bp != nb:
        u = jnp.pad(u, ((0, 0), (0, bp - nb), (0, 0)))
    return u.reshape(t_len * bp, -1)


def _from_time_major(y, nb, t_len, bp):
    return y.reshape(t_len, bp, -1)[:, :nb].transpose(1, 0, 2).reshape(nb * t_len, -1)


def _s5_state_to_slabs(s_re, s_im, n_slab, bp):
    nb = s_re.shape[0]
    st = jnp.concatenate([s_re.reshape(nb, 2, n_slab, -1), s_im.reshape(nb, 2, n_slab, -1)], axis=-1)
    st = st.transpose(1, 2, 0, 3)
    return jnp.pad(st, ((0, 0), (0, 0), (0, bp - nb), (0, 0)))


def _s5_slabs_to_state(xf, nb, groups, p):
    half = xf.shape[-1] // 2
    re = xf[:, :, :nb, :half].transpose(2, 0, 1, 3).reshape(nb, 2, groups, p)
    im = xf[:, :, :nb, half:].transpose(2, 0, 1, 3).reshape(nb, 2, groups, p)
    return re, im


def _round_up(x, m):
    return (x + m - 1) // m * m


def kernel(x_prompt, x_sample, c, c_ctx, state_s5_re, state_s5_im, state_gla, state_ml_c, state_ml_n, state_ml_m, state_ret, w_ada, b_ada, norm1_g, norm2_g, w_in, s5_lambda_re, s5_lambda_im, s5_log_step, s5_b_re, s5_b_im, s5_c_re, s5_c_im, s5_d, s5_w_glu, gla_w_a, gla_b_a, gla_norm_g, ml_i_bias, ml_f_bias, ml_norm_g, ret_decay_logit, ret_norm_g, w_branch, w_out, w_router, router_bias, w_exp_gate, w_exp_up, w_exp_down, final_g):
    batch, seq, d = x_prompt.shape
    dec_batch, dec_seq, _ = x_sample.shape
    depth = w_in.shape[0]
    mix = s5_d.shape[-1]
    n_exp = w_router.shape[1]
    gla_heads = state_gla.shape[3]
    ml_heads = ml_i_bias.shape[-1]
    ret_heads = ret_decay_logit.shape[-1]
    gla_rank = gla_w_a.shape[2]
    groups, p_state = s5_lambda_re.shape[2], s5_lambda_re.shape[3]
    n_ctx, n_lat = batch * seq, dec_batch * dec_seq
    n = n_ctx + n_lat
    assert dec_batch + 1 <= SUBLANES and n_ctx % dec_seq == 0 and dec_seq % seq == 0
    geom = dict(batch=batch, seq=seq, dec_batch=dec_batch, dec_seq=dec_seq, n_ctx=n_ctx, n=n, ctx_row=dec_batch,
                mix=mix, gla_heads=gla_heads, ml_heads=ml_heads, ret_heads=ret_heads)
    n_slab = mix // S5_SLAB
    bp_ctx, bp_lat = _round_up(batch, SUBLANES), _round_up(dec_batch, SUBLANES)

    half = mix // 2
    widths = [("s5_u", mix), ("gla_q", half), ("gla_k", half), ("gla_v", mix), ("gla_g", mix), ("gla_r", gla_rank),
              ("ml_q", mix), ("ml_k", mix), ("ml_v", mix), ("ml_o", mix), ("ml_if", 4 * ml_heads),
              ("ret_q", mix), ("ret_k", mix), ("ret_v", mix), ("ret_g", mix), ("merge", 4 * d)]
    src_off, o = {}, 0
    for name, w in widths:
        src_off[name] = (o, w)
        o += w
    main_names = [nm for nm, _ in widths if nm not in ("gla_r", "ml_if", "merge")]
    cols, o = {}, 0
    for nm in main_names:
        cols[nm] = o
        o += src_off[nm][1]

    repack_tn = _pick(math.gcd(half, 4 * d), (512, 256, 128))
    main_starts = [src_off[nm][0] + off for nm in main_names for off in range(0, src_off[nm][1], repack_tn)]
    merge_starts = [src_off["merge"][0] + off for off in range(0, 4 * d, repack_tn)]
    w_in_t = jnp.swapaxes(w_in, 1, 2)

    def gate_cols(wl):
        g = jnp.concatenate([wl[:, src_off[nm][0]:src_off[nm][0] + src_off[nm][1]] for nm in ("gla_r", "ml_if")], 1)
        return jnp.pad(g, ((0, 0), (0, LANES - g.shape[1])))

    x = (x_prompt.reshape(n_ctx, d), x_sample.reshape(n_lat, d))
    c8 = jnp.zeros((SUBLANES, d), F32).at[:dec_batch].set(c).at[dec_batch].set(c_ctx)
    mod = _ada(c8, w_ada, b_ada)
    rope_tabs = _axial_rope(dec_seq, mix // ret_heads)
    w_router_t = w_router.T

    ctx_states = []
    for l in range(depth):
        mod4 = mod[l].reshape(SUBLANES, N_MOD, 1, d)
        w_main = _repack_proj(w_in_t, l, main_starts, repack_tn)
        w_merge = _repack_proj(w_in_t, l, merge_starts, repack_tn)
        w_gate = gate_cols(w_in[l])

        h = _norm_mod(x, norm1_g[l], mod4, 0, 1, geom, BF16)
        z = _matmul(h, w_main, BF16, _pick(n, (1024, 512, 256)), _pick(w_main.shape[1], (512, 256, 128)), "in_proj")
        zg = _matmul(h, w_gate, F32, _pick(n, (1024, 512, 256)), LANES, "gate_proj")
        zgt = zg[:, gla_rank:gla_rank + 4 * ml_heads].T

        bm, cm, a8 = _s5_params(s5_lambda_re[l], s5_lambda_im[l], s5_log_step[l], s5_b_re[l], s5_b_im[l],
                                s5_c_re[l], s5_c_im[l])
        dvec = s5_d[l].reshape(1, mix)
        u = z[:, :mix]
        y_c, xf = _s5_mixer(_to_time_major(u[:n_ctx], batch, seq, bp_ctx), bm, cm, a8, dvec, None, seq, bp_ctx, True)
        x0 = _s5_state_to_slabs(state_s5_re[:, l], state_s5_im[:, l], n_slab, bp_lat)
        y_l, _ = _s5_mixer(_to_time_major(u[n_ctx:], dec_batch, dec_seq, bp_lat), bm, cm, a8, dvec, x0, dec_seq,
                           bp_lat, False)
        w_glu = s5_w_glu[l].astype(BF16)
        y_s5 = (_glu(_from_time_major(y_c, batch, seq, bp_ctx), w_glu),
                _glu(_from_time_major(y_l, dec_batch, dec_seq, bp_lat), w_glu))
        s5_re_l, s5_im_l = _s5_slabs_to_state(xf, batch, groups, p_state)

        dk = half // gla_heads
        wa = gla_w_a[l].reshape(2, gla_rank, gla_heads, dk).transpose(0, 2, 1, 3)
        wa = jnp.pad(wa, ((0, 0), (0, 0), (0, LANES - gla_rank), (0, 0)))
        ba = gla_b_a[l].reshape(2, gla_heads, 1, dk)
        gp = dict(wa=wa, ba=ba, normg=gla_norm_g[l].reshape(1, mix))
        y_gla_c, gla_l = _gla_mixer(z, zg, gp, None, geom, cols, True)
        y_gla_l, _ = _gla_mixer(z, zg, gp, state_gla[:, l], geom, cols, False)

        ml_bias = jnp.stack([ml_i_bias[l], ml_f_bias[l]], axis=0).reshape(-1)
        ml_ng = ml_norm_g[l].reshape(1, mix)
        y_ml_c, ml_l = _ml_mixer(z, zg, zgt, ml_bias, ml_ng, None, l, depth, gla_rank, geom, cols, True)
        ml_init = dict(c=state_ml_c[:, l], n=state_ml_n[:, l][:, :, :, None, :], m=state_ml_m.reshape(-1))
        y_ml_l, _ = _ml_mixer(z, zg, zgt, ml_bias, ml_ng, ml_init, l, depth, gla_rank, geom, cols, False)

        lg = jax.nn.log_sigmoid(ret_decay_logit[l]).reshape(-1)
        ret_ng = ret_norm_g[l].reshape(1, mix)
        y_ret_c, ret_l = _ret_mixer(z, lg, ret_ng, None, None, geom, cols, True)
        y_ret_l, _ = _ret_mixer(z, lg, ret_ng, rope_tabs, state_ret[:, l], geom, cols, False)

        y_pairs = [y_s5, (y_gla_c, y_gla_l), (y_ml_c, y_ml_l), (y_ret_c, y_ret_l)]
        merged = _merge(h, y_pairs, w_merge, w_branch, l, geom)
        x = _outproj(merged, w_out, l, x, mod4, 2, geom)
        ctx_states.append((s5_re_l, s5_im_l, gla_l, ml_l[0], ml_l[1][:, :, :, 0, :], ml_l[2][:, :, :, 0, 0], ret_l))

        h2, eidx, wgt, rank, counts = _norm_router(x, norm2_g[l], mod4, 3, 4, w_router_t, router_bias, geom)
        last = l == depth - 1
        combine_tile = ROUTE_TILE if last else _pick(math.gcd(n_ctx, n_lat), (2 * ROUTE_TILE, ROUTE_TILE))
        pos_tiles, src, tile_expert, tile_valid = _route_plan(eidx, rank, counts[:, 0], n_exp, combine_tile)
        xs = _dispatch(h2, src)
        hmid = _ffn_up(xs, w_exp_gate, w_exp_up, l, tile_expert, tile_valid)
        ys = _ffn_down(hmid, w_exp_down, l, tile_expert, tile_valid)
        x = _combine(x, ys, pos_tiles, wgt.T, mod4, 5, final_g, geom, final_norm=last, split_out=last)

    y_prompt = x[0].reshape(batch, seq, d)
    y_sample = x[1].reshape(dec_batch, dec_seq, d)
    stacked = [jnp.stack([st[i] for st in ctx_states], axis=1) for i in range(7)]
    return (y_prompt, y_sample, *stacked)
```

```python
import functools
import math

import jax
import jax.numpy as jnp
from jax import lax
from jax.experimental import pallas as pl
from jax.experimental.pallas import tpu as pltpu

F32 = jnp.float32
BF16 = jnp.bfloat16
HIGHEST = lax.Precision.HIGHEST

EPS = 1e-6
GRID_W = 64
ROPE_BASE = 10000.0
GLA_TAU = 16.0
N_EXPERT_GROUPS = 4
N_MOD = 6

LANES = 128
SUBLANES = 8
S5_SLAB = LANES
GLA_CHUNK = 64
SEQ_CHUNK = 256
VMEM_LIMIT_BYTES = 56 * 1024 * 1024
ROUTE_TILE = 256


def _cparams(sem):
    return pltpu.CompilerParams(dimension_semantics=sem, vmem_limit_bytes=VMEM_LIMIT_BYTES)


def _pick(n, cands):
    for c in cands:
        if c <= n and n % c == 0:
            return c
    return n


def _nt(a, b):
    return lax.dot_general(a, b, (((1,), (1,)), ((), ())), preferred_element_type=F32)


def _tn(a, b):
    return lax.dot_general(a, b, (((0,), (0,)), ((), ())), preferred_element_type=F32)


def _dot(a, b):
    return jnp.dot(a, b, preferred_element_type=F32)


def _log_sigmoid(x):
    return jnp.minimum(x, 0.0) - jnp.log1p(jnp.exp(-jnp.abs(x)))


def _silu(x):
    return x * jax.nn.sigmoid(x)


def _iota2(shape, dim):
    return lax.broadcasted_iota(jnp.int32, shape, dim)


HI16 = 0xFFFF0000


def _pack_bf16_pairs(x):
    c = x.shape[1] // 2
    bits = lax.bitcast_convert_type(x.astype(BF16).astype(F32), jnp.uint32)
    return (bits[:, :c] >> 16) | (bits[:, c:] & jnp.uint32(HI16))


def _unpack_bf16_pairs(w):
    lo = lax.bitcast_convert_type(w << 16, F32)
    hi = lax.bitcast_convert_type(w & jnp.uint32(HI16), F32)
    return lo, hi


def _ada_kernel(c_ref, w_ref, b_ref, o_ref):
    s = _silu(c_ref[...]).astype(BF16)
    o_ref[0] = _dot(s, w_ref[0].astype(BF16)) + b_ref[0]


def _ada(c8, w_ada, b_ada):
    depth, d, n = w_ada.shape
    tn = _pick(n, (512, 256, 128))
    return pl.pallas_call(
        _ada_kernel,
        grid=(depth, n // tn),
        in_specs=[
            pl.BlockSpec((SUBLANES, d), lambda l, j: (0, 0)),
            pl.BlockSpec((1, d, tn), lambda l, j: (l, 0, j)),
            pl.BlockSpec((1, 1, tn), lambda l, j: (l, 0, j)),
        ],
        out_specs=pl.BlockSpec((1, SUBLANES, tn), lambda l, j: (l, 0, j)),
        out_shape=jax.ShapeDtypeStruct((depth, SUBLANES, n), F32),
        compiler_params=_cparams(("arbitrary", "arbitrary")),
        name="ada_mod",
    )(c8, w_ada, b_ada.reshape(depth, 1, n))


def _mod_spec(d, tm, geom, comp):
    n_ctx, dec_seq, ctx_row = geom["n_ctx"], geom["dec_seq"], geom["ctx_row"]

    def index_map(i, *_):
        start = i * tm
        row = jnp.where(start < n_ctx, ctx_row, (start - n_ctx) // dec_seq)
        return (row, comp, 0, 0)

    return pl.BlockSpec((1, 1, 1, d), index_map)


def _row_specs(x, block, ctx_tiles, col=lambda *g: 0):
    if not isinstance(x, tuple):
        return [pl.BlockSpec(block, lambda i, *g: (i, col(i, *g)))], [x]
    return ([pl.BlockSpec(block, lambda i, *g: (jnp.minimum(i, ctx_tiles - 1),
                                                jnp.where(i < ctx_tiles, col(i, *g), 0))),
             pl.BlockSpec(block, lambda i, *g: (jnp.maximum(i - ctx_tiles, 0),
                                                jnp.where(i >= ctx_tiles, col(i, *g), 0)))], list(x))


def _read_rows(refs, ctx_tiles):
    if len(refs) == 1:
        return refs[0][...]
    return jnp.where(pl.program_id(0) < ctx_tiles, refs[0][...], refs[1][...])


def _norm_kernel(*refs, n_x, ctx_tiles):
    g_ref, sh_ref, sc_ref, o_ref = refs[n_x:]
    x = _read_rows(refs[:n_x], ctx_tiles)
    y = x * lax.rsqrt(jnp.mean(x * x, axis=-1, keepdims=True) + EPS)
    h = (y * g_ref[...]) * (1.0 + sc_ref[0, 0]) + sh_ref[0, 0]
    o_ref[...] = h.astype(o_ref.dtype)


def _norm_mod(x, g, mod4, comp_shift, comp_scale, geom, out_dtype):
    d = g.shape[0]
    n = geom["n"]
    tm = 256
    ctx_tiles = geom["n_ctx"] // tm
    x_specs, x_args = _row_specs(x, (tm, d), ctx_tiles)
    return pl.pallas_call(
        functools.partial(_norm_kernel, n_x=len(x_args), ctx_tiles=ctx_tiles),
        grid=(n // tm,),
        in_specs=x_specs + [
            pl.BlockSpec((1, d), lambda i: (0, 0)),
            _mod_spec(d, tm, geom, comp_shift),
            _mod_spec(d, tm, geom, comp_scale),
        ],
        out_specs=pl.BlockSpec((tm, d), lambda i: (i, 0)),
        out_shape=jax.ShapeDtypeStruct((n, d), out_dtype),
        compiler_params=_cparams(("arbitrary",)),
        name="norm_mod",
    )(*x_args, g.reshape(1, d), mod4, mod4)


def _mm_kernel(x_ref, w_ref, o_ref):
    x = x_ref[...]
    o_ref[...] = _dot(x, w_ref[...].astype(x.dtype)).astype(o_ref.dtype)


def _matmul(x, w, out_dtype, tm, tn, name):
    m, k = x.shape
    n = w.shape[1]
    return pl.pallas_call(
        _mm_kernel,
        grid=(m // tm, n // tn),
        in_specs=[
            pl.BlockSpec((tm, k), lambda i, j: (i, 0)),
            pl.BlockSpec((k, tn), lambda i, j: (0, j)),
        ],
        out_specs=pl.BlockSpec((tm, tn), lambda i, j: (i, j)),
        out_shape=jax.ShapeDtypeStruct((m, n), out_dtype),
        compiler_params=_cparams(("arbitrary", "arbitrary")),
        name=name,
    )(x, w)


def _repack_kernel(starts_ref, wt_ref, o_ref):
    del starts_ref
    o_ref[...] = wt_ref[...].T.astype(o_ref.dtype)


def _repack_proj(w_in_t, layer, starts, tn):
    d = w_in_t.shape[2]
    grid_spec = pltpu.PrefetchScalarGridSpec(
        num_scalar_prefetch=1,
        grid=(len(starts),),
        in_specs=[pl.BlockSpec((None, pl.Element(tn), pl.Element(d)),
                               lambda j, st: (layer, pl.multiple_of(st[j], 2 * SUBLANES), 0))],
        out_specs=pl.BlockSpec((d, tn), lambda j, st: (0, j)),
    )
    return pl.pallas_call(
        _repack_kernel,
        grid_spec=grid_spec,
        out_shape=jax.ShapeDtypeStruct((d, len(starts) * tn), BF16),
        compiler_params=_cparams(("arbitrary",)),
        name="repack_proj",
    )(jnp.asarray(starts, jnp.int32), w_in_t)


def _s5_kernel(*refs, t_len, bp, tc, has_init, want_final):
    it = iter(refs)
    u_ref, bm_ref, cm_ref, a_ref, d_ref = (next(it) for _ in range(5))
    x0_ref = next(it) if has_init else None
    y_ref = next(it)
    xf_ref = next(it) if want_final else None
    bu_refs = (next(it), next(it))
    y_refs = (next(it), next(it))
    st_refs = (next(it), next(it))

    nc = t_len // tc
    rc = tc * bp
    half = bu_refs[0].shape[1] // 2
    for d in (0, 1):
        st_refs[d][...] = x0_ref[d, 0] if has_init else jnp.zeros_like(st_refs[d])
    a_parts = [(a_ref[d, 0, :, :half], a_ref[d, 0, :, half:]) for d in (0, 1)]

    def chunk_body(c, carry):
        r0s = (pl.multiple_of(c * rc, rc), pl.multiple_of((nc - 1 - c) * rc, rc))
        for d in (0, 1):
            bu_refs[d][...] = _dot(u_ref[pl.ds(r0s[d], rc), :], bm_ref[d, 0])
        for rt in range(bp // SUBLANES):
            rows = slice(rt * SUBLANES, (rt + 1) * SUBLANES)

            def step(t, s, rt=rt):
                out = []
                for d in (0, 1):
                    sr, si = s[2 * d], s[2 * d + 1]
                    ar, ai = a_parts[d]
                    tt = t if d == 0 else tc - 1 - t
                    row = pl.multiple_of(tt * bp + rt * SUBLANES, SUBLANES)
                    b = bu_refs[d][pl.ds(row, SUBLANES), :]
                    xr = ar * sr - ai * si + b[:, :half]
                    xi = ar * si + ai * sr + b[:, half:]
                    bu_refs[d][pl.ds(row, SUBLANES), :] = jnp.concatenate([xr, xi], axis=-1)
                    out += [xr, xi]
                return tuple(out)

            init = tuple(st_refs[d][rows, sl] for d in (0, 1) for sl in (slice(0, half), slice(half, 2 * half)))
            fin = lax.fori_loop(0, tc, step, init, unroll=math.gcd(tc, 2))
            for d in (0, 1):
                st_refs[d][rows, :] = jnp.concatenate([fin[2 * d], fin[2 * d + 1]], axis=-1)
        for d in (0, 1):
            y_refs[d][pl.ds(r0s[d], rc), :] = _dot(bu_refs[d][...].astype(BF16), cm_ref[0])
        return carry

    lax.fori_loop(0, nc, chunk_body, 0)
    if want_final:
        for d in (0, 1):
            xf_ref[d, 0] = st_refs[d][...]
    y = y_refs[0][...] + y_refs[1][...] + d_ref[...] * u_ref[...].astype(F32)
    y_ref[...] = jax.nn.gelu(y).astype(y_ref.dtype)


def _s5_mixer(u_tm, bm, cm, a8, dvec, x0, t_len, bp, want_final):
    rows, mix = u_tm.shape
    n_slab = mix // S5_SLAB
    two_half = bm.shape[-1]
    tc = _pick(t_len, tuple(max(1, 1024 // bp) >> s for s in range(6)))
    has_init = x0 is not None
    in_specs = [
        pl.BlockSpec((rows, S5_SLAB), lambda s: (0, s)),
        pl.BlockSpec((2, 1, S5_SLAB, two_half), lambda s: (0, s, 0, 0)),
        pl.BlockSpec((1, two_half, S5_SLAB), lambda s: (s, 0, 0)),
        pl.BlockSpec((2, 1, SUBLANES, two_half), lambda s: (0, s, 0, 0)),
        pl.BlockSpec((1, S5_SLAB), lambda s: (0, s)),
    ]
    args = [u_tm, bm, cm, a8, dvec]
    if has_init:
        in_specs.append(pl.BlockSpec((2, 1, bp, two_half), lambda s: (0, s, 0, 0)))
        args.append(x0)
    out_specs = [pl.BlockSpec((rows, S5_SLAB), lambda s: (0, s))]
    out_shape = [jax.ShapeDtypeStruct((rows, mix), BF16)]
    if want_final:
        out_specs.append(pl.BlockSpec((2, 1, bp, two_half), lambda s: (0, s, 0, 0)))
        out_shape.append(jax.ShapeDtypeStruct((2, n_slab, bp, two_half), F32))
    res = pl.pallas_call(
        functools.partial(_s5_kernel, t_len=t_len, bp=bp, tc=tc, has_init=has_init, want_final=want_final),
        grid=(n_slab,),
        in_specs=in_specs,
        out_specs=out_specs,
        out_shape=out_shape,
        scratch_shapes=[pltpu.VMEM((tc * bp, two_half), F32)] * 2 + [pltpu.VMEM((rows, S5_SLAB), F32)] * 2
        + [pltpu.VMEM((bp, two_half), F32)] * 2,
        compiler_params=_cparams(("arbitrary",)),
        name="s5_ctx" if want_final else "s5_lat",
    )(*args)
    return (res[0], res[1]) if want_final else (res[0], None)


def _glu_kernel(y_ref, w_ref, o_ref):
    y = y_ref[...]
    z = _dot(y, w_ref[...])
    o_ref[...] = (y.astype(F32) * jax.nn.sigmoid(z)).astype(o_ref.dtype)


def _glu(y, w):
    n, mix = y.shape
    tm = _pick(n, (512, 256, 128))
    return pl.pallas_call(
        _glu_kernel,
        grid=(n // tm,),
        in_specs=[pl.BlockSpec((tm, mix), lambda i: (i, 0)), pl.BlockSpec((mix, mix), lambda i: (0, 0))],
        out_specs=pl.BlockSpec((tm, mix), lambda i: (i, 0)),
        out_shape=jax.ShapeDtypeStruct((n, mix), BF16),
        compiler_params=_cparams(("arbitrary",)),
        name="s5_glu",
    )(y, w)


def _head_norm_gate(o, normg_ref, gate, act):
    of = o * lax.rsqrt(jnp.mean(o * o, axis=-1, keepdims=True) + EPS)
    return of * normg_ref[...] * act(gate)


def _seq_geometry(geom, ctx):
    if ctx:
        return geom["batch"], geom["seq"], 0
    return geom["dec_batch"], geom["dec_seq"], geom["n_ctx"] // geom["dec_seq"]


def _col_spec(t_len, width, col0, blk0):
    base = col0 // width
    return pl.BlockSpec((t_len, width), lambda b, h: (blk0 + b, base + h))


def _launch_mixer(body, name, geom, ctx, width, in_specs, args, extra_out_specs, extra_out_shapes, scratch, **kw):
    nb, t_len, _ = _seq_geometry(geom, ctx)
    heads = geom["mix"] // width
    out_specs = [pl.BlockSpec((t_len, width), lambda b, h: (b, h))] + list(extra_out_specs)
    out_shapes = [jax.ShapeDtypeStruct((nb * t_len, geom["mix"]), BF16)] + list(extra_out_shapes)
    return pl.pallas_call(
        functools.partial(body, **kw),
        grid=(nb, heads),
        in_specs=list(in_specs),
        out_specs=out_specs,
        out_shape=out_shapes,
        scratch_shapes=scratch,
        compiler_params=_cparams(("arbitrary", "arbitrary")),
        name=name,
    )(*args)


def _gla_kernel(*refs, t_len, dk, has_init, want_final):
    it = iter(refs)
    q_ref, k_ref, v_ref, g_ref, zg_ref, wa_ref, ba_ref, normg_ref = (next(it) for _ in range(8))
    s0_ref = next(it) if has_init else None
    y_ref = next(it)
    sf_ref = next(it) if want_final else None
    of_ref, ob_ref, cumf_ref, cumb_ref, stf_ref, stb_ref = (next(it) for _ in range(6))

    cl = min(GLA_CHUNK, t_len)
    nc = t_len // cl
    blk = min(SEQ_CHUNK, t_len)
    scale = dk ** -0.5
    dirs = ((of_ref, cumf_ref, stf_ref), (ob_ref, cumb_ref, stb_ref))

    zg = zg_ref[...]
    ti = _iota2((blk, blk), 0)
    si = _iota2((blk, blk), 1)
    same_chunk = (ti // cl) == (si // cl)
    for d, (_, cum_ref, st_ref) in enumerate(dirs):
        la = _log_sigmoid(jnp.dot(zg, wa_ref[d, 0], precision=HIGHEST, preferred_element_type=F32)
                          + ba_ref[d, 0]) / GLA_TAU
        tri = jnp.where(same_chunk & ((si <= ti) if d == 0 else (si >= ti)), 1.0, 0.0)
        for p in range(t_len // blk):
            rows = slice(p * blk, (p + 1) * blk)
            cum_ref[rows, :] = jnp.dot(tri, la[rows], precision=HIGHEST, preferred_element_type=F32)
        st_ref[...] = s0_ref[0, d, 0].T if has_init else jnp.zeros_like(st_ref)

    tl = _iota2((cl, cl), 0)
    sl = _iota2((cl, cl), 1)
    keeps = (sl <= tl, sl >= tl)

    def chunk_body(c, carry):
        for d, (o_ref, cum_ref, st_ref) in enumerate(dirs):
            cc = c if d == 0 else nc - 1 - c
            r0 = pl.multiple_of(cc * cl, cl)
            cum = cum_ref[pl.ds(r0, cl), :]
            tot = cum[cl - 1:cl, :] if d == 0 else cum[0:1, :]
            q = q_ref[pl.ds(r0, cl), :].astype(F32) * scale
            k = k_ref[pl.ds(r0, cl), :].astype(F32)
            v = v_ref[pl.ds(r0, cl), :]
            qd = (q * jnp.exp(cum)).astype(BF16)
            kd = (k * jnp.exp(-cum)).astype(BF16)
            sc = jnp.where(keeps[d], _nt(qd, kd), 0.0)
            st = st_ref[...]
            o_ref[pl.ds(r0, cl), :] = _nt(qd, st.astype(BF16)) + _dot(sc.astype(BF16), v)
            kl = (k * jnp.exp(tot - cum)).astype(BF16)
            st_ref[...] = st * jnp.exp(tot) + _tn(v, kl)
        return carry

    lax.fori_loop(0, nc, chunk_body, 0)
    if want_final:
        sf_ref[0, 0, 0] = stf_ref[...].T
        sf_ref[0, 1, 0] = stb_ref[...].T
    y = _head_norm_gate(of_ref[...] + ob_ref[...], normg_ref, g_ref[...].astype(F32), _silu)
    y_ref[...] = y.astype(y_ref.dtype)


def _gla_mixer(z, zg, gp, s0, geom, cols, ctx):
    nb, t_len, blk0 = _seq_geometry(geom, ctx)
    mix, heads = geom["mix"], geom["gla_heads"]
    dk, dv = mix // 2 // heads, mix // heads
    has_init, want_final = s0 is not None, ctx
    in_specs = [
        _col_spec(t_len, dk, cols["gla_q"], blk0),
        _col_spec(t_len, dk, cols["gla_k"], blk0),
        _col_spec(t_len, dv, cols["gla_v"], blk0),
        _col_spec(t_len, dv, cols["gla_g"], blk0),
        pl.BlockSpec((t_len, LANES), lambda b, h: (blk0 + b, 0)),
        pl.BlockSpec((2, 1, LANES, dk), lambda b, h: (0, h, 0, 0)),
        pl.BlockSpec((2, 1, 1, dk), lambda b, h: (0, h, 0, 0)),
        pl.BlockSpec((1, dv), lambda b, h: (0, h)),
    ]
    args = [z, z, z, z, zg, gp["wa"], gp["ba"], gp["normg"]]
    if has_init:
        in_specs.append(pl.BlockSpec((1, 2, 1, dk, dv), lambda b, h: (b, 0, h, 0, 0)))
        args.append(s0)
    x_specs, x_shapes = [], []
    if want_final:
        x_specs.append(pl.BlockSpec((1, 2, 1, dk, dv), lambda b, h: (b, 0, h, 0, 0)))
        x_shapes.append(jax.ShapeDtypeStruct((nb, 2, heads, dk, dv), F32))
    res = _launch_mixer(_gla_kernel, "gla_ctx" if ctx else "gla_lat", geom, ctx, dv, in_specs, args,
                        x_specs, x_shapes,
                        [pltpu.VMEM((t_len, dv), F32)] * 2 + [pltpu.VMEM((t_len, dk), F32)] * 2
                        + [pltpu.VMEM((dv, dk), F32)] * 2,
                        t_len=t_len, dk=dk, has_init=has_init, want_final=want_final)
    return (res[0], res[1]) if want_final else (res[0], None)


def _ret_kernel(*refs, t_len, dh, has_init, want_final, rope):
    it = iter(refs)
    lg_ref = next(it)
    q_ref, k_ref, v_ref, g_ref, normg_ref = (next(it) for _ in range(5))
    cos_ref, sin_ref = (next(it), next(it)) if rope else (None, None)
    s0_ref = next(it) if has_init else None
    y_ref = next(it)
    sf_ref = next(it) if want_final else None
    o_ref = next(it)

    h = pl.program_id(1)
    n_heads = lg_ref.shape[0] // 2
    lgf = lg_ref[h]
    lgb = lg_ref[n_heads + h]
    cl = min(SEQ_CHUNK, t_len)
    nc = t_len // cl
    scale = dh ** -0.5
    half = dh // 2

    q = q_ref[...].astype(F32)
    k = k_ref[...].astype(F32) * scale
    if rope:
        cos = cos_ref[...]
        sin = sin_ref[...]

        def rot(x):
            x1, x2 = x[:, :half], x[:, half:]
            return jnp.concatenate([x1 * cos - x2 * sin, x1 * sin + x2 * cos], axis=-1)

        q, k = rot(q), rot(k)
    qb = q.astype(BF16)
    kb = k.astype(BF16)

    dt = (_iota2((cl, cl), 0) - _iota2((cl, cl), 1)).astype(F32)
    decay = (jnp.where(dt >= 0, jnp.exp(lgf * jnp.maximum(dt, 0.0)), 0.0)
             + jnp.where(dt <= 0, jnp.exp(lgb * jnp.maximum(-dt, 0.0)), 0.0))
    tcol = _iota2((cl, 1), 0).astype(F32)

    def rows(c):
        return slice(c * cl, (c + 1) * cl)

    s_f = s0_ref[0, 0, 0] if has_init else None
    for c in range(nc):
        qc, kc, vc = qb[rows(c)], kb[rows(c)], v_ref[rows(c), :]
        o = _dot((_nt(qc, kc) * decay).astype(BF16), vc)
        if s_f is not None:
            o = o + _dot((q[rows(c)] * jnp.exp(lgf * (tcol + 1.0))).astype(BF16), s_f.astype(BF16))
        o_ref[rows(c), :] = o
        if c < nc - 1 or want_final:
            upd = _tn((k[rows(c)] * jnp.exp(lgf * (cl - 1.0 - tcol))).astype(BF16), vc)
            s_f = upd if s_f is None else jnp.exp(lgf * cl) * s_f + upd
    if want_final:
        sf_ref[0, 0, 0] = s_f
    s_b = s0_ref[0, 1, 0] if has_init else None
    for c in range(nc - 1, -1, -1):
        vc = v_ref[rows(c), :]
        if s_b is not None:
            o_ref[rows(c), :] += _dot((q[rows(c)] * jnp.exp(lgb * (cl - tcol))).astype(BF16), s_b.astype(BF16))
        if c > 0 or want_final:
            upd = _tn((k[rows(c)] * jnp.exp(lgb * tcol)).astype(BF16), vc)
            s_b = upd if s_b is None else jnp.exp(lgb * cl) * s_b + upd
    if want_final:
        sf_ref[0, 1, 0] = s_b
    y = _head_norm_gate(o_ref[...], normg_ref, g_ref[...].astype(F32), _silu)
    y_ref[...] = y.astype(y_ref.dtype)


def _ret_mixer(z, lg, normg, rope_tabs, s0, geom, cols, ctx):
    nb, t_len, blk0 = _seq_geometry(geom, ctx)
    mix, heads = geom["mix"], geom["ret_heads"]
    dh = mix // heads
    has_init, want_final, rope = s0 is not None, ctx, rope_tabs is not None
    in_specs = [
        pl.BlockSpec(memory_space=pltpu.SMEM),
        _col_spec(t_len, dh, cols["ret_q"], blk0),
        _col_spec(t_len, dh, cols["ret_k"], blk0),
        _col_spec(t_len, dh, cols["ret_v"], blk0),
        _col_spec(t_len, dh, cols["ret_g"], blk0),
        pl.BlockSpec((1, dh), lambda b, h: (0, h)),
    ]
    args = [lg, z, z, z, z, normg]
    if rope:
        in_specs += [pl.BlockSpec((t_len, dh // 2), lambda b, h: (0, 0))] * 2
        args += list(rope_tabs)
    if has_init:
        in_specs.append(pl.BlockSpec((1, 2, 1, dh, dh), lambda b, h: (b, 0, h, 0, 0)))
        args.append(s0)
    x_specs, x_shapes = [], []
    if want_final:
        x_specs.append(pl.BlockSpec((1, 2, 1, dh, dh), lambda b, h: (b, 0, h, 0, 0)))
        x_shapes.append(jax.ShapeDtypeStruct((nb, 2, heads, dh, dh), F32))
    res = _launch_mixer(_ret_kernel, "ret_ctx" if ctx else "ret_lat", geom, ctx, dh, in_specs, args,
                        x_specs, x_shapes, [pltpu.VMEM((t_len, dh), F32)],
                        t_len=t_len, dh=dh, has_init=has_init, want_final=want_final, rope=rope)
    return (res[0], res[1]) if want_final else (res[0], None)


def _ml_kernel(*refs, t_len, dh, n_heads, layer, depth, gate_lane0, has_init, want_final):
    it = iter(refs)
    bias_ref = next(it)
    m0_ref = next(it) if has_init else None
    q_ref, k_ref, v_ref, og_ref, zg_ref, zgt_ref, normg_ref = (next(it) for _ in range(7))
    c0_ref, n0_ref = (next(it), next(it)) if has_init else (None, None)
    y_ref = next(it)
    cf_ref, nf_ref, mf_ref = (next(it), next(it), next(it)) if want_final else (None, None, None)
    o_ref = next(it)

    b = pl.program_id(0)
    h = pl.program_id(1)
    cl = min(SEQ_CHUNK, t_len)
    nc = t_len // cl
    scale = dh ** -0.5

    qb = q_ref[...]
    qf = qb.astype(F32)
    kf = k_ref[...].astype(F32) * scale
    kb = kf.astype(BF16)

    zg = zg_ref[...]
    zgt = zgt_ref[...]
    lane = _iota2((1, zg.shape[1]), 1)
    sub = _iota2((zgt.shape[0], 1), 0)
    ti = _iota2((cl, cl), 0)
    si = _iota2((cl, cl), 1)

    def rows(c):
        return slice(c * cl, (c + 1) * cl)

    for d in (0, 1):
        gi = d * 2 * n_heads + h
        gf = gi + n_heads
        bi = bias_ref[d * n_heads + h]
        bf = bias_ref[(2 + d) * n_heads + h]
        i_col = jnp.sum(jnp.where(lane == gate_lane0 + gi, zg, 0.0), axis=1, keepdims=True) + bi
        f_col = _log_sigmoid(jnp.sum(jnp.where(lane == gate_lane0 + gf, zg, 0.0), axis=1, keepdims=True) + bf)
        i_row = jnp.sum(jnp.where(sub == gi, zgt, 0.0), axis=0, keepdims=True) + bi
        f_row = _log_sigmoid(jnp.sum(jnp.where(sub == gf, zgt, 0.0), axis=0, keepdims=True) + bf)
        keep = (si <= ti) if d == 0 else (si >= ti)
        keep_t = (ti <= si) if d == 0 else (ti >= si)

        if has_init:
            c_st = c0_ref[0, d, 0]
            n_st = n0_ref[0, d, 0]
            m_st = jnp.full((1, 1), m0_ref[((b * depth + layer) * 2 + d) * n_heads + h], F32)
        else:
            c_st = None
            n_st = None
            m_st = jnp.zeros((1, 1), F32)

        order = range(nc) if d == 0 else range(nc - 1, -1, -1)
        for pos, c in enumerate(order):
            rc = rows(c)
            qc, kc, vc = qb[rc], kb[rc], v_ref[rc, :]
            ic = i_col[rc]
            ir, fr = i_row[:, rc], f_row[:, rc]
            f_cum_col = jnp.sum(jnp.where(keep, fr, 0.0), axis=1, keepdims=True)
            f_cum_row = jnp.sum(jnp.where(keep_t, f_col[rc], 0.0), axis=0, keepdims=True)
            dmat = jnp.where(keep, f_cum_col + (ir - f_cum_row), -jnp.inf)
            g = f_cum_col + m_st
            m_t = jnp.maximum(g, jnp.max(dmat, axis=1, keepdims=True))
            p = _nt(qc, kc) * jnp.exp(dmat - m_t)
            num = _dot(p.astype(BF16), vc)
            den = jnp.sum(p, axis=1, keepdims=True)
            if c_st is not None:
                w_state = jnp.exp(g - m_t)
                num = num + w_state * _dot(qc, c_st.astype(BF16))
                den = den + w_state * jnp.sum(qf[rc] * n_st, axis=1, keepdims=True)
            hh = num / jnp.maximum(jnp.abs(den), jnp.exp(-m_t))
            if d == 0:
                o_ref[rc, :] = hh
            else:
                o_ref[rc, :] += hh
            if pos < nc - 1 or want_final:
                f_last = jnp.sum(fr, axis=1, keepdims=True)
                src = f_last - f_cum_col + ic
                m_new = jnp.maximum(f_last + m_st, jnp.max(src, axis=0, keepdims=True))
                kw = kf[rc] * jnp.exp(src - m_new)
                upd_c = _tn(kw.astype(BF16), vc)
                upd_n = jnp.sum(kw, axis=0, keepdims=True)
                if c_st is not None:
                    w_keep = jnp.exp(f_last + m_st - m_new)
                    c_st = w_keep * c_st + upd_c
                    n_st = w_keep * n_st + upd_n
                else:
                    c_st, n_st = upd_c, upd_n
                m_st = m_new
        if want_final:
            cf_ref[0, d, 0] = c_st
            nf_ref[0, d, 0] = n_st
            mf_ref[0, d, 0] = jnp.broadcast_to(m_st, (1, mf_ref.shape[-1]))
    y = _head_norm_gate(o_ref[...], normg_ref, og_ref[...].astype(F32), jax.nn.sigmoid)
    y_ref[...] = y.astype(y_ref.dtype)


def _ml_mixer(z, zg, zg_block, zgt, bias, normg, init, layer, depth, gate_lane0, geom, cols, ctx):
    nb, t_len, blk0 = _seq_geometry(geom, ctx)
    mix, heads = geom["mix"], geom["ml_heads"]
    dh = mix // heads
    has_init, want_final = init is not None, ctx
    smem = pl.BlockSpec(memory_space=pltpu.SMEM)
    in_specs, args = [smem], [bias]
    if has_init:
        in_specs.append(smem)
        args.append(init["m"])
    in_specs += [
        _col_spec(t_len, dh, cols["ml_q"], blk0),
        _col_spec(t_len, dh, cols["ml_k"], blk0),
        _col_spec(t_len, dh, cols["ml_v"], blk0),
        _col_spec(t_len, dh, cols["ml_o"], blk0),
        pl.BlockSpec((t_len, LANES), lambda b, h: (blk0 + b, zg_block)),
        pl.BlockSpec((zgt.shape[0], t_len), lambda b, h: (0, blk0 + b)),
        pl.BlockSpec((1, dh), lambda b, h: (0, h)),
    ]
    args += [z, z, z, z, zg, zgt, normg]
    if has_init:
        in_specs += [pl.BlockSpec((1, 2, 1, dh, dh), lambda b, h: (b, 0, h, 0, 0)),
                     pl.BlockSpec((1, 2, 1, 1, dh), lambda b, h: (b, 0, h, 0, 0))]
        args += [init["c"], init["n"]]
    x_specs, x_shapes = [], []
    if want_final:
        x_specs = [pl.BlockSpec((1, 2, 1, dh, dh), lambda b, h: (b, 0, h, 0, 0)),
                   pl.BlockSpec((1, 2, 1, 1, dh), lambda b, h: (b, 0, h, 0, 0)),
                   pl.BlockSpec((1, 2, 1, 1, LANES), lambda b, h: (b, 0, h, 0, 0))]
        x_shapes = [jax.ShapeDtypeStruct((nb, 2, heads, dh, dh), F32),
                    jax.ShapeDtypeStruct((nb, 2, heads, 1, dh), F32),
                    jax.ShapeDtypeStruct((nb, 2, heads, 1, LANES), F32)]
    res = _launch_mixer(_ml_kernel, "mlstm_ctx" if ctx else "mlstm_lat", geom, ctx, dh, in_specs, args,
                        x_specs, x_shapes, [pltpu.VMEM((t_len, dh), F32)],
                        t_len=t_len, dh=dh, n_heads=heads, layer=layer, depth=depth, gate_lane0=gate_lane0,
                        has_init=has_init, want_final=want_final)
    return (res[0], res[1:]) if want_final else (res[0], None)


def _merge_kernel(*refs, nbr, ctx_tiles):
    h_ref = refs[0]
    y_refs = refs[1:1 + 2 * nbr]
    m_refs = refs[1 + 2 * nbr:1 + 3 * nbr]
    wb_ref, o_ref = refs[1 + 3 * nbr], refs[2 + 3 * nbr]
    is_ctx = pl.program_id(0) < ctx_tiles
    h = h_ref[...]
    acc = None
    for i in range(nbr):
        y = jnp.where(is_ctx, y_refs[2 * i][...], y_refs[2 * i + 1][...])
        term = jax.nn.sigmoid(_dot(h, m_refs[i][...])) * _dot(y, wb_ref[i].astype(y.dtype))
        acc = term if acc is None else acc + term
    o_ref[...] = acc.astype(o_ref.dtype)


def _merge(h, y_pairs, w_merge, w_branch, layer, geom):
    n, d = h.shape
    mix = geom["mix"]
    nbr = len(y_pairs)
    tm = _pick(math.gcd(geom["n_ctx"], n - geom["n_ctx"]), (512, 256, 128))
    tn = _pick(d, (256, 128))
    nj = d // tn
    ctx_tiles = geom["n_ctx"] // tm
    in_specs = [pl.BlockSpec((tm, d), lambda i, j: (i, 0))]
    for _ in range(nbr):
        in_specs.append(pl.BlockSpec((tm, mix), lambda i, j: (jnp.minimum(i, ctx_tiles - 1), 0)))
        in_specs.append(pl.BlockSpec((tm, mix), lambda i, j: (jnp.maximum(i - ctx_tiles, 0), 0)))
    in_specs += [pl.BlockSpec((d, tn), lambda i, j, br=br: (0, br * nj + j)) for br in range(nbr)]
    in_specs += [pl.BlockSpec((None, nbr, mix, tn), lambda i, j: (layer, 0, 0, j))]
    flat = [y for pair in y_pairs for y in pair]
    return pl.pallas_call(
        functools.partial(_merge_kernel, nbr=nbr, ctx_tiles=ctx_tiles),
        grid=(n // tm, nj),
        in_specs=in_specs,
        out_specs=pl.BlockSpec((tm, tn), lambda i, j: (i, j)),
        out_shape=jax.ShapeDtypeStruct((n, d), BF16),
        compiler_params=_cparams(("arbitrary", "arbitrary")),
        name="merge",
    )(h, *flat, *([w_merge] * nbr), w_branch)


def _outproj_kernel(*refs, n_x, ctx_tiles):
    m_ref, w_ref = refs[:2]
    g_ref, o_ref = refs[2 + n_x:]
    x = _read_rows(refs[2:2 + n_x], ctx_tiles)
    m = m_ref[...]
    o_ref[...] = x + g_ref[0, 0] * _dot(m, w_ref[...].astype(m.dtype))


def _outproj(merged, w_out, layer, x, mod4, comp_gate, geom):
    n, d = merged.shape
    tm = _pick(math.gcd(geom["n_ctx"], geom["dec_seq"]), (1024, 512, 256, 128))
    tn = _pick(d, (512, 256, 128))
    n_ctx, dec_seq, ctx_row = geom["n_ctx"], geom["dec_seq"], geom["ctx_row"]
    ctx_tiles = n_ctx // tm

    def gate_map(i, j):
        start = i * tm
        return (jnp.where(start < n_ctx, ctx_row, (start - n_ctx) // dec_seq), comp_gate, 0, j)

    x_specs, x_args = _row_specs(x, (tm, tn), ctx_tiles, col=lambda i, j: j)
    return pl.pallas_call(
        functools.partial(_outproj_kernel, n_x=len(x_args), ctx_tiles=ctx_tiles),
        grid=(n // tm, d // tn),
        in_specs=[
            pl.BlockSpec((tm, d), lambda i, j: (i, 0)),
            pl.BlockSpec((None, d, tn), lambda i, j: (layer, 0, j)),
        ] + x_specs + [pl.BlockSpec((1, 1, 1, tn), gate_map)],
        out_specs=pl.BlockSpec((tm, tn), lambda i, j: (i, j)),
        out_shape=jax.ShapeDtypeStruct((n, d), F32),
        compiler_params=_cparams(("arbitrary", "arbitrary")),
        name="out_proj",
    )(merged, w_out, *x_args, mod4)


def _first_argmax(vals):
    best, idx = vals[0], jnp.zeros_like(vals[0])
    for j in range(1, len(vals)):
        better = vals[j] > best
        idx = jnp.where(better, float(j), idx)
        best = jnp.where(better, vals[j], best)
    return idx, best


def _pick_row(vals, idx):
    out = vals[0]
    for j in range(1, len(vals)):
        out = jnp.where(idx == float(j), vals[j], out)
    return out


def _router_kernel(x_ref, g_ref, sh_ref, sc_ref, wrt_ref, rb_ref, h_ref, eidx_ref, wgt_ref, rank_ref, cnt_ref,
                   carry_ref, *, n_exp, n_groups):
    i = pl.program_id(0)
    x = x_ref[...]
    y = x * lax.rsqrt(jnp.mean(x * x, axis=-1, keepdims=True) + EPS)
    h = (y * g_ref[...]) * (1.0 + sc_ref[0, 0]) + sh_ref[0, 0]
    h_ref[...] = _pack_bf16_pairs(h)
    tm = x.shape[0]
    per = n_exp // n_groups

    logits = lax.dot_general(wrt_ref[...], h, (((1,), (1,)), ((), ())), precision=HIGHEST,
                             preferred_element_type=F32)
    scores = jax.nn.sigmoid(logits)
    sel = scores + rb_ref[...]
    sel_rows = [sel[e:e + 1, :] for e in range(n_exp)]
    sc_rows = [scores[e:e + 1, :] for e in range(n_exp)]
    group_scores = []
    for g in range(n_groups):
        r = sel_rows[g * per:(g + 1) * per]
        pair = None
        for a in range(per):
            for bb in range(a + 1, per):
                s2 = r[a] + r[bb]
                pair = s2 if pair is None else jnp.maximum(pair, s2)
        group_scores.append(pair)
    g_best, _ = _first_argmax(group_scores)
    in_sel = [_pick_row([sel_rows[g * per + j] for g in range(n_groups)], g_best) for j in range(per)]
    in_sc = [_pick_row([sc_rows[g * per + j] for g in range(n_groups)], g_best) for j in range(per)]
    j1, _ = _first_argmax(in_sel)
    masked = [jnp.where(j1 == float(j), -jnp.inf, in_sel[j]) for j in range(per)]
    j2, _ = _first_argmax(masked)
    w1 = _pick_row(in_sc, j1)
    w2 = _pick_row(in_sc, j2)
    tot = w1 + w2
    e1 = (g_best * per + j1).astype(jnp.int32)
    e2 = (g_best * per + j2).astype(jnp.int32)
    eidx_ref[...] = jnp.concatenate([e1, e2], axis=0)
    wgt_ref[...] = jnp.concatenate([w1 / tot, w2 / tot], axis=0)

    @pl.when(i == 0)
    def _():
        carry_ref[...] = jnp.zeros_like(carry_ref)

    eio = _iota2((n_exp, tm), 0)
    hit1 = eio == e1
    hit2 = eio == e2
    onehot = jnp.where(hit1 | hit2, 1.0, 0.0)
    before = (_iota2((tm, tm), 0) < _iota2((tm, tm), 1)).astype(BF16)
    prefix = _dot(onehot.astype(BF16), before) + carry_ref[:, 0:1]
    r1 = jnp.sum(jnp.where(hit1, prefix, 0.0), axis=0, keepdims=True)
    r2 = jnp.sum(jnp.where(hit2, prefix, 0.0), axis=0, keepdims=True)
    rank_ref[...] = jnp.concatenate([r1, r2], axis=0).astype(jnp.int32)
    carry_ref[...] = carry_ref[...] + jnp.sum(onehot, axis=1, keepdims=True)
    cnt_ref[...] = carry_ref[...]


def _norm_router(x, g, mod4, comp_shift, comp_scale, w_router_t, router_bias, geom):
    n, d = x.shape
    n_exp = w_router_t.shape[0]
    tm = 256
    row2 = pl.BlockSpec((2, tm), lambda i: (0, i))
    return pl.pallas_call(
        functools.partial(_router_kernel, n_exp=n_exp, n_groups=N_EXPERT_GROUPS),
        grid=(n // tm,),
        in_specs=[
            pl.BlockSpec((tm, d), lambda i: (i, 0)),
            pl.BlockSpec((1, d), lambda i: (0, 0)),
            _mod_spec(d, tm, geom, comp_shift),
            _mod_spec(d, tm, geom, comp_scale),
            pl.BlockSpec((n_exp, d), lambda i: (0, 0)),
            pl.BlockSpec((n_exp, 1), lambda i: (0, 0)),
        ],
        out_specs=[pl.BlockSpec((tm, d // 2), lambda i: (i, 0)), row2, row2, row2,
                   pl.BlockSpec((n_exp, LANES), lambda i: (0, 0))],
        out_shape=[jax.ShapeDtypeStruct((n, d // 2), jnp.uint32),
                   jax.ShapeDtypeStruct((2, n), jnp.int32),
                   jax.ShapeDtypeStruct((2, n), F32),
                   jax.ShapeDtypeStruct((2, n), jnp.int32),
                   jax.ShapeDtypeStruct((n_exp, LANES), F32)],
        scratch_shapes=[pltpu.VMEM((n_exp, LANES), F32)],
        compiler_params=_cparams(("arbitrary",)),
        name="norm_router",
    )(x, g.reshape(1, d), mod4, mod4, w_router_t, router_bias.reshape(n_exp, 1))


def _row_copy(src_hbm, row, dst_vmem, r, sem):
    return pltpu.make_async_copy(src_hbm.at[pl.ds(row, 1), :], dst_vmem.at[pl.ds(r, 1), :], sem)


def _dispatch_kernel(src_ref, h_hbm, o_ref, sem):
    tg = o_ref.shape[0]

    def start(i, c):
        for u in range(2):
            r = 2 * i + u
            _row_copy(h_hbm, src_ref[0, 0, r], o_ref, r, sem).start(priority=u)
        return c

    def wait(r, c):
        _row_copy(h_hbm, 0, o_ref, r, sem).wait()
        return c

    lax.fori_loop(0, tg // 2, start, 0, unroll=8)
    lax.fori_loop(0, tg, wait, 0, unroll=16)


def _dispatch(h, src):
    p_rows = src.shape[0]
    dw = h.shape[1]
    tg = _pick(p_rows, (8 * ROUTE_TILE, 4 * ROUTE_TILE, 2 * ROUTE_TILE, ROUTE_TILE))
    return pl.pallas_call(
        _dispatch_kernel,
        grid=(p_rows // tg,),
        in_specs=[
            pl.BlockSpec((1, 1, tg), lambda i: (i, 0, 0), memory_space=pltpu.SMEM),
            pl.BlockSpec(memory_space=pl.ANY),
        ],
        out_specs=pl.BlockSpec((tg, dw), lambda i: (i, 0)),
        out_shape=jax.ShapeDtypeStruct((p_rows, dw), h.dtype),
        scratch_shapes=[pltpu.SemaphoreType.DMA(())],
        compiler_params=_cparams(("arbitrary",)),
        name="moe_dispatch",
    )(src.reshape(p_rows // tg, 1, tg), h)


def _new_expert(te_ref, t):
    return (t == 0) | (te_ref[t] != te_ref[jnp.maximum(t - 1, 0)])


def _ffn_up_kernel(te_ref, tv_ref, x_ref, wg_ref, wu_ref, o_ref, wgb_ref, wub_ref):
    t = pl.program_id(1)

    @pl.when(_new_expert(te_ref, t))
    def _():
        wgb_ref[...] = wg_ref[0].astype(BF16)
        wub_ref[...] = wu_ref[0].astype(BF16)

    @pl.when(tv_ref[t] == 1)
    def _():
        lo, hi = _unpack_bf16_pairs(x_ref[...])
        lo, hi = lo.astype(BF16), hi.astype(BF16)
        half = lo.shape[1]
        g = _dot(lo, wgb_ref[:half, :]) + _dot(hi, wgb_ref[half:, :])
        u = _dot(lo, wub_ref[:half, :]) + _dot(hi, wub_ref[half:, :])
        o_ref[...] = (_silu(g) * u).astype(o_ref.dtype)

    @pl.when(tv_ref[t] == 0)
    def _():
        o_ref[...] = jnp.zeros_like(o_ref)


def _ffn_up(xs, w_gate, w_up, layer, tile_expert, tile_valid):
    p_rows, dw = xs.shape
    d, f = w_gate.shape[2], w_gate.shape[3]
    tm = ROUTE_TILE
    tf = _pick(f, (512, 256, 128))
    w_spec = pl.BlockSpec((None, 1, d, tf), lambda j, t, te, tv: (layer, te[t], 0, j))
    grid_spec = pltpu.PrefetchScalarGridSpec(
        num_scalar_prefetch=2,
        grid=(f // tf, p_rows // tm),
        in_specs=[pl.BlockSpec((tm, dw), lambda j, t, te, tv: (t, 0)), w_spec, w_spec],
        out_specs=pl.BlockSpec((tm, tf), lambda j, t, te, tv: (t, j)),
        scratch_shapes=[pltpu.VMEM((d, tf), BF16), pltpu.VMEM((d, tf), BF16)],
    )
    return pl.pallas_call(
        _ffn_up_kernel,
        grid_spec=grid_spec,
        out_shape=jax.ShapeDtypeStruct((p_rows, f), BF16),
        compiler_params=_cparams(("arbitrary", "arbitrary")),
        name="moe_ffn_up",
    )(tile_expert, tile_valid, xs, w_gate, w_up)


def _ffn_down_kernel(te_ref, tv_ref, h_ref, wd_ref, o_ref, wdb_ref):
    t = pl.program_id(1)

    @pl.when(_new_expert(te_ref, t))
    def _():
        wdb_ref[...] = wd_ref[0].astype(BF16)

    @pl.when(tv_ref[t] == 1)
    def _():
        o_ref[...] = _pack_bf16_pairs(_dot(h_ref[...], wdb_ref[...]))

    @pl.when(tv_ref[t] == 0)
    def _():
        o_ref[...] = jnp.zeros_like(o_ref)


def _ffn_down_tile(d):
    return _pick(d, (2048, 1024, 512, 256))


def _ffn_down(hmid, w_down, layer, tile_expert, tile_valid):
    p_rows, f = hmid.shape
    d = w_down.shape[3]
    tm = ROUTE_TILE
    tn = _ffn_down_tile(d)
    grid_spec = pltpu.PrefetchScalarGridSpec(
        num_scalar_prefetch=2,
        grid=(d // tn, p_rows // tm),
        in_specs=[
            pl.BlockSpec((tm, f), lambda j, t, te, tv: (t, 0)),
            pl.BlockSpec((None, 1, f, tn), lambda j, t, te, tv: (layer, te[t], 0, j)),
        ],
        out_specs=pl.BlockSpec((tm, tn // 2), lambda j, t, te, tv: (t, j)),
        scratch_shapes=[pltpu.VMEM((f, tn), BF16)],
    )
    return pl.pallas_call(
        _ffn_down_kernel,
        grid_spec=grid_spec,
        out_shape=jax.ShapeDtypeStruct((p_rows, d // 2), jnp.uint32),
        compiler_params=_cparams(("arbitrary", "arbitrary")),
        name="moe_ffn_down",
    )(tile_expert, tile_valid, hmid, w_down)


def _combine_kernel(*refs, final_norm, ctx_tiles, tn):
    pos_ref, x_ref, wt_ref, g_ref, fg_ref, ys_hbm = refs[:6]
    out_refs = refs[6:-2]
    buf_ref, sem = refs[-2:]
    tm = x_ref.shape[0]

    def start(r, c):
        for s in range(2):
            _row_copy(ys_hbm, pos_ref[0, s, r], buf_ref.at[s], r, sem).start(priority=s)
        return c

    def wait(r, c):
        for s in range(2):
            _row_copy(ys_hbm, 0, buf_ref.at[s], r, sem).wait()
        return c

    lax.fori_loop(0, tm, start, 0, unroll=4)
    lax.fori_loop(0, tm, wait, 0, unroll=4)

    d = x_ref.shape[1]
    rg = 2 * SUBLANES
    cw = min(2 * LANES, tn // 2)
    gate = g_ref[0, 0]
    lane = _iota2((1, wt_ref.shape[1]), 1)
    pieces = [(j * tn // 2 + p, j * tn + p, j * tn + tn // 2 + p)
              for j in range(d // tn) for p in range(0, tn // 2, cw)]

    def fill_normed(o_ref):
        w = wt_ref[...]
        w0 = jnp.sum(jnp.where(lane == 0, w, 0.0), axis=1, keepdims=True)
        w1 = jnp.sum(jnp.where(lane == 1, w, 0.0), axis=1, keepdims=True)
        parts = []
        for j in range(d // tn):
            seg = slice(j * tn // 2, (j + 1) * tn // 2)
            lo0, hi0 = _unpack_bf16_pairs(buf_ref[0, :, seg])
            lo1, hi1 = _unpack_bf16_pairs(buf_ref[1, :, seg])
            parts += [w0 * lo0 + w1 * lo1, w0 * hi0 + w1 * hi1]
        x = x_ref[...] + gate * jnp.concatenate(parts, axis=1)
        o_ref[...] = (x * lax.rsqrt(jnp.mean(x * x, axis=-1, keepdims=True) + EPS)) * fg_ref[...]

    def fill(o_ref):
        if final_norm:
            return fill_normed(o_ref)

        def rows_body(gi, c):
            rows = pl.ds(pl.multiple_of(gi * rg, rg), rg)
            w = wt_ref[rows, :]
            w0 = jnp.sum(jnp.where(lane == 0, w, 0.0), axis=1, keepdims=True)
            w1 = jnp.sum(jnp.where(lane == 1, w, 0.0), axis=1, keepdims=True)
            for p0, c_lo, c_hi in pieces:
                lo0, hi0 = _unpack_bf16_pairs(buf_ref[0, rows, p0:p0 + cw])
                lo1, hi1 = _unpack_bf16_pairs(buf_ref[1, rows, p0:p0 + cw])
                for col, moe in ((c_lo, w0 * lo0 + w1 * lo1), (c_hi, w0 * hi0 + w1 * hi1)):
                    o_ref[rows, col:col + cw] = x_ref[rows, col:col + cw] + gate[:, col:col + cw] * moe
            return c

        lax.fori_loop(0, tm // rg, rows_body, 0)

    if len(out_refs) == 1:
        fill(out_refs[0])
    else:
        is_ctx = pl.program_id(0) < ctx_tiles
        pl.when(is_ctx)(lambda: fill(out_refs[0]))
        pl.when(jnp.logical_not(is_ctx))(lambda: fill(out_refs[1]))


def _combine(x, ys, pos, wgt_cols, mod4, comp_gate, final_g, geom, final_norm, split_out):
    n, d = x.shape
    tm = pos.shape[2]
    ctx_tiles = geom["n_ctx"] // tm
    if split_out:
        out_specs = [pl.BlockSpec((tm, d), lambda i: (jnp.minimum(i, ctx_tiles - 1), 0)),
                     pl.BlockSpec((tm, d), lambda i: (jnp.maximum(i - ctx_tiles, 0), 0))]
        out_shape = [jax.ShapeDtypeStruct((geom["n_ctx"], d), F32),
                     jax.ShapeDtypeStruct((n - geom["n_ctx"], d), F32)]
    else:
        out_specs = pl.BlockSpec((tm, d), lambda i: (i, 0))
        out_shape = jax.ShapeDtypeStruct((n, d), F32)
    return pl.pallas_call(
        functools.partial(_combine_kernel, final_norm=final_norm, ctx_tiles=ctx_tiles, tn=_ffn_down_tile(d)),
        grid=(n // tm,),
        in_specs=[
            pl.BlockSpec((1, 2, tm), lambda i: (i, 0, 0), memory_space=pltpu.SMEM),
            pl.BlockSpec((tm, d), lambda i: (i, 0)),
            pl.BlockSpec((tm, wgt_cols.shape[1]), lambda i: (i, 0)),
            _mod_spec(d, tm, geom, comp_gate),
            pl.BlockSpec((1, d), lambda i: (0, 0)),
            pl.BlockSpec(memory_space=pl.ANY),
        ],
        out_specs=out_specs,
        out_shape=out_shape,
        scratch_shapes=[pltpu.VMEM((2, tm, d // 2), jnp.uint32), pltpu.SemaphoreType.DMA(())],
        compiler_params=_cparams(("arbitrary",)),
        name="moe_combine",
    )(pos, x, wgt_cols, mod4, final_g.reshape(1, d), ys)


def _route_plan(eidx, rank, counts, n_exp, combine_tile):
    n = eidx.shape[1]
    tile = ROUTE_TILE
    n_tiles = (2 * n) // tile + n_exp
    cnt = counts.astype(jnp.int32)
    padded = ((cnt + tile - 1) // tile) * tile
    e_ids = jnp.arange(n_exp, dtype=jnp.int32)
    ends = jnp.sum(jnp.where(e_ids[None, :] <= e_ids[:, None], padded[None, :], 0), axis=1)
    offs = ends - padded
    pos = jnp.sum(jnp.where(eidx[:, :, None] == e_ids, offs, 0), axis=-1) + rank
    tile_start = jnp.arange(n_tiles, dtype=jnp.int32) * tile
    tile_expert = jnp.minimum(jnp.sum((ends[None, :] <= tile_start[:, None]).astype(jnp.int32), axis=1), n_exp - 1)
    tile_valid = (tile_start < ends[-1]).astype(jnp.int32)
    tok = jnp.broadcast_to(jnp.arange(n, dtype=jnp.int32), (2, n))
    src = jnp.zeros((n_tiles * tile,), jnp.int32).at[pos.reshape(-1)].set(tok.reshape(-1))
    pos_tiles = pos.reshape(2, n // combine_tile, combine_tile).transpose(1, 0, 2)
    return pos_tiles, src, tile_expert, tile_valid


def _s5_params(lam_re, lam_im, log_step, b_re, b_im, c_re, c_im):
    n_dir, groups, p = lam_re.shape
    gch = b_re.shape[-1]
    gps = S5_SLAB // gch
    n_slab = groups // gps
    step = jnp.exp(log_step)[..., None]
    mag = jnp.exp(lam_re * step)
    a_re = mag * jnp.cos(lam_im * step)
    a_im = mag * jnp.sin(lam_im * step)
    den = lam_re * lam_re + lam_im * lam_im
    z_re = ((a_re - 1.0) * lam_re + a_im * lam_im) / den
    z_im = (a_im * lam_re - (a_re - 1.0) * lam_im) / den
    bb_re = z_re[..., None] * b_re - z_im[..., None] * b_im
    bb_im = z_re[..., None] * b_im + z_im[..., None] * b_re
    eye = jnp.eye(gps, dtype=F32)
    bbs = jnp.stack([bb_re, bb_im], axis=2).reshape(n_dir, n_slab, gps, 2, p, gch)
    bm = jnp.einsum("dsgrpc,gh->dsgcrhp", bbs, eye).reshape(n_dir, n_slab, S5_SLAB, 2 * gps * p)
    cs = jnp.stack([c_re, -c_im], axis=0).reshape(2, n_slab, gps, gch, p)
    cm = jnp.einsum("rsgcp,gh->srgphc", cs, eye).reshape(n_slab, 2 * gps * p, S5_SLAB)
    a = jnp.stack([a_re, a_im], axis=2).reshape(n_dir, n_slab, gps, 2, p).transpose(0, 1, 3, 2, 4)
    a = a.reshape(n_dir, n_slab, 1, 2 * gps * p)
    a8 = jnp.broadcast_to(a, (n_dir, n_slab, SUBLANES, 2 * gps * p))
    return bm.astype(BF16), cm.astype(BF16), a8


def _axial_rope(t, dh):
    rows = t // GRID_W
    row = jnp.repeat(jnp.arange(rows, dtype=F32), GRID_W)
    col = (jnp.arange(rows * GRID_W) % GRID_W).astype(F32)
    n_freq = dh // 4
    inv = ROPE_BASE ** (-jnp.arange(n_freq, dtype=F32) / n_freq)
    ang = jnp.concatenate([row[:, None] * inv, col[:, None] * inv], axis=-1)
    return jnp.cos(ang), jnp.sin(ang)


def _to_time_major(u, nb, t_len, bp):
    u = u.reshape(nb, t_len, -1).transpose(1, 0, 2)
    if bp != nb:
        u = jnp.pad(u, ((0, 0), (0, bp - nb), (0, 0)))
    return u.reshape(t_len * bp, -1)


def _from_time_major(y, nb, t_len, bp):
    return y.reshape(t_len, bp, -1)[:, :nb].transpose(1, 0, 2).reshape(nb * t_len, -1)


def _s5_state_to_slabs(s_re, s_im, n_slab, bp):
    nb = s_re.shape[0]
    st = jnp.concatenate([s_re.reshape(nb, 2, n_slab, -1), s_im.reshape(nb, 2, n_slab, -1)], axis=-1)
    st = st.transpose(1, 2, 0, 3)
    return jnp.pad(st, ((0, 0), (0, 0), (0, bp - nb), (0, 0)))


def _s5_slabs_to_state(xf, nb, groups, p):
    half = xf.shape[-1] // 2
    re = xf[:, :, :nb, :half].transpose(2, 0, 1, 3).reshape(nb, 2, groups, p)
    im = xf[:, :, :nb, half:].transpose(2, 0, 1, 3).reshape(nb, 2, groups, p)
    return re, im


def _round_up(x, m):
    return (x + m - 1) // m * m


def kernel(x_prompt, x_sample, c, c_ctx, state_s5_re, state_s5_im, state_gla, state_ml_c, state_ml_n, state_ml_m, state_ret, w_ada, b_ada, norm1_g, norm2_g, w_in, s5_lambda_re, s5_lambda_im, s5_log_step, s5_b_re, s5_b_im, s5_c_re, s5_c_im, s5_d, s5_w_glu, gla_w_a, gla_b_a, gla_norm_g, ml_i_bias, ml_f_bias, ml_norm_g, ret_decay_logit, ret_norm_g, w_branch, w_out, w_router, router_bias, w_exp_gate, w_exp_up, w_exp_down, final_g):
    batch, seq, d = x_prompt.shape
    dec_batch, dec_seq, _ = x_sample.shape
    depth = w_in.shape[0]
    mix = s5_d.shape[-1]
    n_exp = w_router.shape[1]
    gla_heads = state_gla.shape[3]
    ml_heads = ml_i_bias.shape[-1]
    ret_heads = ret_decay_logit.shape[-1]
    gla_rank = gla_w_a.shape[2]
    groups, p_state = s5_lambda_re.shape[2], s5_lambda_re.shape[3]
    n_ctx, n_lat = batch * seq, dec_batch * dec_seq
    n = n_ctx + n_lat
    assert dec_batch + 1 <= SUBLANES and n_ctx % dec_seq == 0 and dec_seq % seq == 0
    geom = dict(batch=batch, seq=seq, dec_batch=dec_batch, dec_seq=dec_seq, n_ctx=n_ctx, n=n, ctx_row=dec_batch,
                mix=mix, gla_heads=gla_heads, ml_heads=ml_heads, ret_heads=ret_heads)
    n_slab = mix // S5_SLAB
    bp_ctx, bp_lat = _round_up(batch, SUBLANES), _round_up(dec_batch, SUBLANES)

    half = mix // 2
    widths = [("s5_u", mix), ("gla_q", half), ("gla_k", half), ("gla_v", mix), ("gla_g", mix), ("gla_r", gla_rank),
              ("ml_q", mix), ("ml_k", mix), ("ml_v", mix), ("ml_o", mix), ("ml_if", 4 * ml_heads),
              ("ret_q", mix), ("ret_k", mix), ("ret_v", mix), ("ret_g", mix), ("merge", 4 * d)]
    src_off, o = {}, 0
    for name, w in widths:
        src_off[name] = (o, w)
        o += w
    main_names = [nm for nm, _ in widths if nm not in ("gla_r", "ml_if", "merge")]
    cols, o = {}, 0
    for nm in main_names:
        cols[nm] = o
        o += src_off[nm][1]

    repack_tn = _pick(math.gcd(half, 4 * d), (512, 256, 128))
    main_starts = [src_off[nm][0] + off for nm in main_names for off in range(0, src_off[nm][1], repack_tn)]
    merge_starts = [src_off["merge"][0] + off for off in range(0, 4 * d, repack_tn)]
    gate_starts = [src_off["gla_r"][0], src_off["ml_if"][0]]
    w_in_t = jnp.swapaxes(w_in, 1, 2)

    x = (x_prompt.reshape(n_ctx, d), x_sample.reshape(n_lat, d))
    c8 = jnp.zeros((SUBLANES, d), F32).at[:dec_batch].set(c).at[dec_batch].set(c_ctx)
    mod = _ada(c8, w_ada, b_ada)
    rope_tabs = _axial_rope(dec_seq, mix // ret_heads)
    w_router_t = w_router.T

    ctx_states = []
    for l in range(depth):
        mod4 = mod[l].reshape(SUBLANES, N_MOD, 1, d)
        w_main = _repack_proj(w_in_t, l, main_starts, repack_tn)
        w_merge = _repack_proj(w_in_t, l, merge_starts, repack_tn)
        w_gate = _repack_proj(w_in_t, l, gate_starts, LANES)

        h = _norm_mod(x, norm1_g[l], mod4, 0, 1, geom, BF16)
        z = _matmul(h, w_main, BF16, _pick(n, (1024, 512, 256)), _pick(w_main.shape[1], (512, 256, 128)), "in_proj")
        zg = _matmul(h, w_gate, F32, _pick(n, (1024, 512, 256)), 2 * LANES, "gate_proj")
        zgt = zg[:, LANES:LANES + 4 * ml_heads].T

        bm, cm, a8 = _s5_params(s5_lambda_re[l], s5_lambda_im[l], s5_log_step[l], s5_b_re[l], s5_b_im[l],
                                s5_c_re[l], s5_c_im[l])
        dvec = s5_d[l].reshape(1, mix)
        u = z[:, :mix]
        y_c, xf = _s5_mixer(_to_time_major(u[:n_ctx], batch, seq, bp_ctx), bm, cm, a8, dvec, None, seq, bp_ctx, True)
        x0 = _s5_state_to_slabs(state_s5_re[:, l], state_s5_im[:, l], n_slab, bp_lat)
        y_l, _ = _s5_mixer(_to_time_major(u[n_ctx:], dec_batch, dec_seq, bp_lat), bm, cm, a8, dvec, x0, dec_seq,
                           bp_lat, False)
        w_glu = s5_w_glu[l].astype(BF16)
        y_s5 = (_glu(_from_time_major(y_c, batch, seq, bp_ctx), w_glu),
                _glu(_from_time_major(y_l, dec_batch, dec_seq, bp_lat), w_glu))
        s5_re_l, s5_im_l = _s5_slabs_to_state(xf, batch, groups, p_state)

        dk = half // gla_heads
        wa = gla_w_a[l].reshape(2, gla_rank, gla_heads, dk).transpose(0, 2, 1, 3)
        wa = jnp.pad(wa, ((0, 0), (0, 0), (0, LANES - gla_rank), (0, 0)))
        ba = gla_b_a[l].reshape(2, gla_heads, 1, dk)
        gp = dict(wa=wa, ba=ba, normg=gla_norm_g[l].reshape(1, mix))
        y_gla_c, gla_l = _gla_mixer(z, zg, gp, None, geom, cols, True)
        y_gla_l, _ = _gla_mixer(z, zg, gp, state_gla[:, l], geom, cols, False)

        ml_bias = jnp.stack([ml_i_bias[l], ml_f_bias[l]], axis=0).reshape(-1)
        ml_ng = ml_norm_g[l].reshape(1, mix)
        y_ml_c, ml_l = _ml_mixer(z, zg, 1, zgt, ml_bias, ml_ng, None, l, depth, 0, geom, cols, True)
        ml_init = dict(c=state_ml_c[:, l], n=state_ml_n[:, l][:, :, :, None, :], m=state_ml_m.reshape(-1))
        y_ml_l, _ = _ml_mixer(z, zg, 1, zgt, ml_bias, ml_ng, ml_init, l, depth, 0, geom, cols, False)

        lg = jax.nn.log_sigmoid(ret_decay_logit[l]).reshape(-1)
        ret_ng = ret_norm_g[l].reshape(1, mix)
        y_ret_c, ret_l = _ret_mixer(z, lg, ret_ng, None, None, geom, cols, True)
        y_ret_l, _ = _ret_mixer(z, lg, ret_ng, rope_tabs, state_ret[:, l], geom, cols, False)

        y_pairs = [y_s5, (y_gla_c, y_gla_l), (y_ml_c, y_ml_l), (y_ret_c, y_ret_l)]
        merged = _merge(h, y_pairs, w_merge, w_branch, l, geom)
        x = _outproj(merged, w_out, l, x, mod4, 2, geom)
        ctx_states.append((s5_re_l, s5_im_l, gla_l, ml_l[0], ml_l[1][:, :, :, 0, :], ml_l[2][:, :, :, 0, 0], ret_l))

        h2, eidx, wgt, rank, counts = _norm_router(x, norm2_g[l], mod4, 3, 4, w_router_t, router_bias, geom)
        last = l == depth - 1
        combine_tile = ROUTE_TILE if last else _pick(math.gcd(n_ctx, n_lat), (2 * ROUTE_TILE, ROUTE_TILE))
        pos_tiles, src, tile_expert, tile_valid = _route_plan(eidx, rank, counts[:, 0], n_exp, combine_tile)
        xs = _dispatch(h2, src)
        hmid = _ffn_up(xs, w_exp_gate, w_exp_up, l, tile_expert, tile_valid)
        ys = _ffn_down(hmid, w_exp_down, l, tile_expert, tile_valid)
        x = _combine(x, ys, pos_tiles, wgt.T, mod4, 5, final_g, geom, final_norm=last, split_out=last)

    y_prompt = x[0].reshape(batch, seq, d)
    y_sample = x[1].reshape(dec_batch, dec_seq, d)
    stacked = [jnp.stack([st[i] for st in ctx_states], axis=1) for i in range(7)]
    return (y_prompt, y_sample, *stacked)
```

```python
import functools
import math

import jax
import jax.numpy as jnp
from jax import lax
from jax.experimental import pallas as pl
from jax.experimental.pallas import tpu as pltpu

F32 = jnp.float32
BF16 = jnp.bfloat16

EPS = 1e-6
GRID_W = 64
ROPE_BASE = 10000.0
GLA_TAU = 16.0
N_EXPERT_GROUPS = 4
N_MOD = 6

LANES = 128
SUBLANES = 8
S5_SLAB = LANES
GLA_CHUNK = 64
SEQ_CHUNK = 256
VMEM_LIMIT_BYTES = 56 * 1024 * 1024
ROUTE_TILE = 256


def _cparams(sem):
    return pltpu.CompilerParams(dimension_semantics=sem, vmem_limit_bytes=VMEM_LIMIT_BYTES)


def _pick(n, cands):
    for c in cands:
        if c <= n and n % c == 0:
            return c
    return n


def _nt(a, b):
    return lax.dot_general(a, b, (((1,), (1,)), ((), ())), preferred_element_type=F32)


def _tn(a, b):
    return lax.dot_general(a, b, (((0,), (0,)), ((), ())), preferred_element_type=F32)


def _dot(a, b):
    return jnp.dot(a, b, preferred_element_type=F32)


def _split_bf16(x):
    hi = x.astype(BF16)
    return hi, (x - hi.astype(F32)).astype(BF16)


def _dot3(a, b, mm):
    ah, al = _split_bf16(a)
    bh, bl = _split_bf16(b)
    return mm(ah, bh) + (mm(ah, bl) + mm(al, bh))


def _log_sigmoid(x):
    return jnp.minimum(x, 0.0) - jnp.log1p(jnp.exp(-jnp.abs(x)))


def _silu(x):
    return x * jax.nn.sigmoid(x)


def _iota2(shape, dim):
    return lax.broadcasted_iota(jnp.int32, shape, dim)


HI16 = 0xFFFF0000


def _pack_bf16_pairs(x):
    c = x.shape[1] // 2
    bits = lax.bitcast_convert_type(x.astype(BF16).astype(F32), jnp.uint32)
    return (bits[:, :c] >> 16) | (bits[:, c:] & jnp.uint32(HI16))


def _unpack_bf16_pairs(w):
    lo = lax.bitcast_convert_type(w << 16, F32)
    hi = lax.bitcast_convert_type(w & jnp.uint32(HI16), F32)
    return lo, hi


def _ada_kernel(c_ref, w_ref, b_ref, o_ref):
    s = _silu(c_ref[...]).astype(BF16)
    o_ref[0] = _dot(s, w_ref[0].astype(BF16)) + b_ref[0]


def _ada(c8, w_ada, b_ada):
    depth, d, n = w_ada.shape
    tn = _pick(n, (512, 256, 128))
    return pl.pallas_call(
        _ada_kernel,
        grid=(depth, n // tn),
        in_specs=[
            pl.BlockSpec((SUBLANES, d), lambda l, j: (0, 0)),
            pl.BlockSpec((1, d, tn), lambda l, j: (l, 0, j)),
            pl.BlockSpec((1, 1, tn), lambda l, j: (l, 0, j)),
        ],
        out_specs=pl.BlockSpec((1, SUBLANES, tn), lambda l, j: (l, 0, j)),
        out_shape=jax.ShapeDtypeStruct((depth, SUBLANES, n), F32),
        compiler_params=_cparams(("arbitrary", "arbitrary")),
        name="ada_mod",
    )(c8, w_ada, b_ada.reshape(depth, 1, n))


def _mod_spec(d, tm, geom, comp):
    n_ctx, dec_seq, ctx_row = geom["n_ctx"], geom["dec_seq"], geom["ctx_row"]

    def index_map(i, *_):
        start = i * tm
        row = jnp.where(start < n_ctx, ctx_row, (start - n_ctx) // dec_seq)
        return (row, comp, 0, 0)

    return pl.BlockSpec((1, 1, 1, d), index_map)


def _row_specs(x, block, ctx_tiles, col=lambda *g: 0):
    if not isinstance(x, tuple):
        return [pl.BlockSpec(block, lambda i, *g: (i, col(i, *g)))], [x]
    return ([pl.BlockSpec(block, lambda i, *g: (jnp.minimum(i, ctx_tiles - 1),
                                                jnp.where(i < ctx_tiles, col(i, *g), 0))),
             pl.BlockSpec(block, lambda i, *g: (jnp.maximum(i - ctx_tiles, 0),
                                                jnp.where(i >= ctx_tiles, col(i, *g), 0)))], list(x))


def _read_rows(refs, ctx_tiles):
    if len(refs) == 1:
        return refs[0][...]
    return jnp.where(pl.program_id(0) < ctx_tiles, refs[0][...], refs[1][...])


def _norm_kernel(*refs, n_x, ctx_tiles):
    g_ref, sh_ref, sc_ref, o_ref = refs[n_x:]
    x = _read_rows(refs[:n_x], ctx_tiles)
    y = x * lax.rsqrt(jnp.mean(x * x, axis=-1, keepdims=True) + EPS)
    h = (y * g_ref[...]) * (1.0 + sc_ref[0, 0]) + sh_ref[0, 0]
    o_ref[...] = h.astype(o_ref.dtype)


def _norm_mod(x, g, mod4, comp_shift, comp_scale, geom, out_dtype):
    d = g.shape[0]
    n = geom["n"]
    tm = 256
    ctx_tiles = geom["n_ctx"] // tm
    x_specs, x_args = _row_specs(x, (tm, d), ctx_tiles)
    return pl.pallas_call(
        functools.partial(_norm_kernel, n_x=len(x_args), ctx_tiles=ctx_tiles),
        grid=(n // tm,),
        in_specs=x_specs + [
            pl.BlockSpec((1, d), lambda i: (0, 0)),
            _mod_spec(d, tm, geom, comp_shift),
            _mod_spec(d, tm, geom, comp_scale),
        ],
        out_specs=pl.BlockSpec((tm, d), lambda i: (i, 0)),
        out_shape=jax.ShapeDtypeStruct((n, d), out_dtype),
        compiler_params=_cparams(("arbitrary",)),
        name="norm_mod",
    )(*x_args, g.reshape(1, d), mod4, mod4)


def _mm_kernel(x_ref, w_ref, o_ref):
    x = x_ref[...]
    o_ref[...] = _dot(x, w_ref[...].astype(x.dtype)).astype(o_ref.dtype)


def _matmul(x, w, out_dtype, tm, tn, name):
    m, k = x.shape
    n = w.shape[1]
    return pl.pallas_call(
        _mm_kernel,
        grid=(m // tm, n // tn),
        in_specs=[
            pl.BlockSpec((tm, k), lambda i, j: (i, 0)),
            pl.BlockSpec((k, tn), lambda i, j: (0, j)),
        ],
        out_specs=pl.BlockSpec((tm, tn), lambda i, j: (i, j)),
        out_shape=jax.ShapeDtypeStruct((m, n), out_dtype),
        compiler_params=_cparams(("arbitrary", "arbitrary")),
        name=name,
    )(x, w)


def _repack_kernel(starts_ref, wt_ref, o_ref):
    del starts_ref
    o_ref[...] = wt_ref[...].T.astype(o_ref.dtype)


def _repack_proj(w_in_t, layer, starts, tn):
    d = w_in_t.shape[2]
    grid_spec = pltpu.PrefetchScalarGridSpec(
        num_scalar_prefetch=1,
        grid=(len(starts),),
        in_specs=[pl.BlockSpec((None, pl.Element(tn), pl.Element(d)),
                               lambda j, st: (layer, pl.multiple_of(st[j], 2 * SUBLANES), 0))],
        out_specs=pl.BlockSpec((d, tn), lambda j, st: (0, j)),
    )
    return pl.pallas_call(
        _repack_kernel,
        grid_spec=grid_spec,
        out_shape=jax.ShapeDtypeStruct((d, len(starts) * tn), BF16),
        compiler_params=_cparams(("arbitrary",)),
        name="repack_proj",
    )(jnp.asarray(starts, jnp.int32), w_in_t)


def _s5_kernel(*refs, t_len, bp, tc, has_init, want_final):
    it = iter(refs)
    u_ref, bm_ref, cm_ref, a_ref, d_ref = (next(it) for _ in range(5))
    x0_ref = next(it) if has_init else None
    y_ref = next(it)
    xf_ref = next(it) if want_final else None
    bu_refs = (next(it), next(it))
    y_refs = (next(it), next(it))
    st_refs = (next(it), next(it))

    nc = t_len // tc
    rc = tc * bp
    half = bu_refs[0].shape[1] // 2
    for d in (0, 1):
        st_refs[d][...] = x0_ref[d, 0] if has_init else jnp.zeros_like(st_refs[d])
    a_parts = [(a_ref[d, 0, :, :half], a_ref[d, 0, :, half:]) for d in (0, 1)]

    def chunk_body(c, carry):
        r0s = (pl.multiple_of(c * rc, rc), pl.multiple_of((nc - 1 - c) * rc, rc))
        for d in (0, 1):
            bu_refs[d][...] = _dot(u_ref[pl.ds(r0s[d], rc), :], bm_ref[d, 0])
        for rt in range(bp // SUBLANES):
            rows = slice(rt * SUBLANES, (rt + 1) * SUBLANES)

            def step(t, s, rt=rt):
                out = []
                for d in (0, 1):
                    sr, si = s[2 * d], s[2 * d + 1]
                    ar, ai = a_parts[d]
                    tt = t if d == 0 else tc - 1 - t
                    row = pl.multiple_of(tt * bp + rt * SUBLANES, SUBLANES)
                    b = bu_refs[d][pl.ds(row, SUBLANES), :]
                    xr = ar * sr - ai * si + b[:, :half]
                    xi = ar * si + ai * sr + b[:, half:]
                    bu_refs[d][pl.ds(row, SUBLANES), :] = jnp.concatenate([xr, xi], axis=-1)
                    out += [xr, xi]
                return tuple(out)

            init = tuple(st_refs[d][rows, sl] for d in (0, 1) for sl in (slice(0, half), slice(half, 2 * half)))
            fin = lax.fori_loop(0, tc, step, init, unroll=math.gcd(tc, 2))
            for d in (0, 1):
                st_refs[d][rows, :] = jnp.concatenate([fin[2 * d], fin[2 * d + 1]], axis=-1)
        for d in (0, 1):
            y_refs[d][pl.ds(r0s[d], rc), :] = _dot(bu_refs[d][...].astype(BF16), cm_ref[0])
        return carry

    lax.fori_loop(0, nc, chunk_body, 0)
    if want_final:
        for d in (0, 1):
            xf_ref[d, 0] = st_refs[d][...]
    y = y_refs[0][...] + y_refs[1][...] + d_ref[...] * u_ref[...].astype(F32)
    y_ref[...] = jax.nn.gelu(y).astype(y_ref.dtype)


def _s5_mixer(u_tm, bm, cm, a8, dvec, x0, t_len, bp, want_final):
    rows, mix = u_tm.shape
    n_slab = mix // S5_SLAB
    two_half = bm.shape[-1]
    tc = _pick(t_len, tuple(max(1, 1024 // bp) >> s for s in range(6)))
    has_init = x0 is not None
    in_specs = [
        pl.BlockSpec((rows, S5_SLAB), lambda s: (0, s)),
        pl.BlockSpec((2, 1, S5_SLAB, two_half), lambda s: (0, s, 0, 0)),
        pl.BlockSpec((1, two_half, S5_SLAB), lambda s: (s, 0, 0)),
        pl.BlockSpec((2, 1, SUBLANES, two_half), lambda s: (0, s, 0, 0)),
        pl.BlockSpec((1, S5_SLAB), lambda s: (0, s)),
    ]
    args = [u_tm, bm, cm, a8, dvec]
    if has_init:
        in_specs.append(pl.BlockSpec((2, 1, bp, two_half), lambda s: (0, s, 0, 0)))
        args.append(x0)
    out_specs = [pl.BlockSpec((rows, S5_SLAB), lambda s: (0, s))]
    out_shape = [jax.ShapeDtypeStruct((rows, mix), BF16)]
    if want_final:
        out_specs.append(pl.BlockSpec((2, 1, bp, two_half), lambda s: (0, s, 0, 0)))
        out_shape.append(jax.ShapeDtypeStruct((2, n_slab, bp, two_half), F32))
    res = pl.pallas_call(
        functools.partial(_s5_kernel, t_len=t_len, bp=bp, tc=tc, has_init=has_init, want_final=want_final),
        grid=(n_slab,),
        in_specs=in_specs,
        out_specs=out_specs,
        out_shape=out_shape,
        scratch_shapes=[pltpu.VMEM((tc * bp, two_half), F32)] * 2 + [pltpu.VMEM((rows, S5_SLAB), F32)] * 2
        + [pltpu.VMEM((bp, two_half), F32)] * 2,
        compiler_params=_cparams(("arbitrary",)),
        name="s5_ctx" if want_final else "s5_lat",
    )(*args)
    return (res[0], res[1]) if want_final else (res[0], None)


def _glu_kernel(y_ref, w_ref, o_ref):
    y = y_ref[...]
    z = _dot(y, w_ref[...])
    o_ref[...] = (y.astype(F32) * jax.nn.sigmoid(z)).astype(o_ref.dtype)


def _glu(y, w):
    n, mix = y.shape
    tm = _pick(n, (512, 256, 128))
    return pl.pallas_call(
        _glu_kernel,
        grid=(n // tm,),
        in_specs=[pl.BlockSpec((tm, mix), lambda i: (i, 0)), pl.BlockSpec((mix, mix), lambda i: (0, 0))],
        out_specs=pl.BlockSpec((tm, mix), lambda i: (i, 0)),
        out_shape=jax.ShapeDtypeStruct((n, mix), BF16),
        compiler_params=_cparams(("arbitrary",)),
        name="s5_glu",
    )(y, w)


def _head_norm_gate(o, normg_ref, gate, act):
    of = o * lax.rsqrt(jnp.mean(o * o, axis=-1, keepdims=True) + EPS)
    return of * normg_ref[...] * act(gate)


def _seq_geometry(geom, ctx):
    if ctx:
        return geom["batch"], geom["seq"], 0
    return geom["dec_batch"], geom["dec_seq"], geom["n_ctx"] // geom["dec_seq"]


def _col_spec(t_len, width, col0, blk0):
    base = col0 // width
    return pl.BlockSpec((t_len, width), lambda b, h: (blk0 + b, base + h))


def _launch_mixer(body, name, geom, ctx, width, in_specs, args, extra_out_specs, extra_out_shapes, scratch, **kw):
    nb, t_len, _ = _seq_geometry(geom, ctx)
    heads = geom["mix"] // width
    out_specs = [pl.BlockSpec((t_len, width), lambda b, h: (b, h))] + list(extra_out_specs)
    out_shapes = [jax.ShapeDtypeStruct((nb * t_len, geom["mix"]), BF16)] + list(extra_out_shapes)
    return pl.pallas_call(
        functools.partial(body, **kw),
        grid=(nb, heads),
        in_specs=list(in_specs),
        out_specs=out_specs,
        out_shape=out_shapes,
        scratch_shapes=scratch,
        compiler_params=_cparams(("arbitrary", "arbitrary")),
        name=name,
    )(*args)


def _gla_kernel(*refs, t_len, dk, has_init, want_final):
    it = iter(refs)
    q_ref, k_ref, v_ref, g_ref, zg_ref, wa_ref, ba_ref, normg_ref = (next(it) for _ in range(8))
    s0_ref = next(it) if has_init else None
    y_ref = next(it)
    sf_ref = next(it) if want_final else None
    of_ref, ob_ref, cumf_ref, cumb_ref, stf_ref, stb_ref = (next(it) for _ in range(6))

    cl = min(GLA_CHUNK, t_len)
    nc = t_len // cl
    blk = min(SEQ_CHUNK, t_len)
    scale = dk ** -0.5
    dirs = ((of_ref, cumf_ref, stf_ref), (ob_ref, cumb_ref, stb_ref))

    zg = zg_ref[...]
    ti = _iota2((blk, blk), 0)
    si = _iota2((blk, blk), 1)
    same_chunk = (ti // cl) == (si // cl)
    for d, (_, cum_ref, st_ref) in enumerate(dirs):
        la = _log_sigmoid(_dot3(zg, wa_ref[d, 0], _dot) + ba_ref[d, 0]) / GLA_TAU
        tri = jnp.where(same_chunk & ((si <= ti) if d == 0 else (si >= ti)), 1.0, 0.0).astype(BF16)
        la_hi, la_lo = _split_bf16(la)
        for p in range(t_len // blk):
            rows = slice(p * blk, (p + 1) * blk)
            cum_ref[rows, :] = _dot(tri, la_hi[rows]) + _dot(tri, la_lo[rows])
        st_ref[...] = s0_ref[0, d, 0].T if has_init else jnp.zeros_like(st_ref)

    tl = _iota2((cl, cl), 0)
    sl = _iota2((cl, cl), 1)
    keeps = (sl <= tl, sl >= tl)

    def chunk_body(c, carry):
        for d, (o_ref, cum_ref, st_ref) in enumerate(dirs):
            cc = c if d == 0 else nc - 1 - c
            r0 = pl.multiple_of(cc * cl, cl)
            cum = cum_ref[pl.ds(r0, cl), :]
            tot = cum[cl - 1:cl, :] if d == 0 else cum[0:1, :]
            q = q_ref[pl.ds(r0, cl), :].astype(F32) * scale
            k = k_ref[pl.ds(r0, cl), :].astype(F32)
            v = v_ref[pl.ds(r0, cl), :]
            qd = (q * jnp.exp(cum)).astype(BF16)
            kd = (k * jnp.exp(-cum)).astype(BF16)
            sc = jnp.where(keeps[d], _nt(qd, kd), 0.0)
            st = st_ref[...]
            o_ref[pl.ds(r0, cl), :] = _nt(qd, st.astype(BF16)) + _dot(sc.astype(BF16), v)
            kl = (k * jnp.exp(tot - cum)).astype(BF16)
            st_ref[...] = st * jnp.exp(tot) + _tn(v, kl)
        return carry

    lax.fori_loop(0, nc, chunk_body, 0)
    if want_final:
        sf_ref[0, 0, 0] = stf_ref[...].T
        sf_ref[0, 1, 0] = stb_ref[...].T
    y = _head_norm_gate(of_ref[...] + ob_ref[...], normg_ref, g_ref[...].astype(F32), _silu)
    y_ref[...] = y.astype(y_ref.dtype)


def _gla_mixer(z, zg, gp, s0, geom, cols, ctx):
    nb, t_len, blk0 = _seq_geometry(geom, ctx)
    mix, heads = geom["mix"], geom["gla_heads"]
    dk, dv = mix // 2 // heads, mix // heads
    has_init, want_final = s0 is not None, ctx
    in_specs = [
        _col_spec(t_len, dk, cols["gla_q"], blk0),
        _col_spec(t_len, dk, cols["gla_k"], blk0),
        _col_spec(t_len, dv, cols["gla_v"], blk0),
        _col_spec(t_len, dv, cols["gla_g"], blk0),
        pl.BlockSpec((t_len, LANES), lambda b, h: (blk0 + b, 0)),
        pl.BlockSpec((2, 1, LANES, dk), lambda b, h: (0, h, 0, 0)),
        pl.BlockSpec((2, 1, 1, dk), lambda b, h: (0, h, 0, 0)),
        pl.BlockSpec((1, dv), lambda b, h: (0, h)),
    ]
    args = [z, z, z, z, zg, gp["wa"], gp["ba"], gp["normg"]]
    if has_init:
        in_specs.append(pl.BlockSpec((1, 2, 1, dk, dv), lambda b, h: (b, 0, h, 0, 0)))
        args.append(s0)
    x_specs, x_shapes = [], []
    if want_final:
        x_specs.append(pl.BlockSpec((1, 2, 1, dk, dv), lambda b, h: (b, 0, h, 0, 0)))
        x_shapes.append(jax.ShapeDtypeStruct((nb, 2, heads, dk, dv), F32))
    res = _launch_mixer(_gla_kernel, "gla_ctx" if ctx else "gla_lat", geom, ctx, dv, in_specs, args,
                        x_specs, x_shapes,
                        [pltpu.VMEM((t_len, dv), F32)] * 2 + [pltpu.VMEM((t_len, dk), F32)] * 2
                        + [pltpu.VMEM((dv, dk), F32)] * 2,
                        t_len=t_len, dk=dk, has_init=has_init, want_final=want_final)
    return (res[0], res[1]) if want_final else (res[0], None)


def _ret_kernel(*refs, t_len, dh, has_init, want_final, rope):
    it = iter(refs)
    lg_ref = next(it)
    q_ref, k_ref, v_ref, g_ref, normg_ref = (next(it) for _ in range(5))
    cos_ref, sin_ref = (next(it), next(it)) if rope else (None, None)
    s0_ref = next(it) if has_init else None
    y_ref = next(it)
    sf_ref = next(it) if want_final else None
    o_ref = next(it)

    h = pl.program_id(1)
    n_heads = lg_ref.shape[0] // 2
    lgf = lg_ref[h]
    lgb = lg_ref[n_heads + h]
    cl = min(SEQ_CHUNK, t_len)
    nc = t_len // cl
    scale = dh ** -0.5
    half = dh // 2

    q = q_ref[...].astype(F32)
    k = k_ref[...].astype(F32) * scale
    if rope:
        cos = cos_ref[...]
        sin = sin_ref[...]

        def rot(x):
            x1, x2 = x[:, :half], x[:, half:]
            return jnp.concatenate([x1 * cos - x2 * sin, x1 * sin + x2 * cos], axis=-1)

        q, k = rot(q), rot(k)
    qb = q.astype(BF16)
    kb = k.astype(BF16)

    dt = (_iota2((cl, cl), 0) - _iota2((cl, cl), 1)).astype(F32)
    decay = (jnp.where(dt >= 0, jnp.exp(lgf * jnp.maximum(dt, 0.0)), 0.0)
             + jnp.where(dt <= 0, jnp.exp(lgb * jnp.maximum(-dt, 0.0)), 0.0))
    tcol = _iota2((cl, 1), 0).astype(F32)

    def rows(c):
        return slice(c * cl, (c + 1) * cl)

    s_f = s0_ref[0, 0, 0] if has_init else None
    for c in range(nc):
        qc, kc, vc = qb[rows(c)], kb[rows(c)], v_ref[rows(c), :]
        o = _dot((_nt(qc, kc) * decay).astype(BF16), vc)
        if s_f is not None:
            o = o + _dot((q[rows(c)] * jnp.exp(lgf * (tcol + 1.0))).astype(BF16), s_f.astype(BF16))
        o_ref[rows(c), :] = o
        if c < nc - 1 or want_final:
            upd = _tn((k[rows(c)] * jnp.exp(lgf * (cl - 1.0 - tcol))).astype(BF16), vc)
            s_f = upd if s_f is None else jnp.exp(lgf * cl) * s_f + upd
    if want_final:
        sf_ref[0, 0, 0] = s_f
    s_b = s0_ref[0, 1, 0] if has_init else None
    for c in range(nc - 1, -1, -1):
        vc = v_ref[rows(c), :]
        if s_b is not None:
            o_ref[rows(c), :] += _dot((q[rows(c)] * jnp.exp(lgb * (cl - tcol))).astype(BF16), s_b.astype(BF16))
        if c > 0 or want_final:
            upd = _tn((k[rows(c)] * jnp.exp(lgb * tcol)).astype(BF16), vc)
            s_b = upd if s_b is None else jnp.exp(lgb * cl) * s_b + upd
    if want_final:
        sf_ref[0, 1, 0] = s_b
    y = _head_norm_gate(o_ref[...], normg_ref, g_ref[...].astype(F32), _silu)
    y_ref[...] = y.astype(y_ref.dtype)


def _ret_mixer(z, lg, normg, rope_tabs, s0, geom, cols, ctx):
    nb, t_len, blk0 = _seq_geometry(geom, ctx)
    mix, heads = geom["mix"], geom["ret_heads"]
    dh = mix // heads
    has_init, want_final, rope = s0 is not None, ctx, rope_tabs is not None
    in_specs = [
        pl.BlockSpec(memory_space=pltpu.SMEM),
        _col_spec(t_len, dh, cols["ret_q"], blk0),
        _col_spec(t_len, dh, cols["ret_k"], blk0),
        _col_spec(t_len, dh, cols["ret_v"], blk0),
        _col_spec(t_len, dh, cols["ret_g"], blk0),
        pl.BlockSpec((1, dh), lambda b, h: (0, h)),
    ]
    args = [lg, z, z, z, z, normg]
    if rope:
        in_specs += [pl.BlockSpec((t_len, dh // 2), lambda b, h: (0, 0))] * 2
        args += list(rope_tabs)
    if has_init:
        in_specs.append(pl.BlockSpec((1, 2, 1, dh, dh), lambda b, h: (b, 0, h, 0, 0)))
        args.append(s0)
    x_specs, x_shapes = [], []
    if want_final:
        x_specs.append(pl.BlockSpec((1, 2, 1, dh, dh), lambda b, h: (b, 0, h, 0, 0)))
        x_shapes.append(jax.ShapeDtypeStruct((nb, 2, heads, dh, dh), F32))
    res = _launch_mixer(_ret_kernel, "ret_ctx" if ctx else "ret_lat", geom, ctx, dh, in_specs, args,
                        x_specs, x_shapes, [pltpu.VMEM((t_len, dh), F32)],
                        t_len=t_len, dh=dh, has_init=has_init, want_final=want_final, rope=rope)
    return (res[0], res[1]) if want_final else (res[0], None)


def _ml_kernel(*refs, t_len, dh, n_heads, layer, depth, gate_lane0, has_init, want_final):
    it = iter(refs)
    bias_ref = next(it)
    m0_ref = next(it) if has_init else None
    q_ref, k_ref, v_ref, og_ref, zg_ref, zgt_ref, normg_ref = (next(it) for _ in range(7))
    c0_ref, n0_ref = (next(it), next(it)) if has_init else (None, None)
    y_ref = next(it)
    cf_ref, nf_ref, mf_ref = (next(it), next(it), next(it)) if want_final else (None, None, None)
    o_ref = next(it)

    b = pl.program_id(0)
    h = pl.program_id(1)
    cl = min(SEQ_CHUNK, t_len)
    nc = t_len // cl
    scale = dh ** -0.5

    qb = q_ref[...]
    qf = qb.astype(F32)
    kf = k_ref[...].astype(F32) * scale
    kb = kf.astype(BF16)

    zg = zg_ref[...]
    zgt = zgt_ref[...]
    lane = _iota2((1, zg.shape[1]), 1)
    sub = _iota2((zgt.shape[0], 1), 0)
    ti = _iota2((cl, cl), 0)
    si = _iota2((cl, cl), 1)

    def rows(c):
        return slice(c * cl, (c + 1) * cl)

    for d in (0, 1):
        gi = d * 2 * n_heads + h
        gf = gi + n_heads
        bi = bias_ref[d * n_heads + h]
        bf = bias_ref[(2 + d) * n_heads + h]
        i_col = jnp.sum(jnp.where(lane == gate_lane0 + gi, zg, 0.0), axis=1, keepdims=True) + bi
        f_col = _log_sigmoid(jnp.sum(jnp.where(lane == gate_lane0 + gf, zg, 0.0), axis=1, keepdims=True) + bf)
        i_row = jnp.sum(jnp.where(sub == gi, zgt, 0.0), axis=0, keepdims=True) + bi
        f_row = _log_sigmoid(jnp.sum(jnp.where(sub == gf, zgt, 0.0), axis=0, keepdims=True) + bf)
        keep = (si <= ti) if d == 0 else (si >= ti)
        keep_t = (ti <= si) if d == 0 else (ti >= si)

        if has_init:
            c_st = c0_ref[0, d, 0]
            n_st = n0_ref[0, d, 0]
            m_st = jnp.full((1, 1), m0_ref[((b * depth + layer) * 2 + d) * n_heads + h], F32)
        else:
            c_st = None
            n_st = None
            m_st = jnp.zeros((1, 1), F32)

        order = range(nc) if d == 0 else range(nc - 1, -1, -1)
        for pos, c in enumerate(order):
            rc = rows(c)
            qc, kc, vc = qb[rc], kb[rc], v_ref[rc, :]
            ic = i_col[rc]
            ir, fr = i_row[:, rc], f_row[:, rc]
            f_cum_col = jnp.sum(jnp.where(keep, fr, 0.0), axis=1, keepdims=True)
            f_cum_row = jnp.sum(jnp.where(keep_t, f_col[rc], 0.0), axis=0, keepdims=True)
            dmat = jnp.where(keep, f_cum_col + (ir - f_cum_row), -jnp.inf)
            g = f_cum_col + m_st
            m_t = jnp.maximum(g, jnp.max(dmat, axis=1, keepdims=True))
            p = _nt(qc, kc) * jnp.exp(dmat - m_t)
            num = _dot(p.astype(BF16), vc)
            den = jnp.sum(p, axis=1, keepdims=True)
            if c_st is not None:
                w_state = jnp.exp(g - m_t)
                num = num + w_state * _dot(qc, c_st.astype(BF16))
                den = den + w_state * jnp.sum(qf[rc] * n_st, axis=1, keepdims=True)
            hh = num / jnp.maximum(jnp.abs(den), jnp.exp(-m_t))
            if d == 0:
                o_ref[rc, :] = hh
            else:
                o_ref[rc, :] += hh
            if pos < nc - 1 or want_final:
                f_last = jnp.sum(fr, axis=1, keepdims=True)
                src = f_last - f_cum_col + ic
                m_new = jnp.maximum(f_last + m_st, jnp.max(src, axis=0, keepdims=True))
                kw = kf[rc] * jnp.exp(src - m_new)
                upd_c = _tn(kw.astype(BF16), vc)
                upd_n = jnp.sum(kw, axis=0, keepdims=True)
                if c_st is not None:
                    w_keep = jnp.exp(f_last + m_st - m_new)
                    c_st = w_keep * c_st + upd_c
                    n_st = w_keep * n_st + upd_n
                else:
                    c_st, n_st = upd_c, upd_n
                m_st = m_new
        if want_final:
            cf_ref[0, d, 0] = c_st
            nf_ref[0, d, 0] = n_st
            mf_ref[0, d, 0] = jnp.broadcast_to(m_st, (1, mf_ref.shape[-1]))
    y = _head_norm_gate(o_ref[...], normg_ref, og_ref[...].astype(F32), jax.nn.sigmoid)
    y_ref[...] = y.astype(y_ref.dtype)


def _ml_mixer(z, zg, zg_block, zgt, bias, normg, init, layer, depth, gate_lane0, geom, cols, ctx):
    nb, t_len, blk0 = _seq_geometry(geom, ctx)
    mix, heads = geom["mix"], geom["ml_heads"]
    dh = mix // heads
    has_init, want_final = init is not None, ctx
    smem = pl.BlockSpec(memory_space=pltpu.SMEM)
    in_specs, args = [smem], [bias]
    if has_init:
        in_specs.append(smem)
        args.append(init["m"])
    in_specs += [
        _col_spec(t_len, dh, cols["ml_q"], blk0),
        _col_spec(t_len, dh, cols["ml_k"], blk0),
        _col_spec(t_len, dh, cols["ml_v"], blk0),
        _col_spec(t_len, dh, cols["ml_o"], blk0),
        pl.BlockSpec((t_len, LANES), lambda b, h: (blk0 + b, zg_block)),
        pl.BlockSpec((zgt.shape[0], t_len), lambda b, h: (0, blk0 + b)),
        pl.BlockSpec((1, dh), lambda b, h: (0, h)),
    ]
    args += [z, z, z, z, zg, zgt, normg]
    if has_init:
        in_specs += [pl.BlockSpec((1, 2, 1, dh, dh), lambda b, h: (b, 0, h, 0, 0)),
                     pl.BlockSpec((1, 2, 1, 1, dh), lambda b, h: (b, 0, h, 0, 0))]
        args += [init["c"], init["n"]]
    x_specs, x_shapes = [], []
    if want_final:
        x_specs = [pl.BlockSpec((1, 2, 1, dh, dh), lambda b, h: (b, 0, h, 0, 0)),
                   pl.BlockSpec((1, 2, 1, 1, dh), lambda b, h: (b, 0, h, 0, 0)),
                   pl.BlockSpec((1, 2, 1, 1, LANES), lambda b, h: (b, 0, h, 0, 0))]
        x_shapes = [jax.ShapeDtypeStruct((nb, 2, heads, dh, dh), F32),
                    jax.ShapeDtypeStruct((nb, 2, heads, 1, dh), F32),
                    jax.ShapeDtypeStruct((nb, 2, heads, 1, LANES), F32)]
    res = _launch_mixer(_ml_kernel, "mlstm_ctx" if ctx else "mlstm_lat", geom, ctx, dh, in_specs, args,
                        x_specs, x_shapes, [pltpu.VMEM((t_len, dh), F32)],
                        t_len=t_len, dh=dh, n_heads=heads, layer=layer, depth=depth, gate_lane0=gate_lane0,
                        has_init=has_init, want_final=want_final)
    return (res[0], res[1:]) if want_final else (res[0], None)


def _merge_kernel(*refs, nbr, ctx_tiles):
    h_ref = refs[0]
    y_refs = refs[1:1 + 2 * nbr]
    m_refs = refs[1 + 2 * nbr:1 + 3 * nbr]
    wb_ref, o_ref = refs[1 + 3 * nbr], refs[2 + 3 * nbr]
    is_ctx = pl.program_id(0) < ctx_tiles
    h = h_ref[...]
    acc = None
    for i in range(nbr):
        y = jnp.where(is_ctx, y_refs[2 * i][...], y_refs[2 * i + 1][...])
        term = jax.nn.sigmoid(_dot(h, m_refs[i][...])) * _dot(y, wb_ref[i].astype(y.dtype))
        acc = term if acc is None else acc + term
    o_ref[...] = acc.astype(o_ref.dtype)


def _merge(h, y_pairs, w_merge, w_branch, layer, geom):
    n, d = h.shape
    mix = geom["mix"]
    nbr = len(y_pairs)
    tm = _pick(math.gcd(geom["n_ctx"], n - geom["n_ctx"]), (512, 256, 128))
    tn = _pick(d, (256, 128))
    nj = d // tn
    ctx_tiles = geom["n_ctx"] // tm
    in_specs = [pl.BlockSpec((tm, d), lambda i, j: (i, 0))]
    for _ in range(nbr):
        in_specs.append(pl.BlockSpec((tm, mix), lambda i, j: (jnp.minimum(i, ctx_tiles - 1), 0)))
        in_specs.append(pl.BlockSpec((tm, mix), lambda i, j: (jnp.maximum(i - ctx_tiles, 0), 0)))
    in_specs += [pl.BlockSpec((d, tn), lambda i, j, br=br: (0, br * nj + j)) for br in range(nbr)]
    in_specs += [pl.BlockSpec((None, nbr, mix, tn), lambda i, j: (layer, 0, 0, j))]
    flat = [y for pair in y_pairs for y in pair]
    return pl.pallas_call(
        functools.partial(_merge_kernel, nbr=nbr, ctx_tiles=ctx_tiles),
        grid=(n // tm, nj),
        in_specs=in_specs,
        out_specs=pl.BlockSpec((tm, tn), lambda i, j: (i, j)),
        out_shape=jax.ShapeDtypeStruct((n, d), BF16),
        compiler_params=_cparams(("arbitrary", "arbitrary")),
        name="merge",
    )(h, *flat, *([w_merge] * nbr), w_branch)


def _outproj_kernel(*refs, n_x, ctx_tiles):
    m_ref, w_ref = refs[:2]
    g_ref, o_ref = refs[2 + n_x:]
    x = _read_rows(refs[2:2 + n_x], ctx_tiles)
    m = m_ref[...]
    o_ref[...] = x + g_ref[0, 0] * _dot(m, w_ref[...].astype(m.dtype))


def _outproj(merged, w_out, layer, x, mod4, comp_gate, geom):
    n, d = merged.shape
    tm = _pick(math.gcd(geom["n_ctx"], geom["dec_seq"]), (1024, 512, 256, 128))
    tn = _pick(d, (512, 256, 128))
    n_ctx, dec_seq, ctx_row = geom["n_ctx"], geom["dec_seq"], geom["ctx_row"]
    ctx_tiles = n_ctx // tm

    def gate_map(i, j):
        start = i * tm
        return (jnp.where(start < n_ctx, ctx_row, (start - n_ctx) // dec_seq), comp_gate, 0, j)

    x_specs, x_args = _row_specs(x, (tm, tn), ctx_tiles, col=lambda i, j: j)
    return pl.pallas_call(
        functools.partial(_outproj_kernel, n_x=len(x_args), ctx_tiles=ctx_tiles),
        grid=(n // tm, d // tn),
        in_specs=[
            pl.BlockSpec((tm, d), lambda i, j: (i, 0)),
            pl.BlockSpec((None, d, tn), lambda i, j: (layer, 0, j)),
        ] + x_specs + [pl.BlockSpec((1, 1, 1, tn), gate_map)],
        out_specs=pl.BlockSpec((tm, tn), lambda i, j: (i, j)),
        out_shape=jax.ShapeDtypeStruct((n, d), F32),
        compiler_params=_cparams(("arbitrary", "arbitrary")),
        name="out_proj",
    )(merged, w_out, *x_args, mod4)


def _first_argmax(vals):
    best, idx = vals[0], jnp.zeros_like(vals[0])
    for j in range(1, len(vals)):
        better = vals[j] > best
        idx = jnp.where(better, float(j), idx)
        best = jnp.where(better, vals[j], best)
    return idx, best


def _pick_row(vals, idx):
    out = vals[0]
    for j in range(1, len(vals)):
        out = jnp.where(idx == float(j), vals[j], out)
    return out


def _router_kernel(x_ref, g_ref, sh_ref, sc_ref, wrt_ref, rb_ref, h_ref, eidx_ref, wgt_ref, rank_ref, cnt_ref,
                   carry_ref, *, n_exp, n_groups):
    i = pl.program_id(0)
    x = x_ref[...]
    y = x * lax.rsqrt(jnp.mean(x * x, axis=-1, keepdims=True) + EPS)
    h = (y * g_ref[...]) * (1.0 + sc_ref[0, 0]) + sh_ref[0, 0]
    h_ref[...] = _pack_bf16_pairs(h)
    tm = x.shape[0]
    per = n_exp // n_groups

    logits = _dot3(wrt_ref[...], h, _nt)
    scores = jax.nn.sigmoid(logits)
    sel = scores + rb_ref[...]
    sel_rows = [sel[e:e + 1, :] for e in range(n_exp)]
    sc_rows = [scores[e:e + 1, :] for e in range(n_exp)]
    group_scores = []
    for g in range(n_groups):
        r = sel_rows[g * per:(g + 1) * per]
        pair = None
        for a in range(per):
            for bb in range(a + 1, per):
                s2 = r[a] + r[bb]
                pair = s2 if pair is None else jnp.maximum(pair, s2)
        group_scores.append(pair)
    g_best, _ = _first_argmax(group_scores)
    in_sel = [_pick_row([sel_rows[g * per + j] for g in range(n_groups)], g_best) for j in range(per)]
    in_sc = [_pick_row([sc_rows[g * per + j] for g in range(n_groups)], g_best) for j in range(per)]
    j1, _ = _first_argmax(in_sel)
    masked = [jnp.where(j1 == float(j), -jnp.inf, in_sel[j]) for j in range(per)]
    j2, _ = _first_argmax(masked)
    w1 = _pick_row(in_sc, j1)
    w2 = _pick_row(in_sc, j2)
    tot = w1 + w2
    e1 = (g_best * per + j1).astype(jnp.int32)
    e2 = (g_best * per + j2).astype(jnp.int32)
    eidx_ref[...] = jnp.concatenate([e1, e2], axis=0)
    wgt_ref[...] = jnp.concatenate([w1 / tot, w2 / tot], axis=0)

    @pl.when(i == 0)
    def _():
        carry_ref[...] = jnp.zeros_like(carry_ref)

    eio = _iota2((n_exp, tm), 0)
    hit1 = eio == e1
    hit2 = eio == e2
    onehot = jnp.where(hit1 | hit2, 1.0, 0.0)
    before = (_iota2((tm, tm), 0) < _iota2((tm, tm), 1)).astype(BF16)
    prefix = _dot(onehot.astype(BF16), before) + carry_ref[:, 0:1]
    r1 = jnp.sum(jnp.where(hit1, prefix, 0.0), axis=0, keepdims=True)
    r2 = jnp.sum(jnp.where(hit2, prefix, 0.0), axis=0, keepdims=True)
    rank_ref[...] = jnp.concatenate([r1, r2], axis=0).astype(jnp.int32)
    carry_ref[...] = carry_ref[...] + jnp.sum(onehot, axis=1, keepdims=True)
    cnt_ref[...] = carry_ref[...]


def _norm_router(x, g, mod4, comp_shift, comp_scale, w_router_t, router_bias, geom):
    n, d = x.shape
    n_exp = w_router_t.shape[0]
    tm = 256
    row2 = pl.BlockSpec((2, tm), lambda i: (0, i))
    return pl.pallas_call(
        functools.partial(_router_kernel, n_exp=n_exp, n_groups=N_EXPERT_GROUPS),
        grid=(n // tm,),
        in_specs=[
            pl.BlockSpec((tm, d), lambda i: (i, 0)),
            pl.BlockSpec((1, d), lambda i: (0, 0)),
            _mod_spec(d, tm, geom, comp_shift),
            _mod_spec(d, tm, geom, comp_scale),
            pl.BlockSpec((n_exp, d), lambda i: (0, 0)),
            pl.BlockSpec((n_exp, 1), lambda i: (0, 0)),
        ],
        out_specs=[pl.BlockSpec((tm, d // 2), lambda i: (i, 0)), row2, row2, row2,
                   pl.BlockSpec((n_exp, LANES), lambda i: (0, 0))],
        out_shape=[jax.ShapeDtypeStruct((n, d // 2), jnp.uint32),
                   jax.ShapeDtypeStruct((2, n), jnp.int32),
                   jax.ShapeDtypeStruct((2, n), F32),
                   jax.ShapeDtypeStruct((2, n), jnp.int32),
                   jax.ShapeDtypeStruct((n_exp, LANES), F32)],
        scratch_shapes=[pltpu.VMEM((n_exp, LANES), F32)],
        compiler_params=_cparams(("arbitrary",)),
        name="norm_router",
    )(x, g.reshape(1, d), mod4, mod4, w_router_t, router_bias.reshape(n_exp, 1))


def _row_copy(src_hbm, row, dst_vmem, r, sem):
    return pltpu.make_async_copy(src_hbm.at[pl.ds(row, 1), :], dst_vmem.at[pl.ds(r, 1), :], sem)


def _dispatch_kernel(src_ref, h_hbm, o_ref, sem):
    tg = o_ref.shape[0]

    def start(i, c):
        for u in range(2):
            r = 2 * i + u
            _row_copy(h_hbm, src_ref[0, 0, r], o_ref, r, sem).start(priority=u)
        return c

    def wait(r, c):
        _row_copy(h_hbm, 0, o_ref, r, sem).wait()
        return c

    lax.fori_loop(0, tg // 2, start, 0, unroll=8)
    lax.fori_loop(0, tg, wait, 0, unroll=16)


def _dispatch(h, src):
    p_rows = src.shape[0]
    dw = h.shape[1]
    tg = _pick(p_rows, (8 * ROUTE_TILE, 4 * ROUTE_TILE, 2 * ROUTE_TILE, ROUTE_TILE))
    return pl.pallas_call(
        _dispatch_kernel,
        grid=(p_rows // tg,),
        in_specs=[
            pl.BlockSpec((1, 1, tg), lambda i: (i, 0, 0), memory_space=pltpu.SMEM),
            pl.BlockSpec(memory_space=pl.ANY),
        ],
        out_specs=pl.BlockSpec((tg, dw), lambda i: (i, 0)),
        out_shape=jax.ShapeDtypeStruct((p_rows, dw), h.dtype),
        scratch_shapes=[pltpu.SemaphoreType.DMA(())],
        compiler_params=_cparams(("arbitrary",)),
        name="moe_dispatch",
    )(src.reshape(p_rows // tg, 1, tg), h)


def _new_expert(te_ref, t):
    return (t == 0) | (te_ref[t] != te_ref[jnp.maximum(t - 1, 0)])


def _ffn_up_kernel(te_ref, tv_ref, x_ref, wg_ref, wu_ref, o_ref, wgb_ref, wub_ref):
    t = pl.program_id(1)

    @pl.when(_new_expert(te_ref, t))
    def _():
        wgb_ref[...] = wg_ref[0].astype(BF16)
        wub_ref[...] = wu_ref[0].astype(BF16)

    @pl.when(tv_ref[t] == 1)
    def _():
        lo, hi = _unpack_bf16_pairs(x_ref[...])
        lo, hi = lo.astype(BF16), hi.astype(BF16)
        half = lo.shape[1]
        g = _dot(lo, wgb_ref[:half, :]) + _dot(hi, wgb_ref[half:, :])
        u = _dot(lo, wub_ref[:half, :]) + _dot(hi, wub_ref[half:, :])
        o_ref[...] = (_silu(g) * u).astype(o_ref.dtype)

    @pl.when(tv_ref[t] == 0)
    def _():
        o_ref[...] = jnp.zeros_like(o_ref)


def _ffn_up(xs, w_gate, w_up, layer, tile_expert, tile_valid):
    p_rows, dw = xs.shape
    d, f = w_gate.shape[2], w_gate.shape[3]
    tm = ROUTE_TILE
    tf = _pick(f, (512, 256, 128))
    w_spec = pl.BlockSpec((None, 1, d, tf), lambda j, t, te, tv: (layer, te[t], 0, j))
    grid_spec = pltpu.PrefetchScalarGridSpec(
        num_scalar_prefetch=2,
        grid=(f // tf, p_rows // tm),
        in_specs=[pl.BlockSpec((tm, dw), lambda j, t, te, tv: (t, 0)), w_spec, w_spec],
        out_specs=pl.BlockSpec((tm, tf), lambda j, t, te, tv: (t, j)),
        scratch_shapes=[pltpu.VMEM((d, tf), BF16), pltpu.VMEM((d, tf), BF16)],
    )
    return pl.pallas_call(
        _ffn_up_kernel,
        grid_spec=grid_spec,
        out_shape=jax.ShapeDtypeStruct((p_rows, f), BF16),
        compiler_params=_cparams(("arbitrary", "arbitrary")),
        name="moe_ffn_up",
    )(tile_expert, tile_valid, xs, w_gate, w_up)


def _ffn_down_kernel(te_ref, tv_ref, h_ref, wd_ref, o_ref, wdb_ref):
    t = pl.program_id(1)

    @pl.when(_new_expert(te_ref, t))
    def _():
        wdb_ref[...] = wd_ref[0].astype(BF16)

    @pl.when(tv_ref[t] == 1)
    def _():
        o_ref[...] = _pack_bf16_pairs(_dot(h_ref[...], wdb_ref[...]))

    @pl.when(tv_ref[t] == 0)
    def _():
        o_ref[...] = jnp.zeros_like(o_ref)


def _ffn_down_tile(d):
    return _pick(d, (2048, 1024, 512, 256))


def _ffn_down(hmid, w_down, layer, tile_expert, tile_valid):
    p_rows, f = hmid.shape
    d = w_down.shape[3]
    tm = ROUTE_TILE
    tn = _ffn_down_tile(d)
    grid_spec = pltpu.PrefetchScalarGridSpec(
        num_scalar_prefetch=2,
        grid=(d // tn, p_rows // tm),
        in_specs=[
            pl.BlockSpec((tm, f), lambda j, t, te, tv: (t, 0)),
            pl.BlockSpec((None, 1, f, tn), lambda j, t, te, tv: (layer, te[t], 0, j)),
        ],
        out_specs=pl.BlockSpec((tm, tn // 2), lambda j, t, te, tv: (t, j)),
        scratch_shapes=[pltpu.VMEM((f, tn), BF16)],
    )
    return pl.pallas_call(
        _ffn_down_kernel,
        grid_spec=grid_spec,
        out_shape=jax.ShapeDtypeStruct((p_rows, d // 2), jnp.uint32),
        compiler_params=_cparams(("arbitrary", "arbitrary")),
        name="moe_ffn_down",
    )(tile_expert, tile_valid, hmid, w_down)


def _combine_kernel(*refs, final_norm, ctx_tiles, tn):
    pos_ref, x_ref, wt_ref, g_ref, fg_ref, ys_hbm = refs[:6]
    out_refs = refs[6:-2]
    buf_ref, sem = refs[-2:]
    tm = x_ref.shape[0]

    def start(r, c):
        for s in range(2):
            _row_copy(ys_hbm, pos_ref[0, s, r], buf_ref.at[s], r, sem).start(priority=s)
        return c

    def wait(r, c):
        for s in range(2):
            _row_copy(ys_hbm, 0, buf_ref.at[s], r, sem).wait()
        return c

    lax.fori_loop(0, tm, start, 0, unroll=4)
    lax.fori_loop(0, tm, wait, 0, unroll=4)

    d = x_ref.shape[1]
    rg = 2 * SUBLANES
    cw = min(2 * LANES, tn // 2)
    gate = g_ref[0, 0]
    lane = _iota2((1, wt_ref.shape[1]), 1)
    pieces = [(j * tn // 2 + p, j * tn + p, j * tn + tn // 2 + p)
              for j in range(d // tn) for p in range(0, tn // 2, cw)]

    def fill_normed(o_ref):
        w = wt_ref[...]
        w0 = jnp.sum(jnp.where(lane == 0, w, 0.0), axis=1, keepdims=True)
        w1 = jnp.sum(jnp.where(lane == 1, w, 0.0), axis=1, keepdims=True)
        parts = []
        for j in range(d // tn):
            seg = slice(j * tn // 2, (j + 1) * tn // 2)
            lo0, hi0 = _unpack_bf16_pairs(buf_ref[0, :, seg])
            lo1, hi1 = _unpack_bf16_pairs(buf_ref[1, :, seg])
            parts += [w0 * lo0 + w1 * lo1, w0 * hi0 + w1 * hi1]
        x = x_ref[...] + gate * jnp.concatenate(parts, axis=1)
        o_ref[...] = (x * lax.rsqrt(jnp.mean(x * x, axis=-1, keepdims=True) + EPS)) * fg_ref[...]

    def fill(o_ref):
        if final_norm:
            return fill_normed(o_ref)

        def rows_body(gi, c):
            rows = pl.ds(pl.multiple_of(gi * rg, rg), rg)
            w = wt_ref[rows, :]
            w0 = jnp.sum(jnp.where(lane == 0, w, 0.0), axis=1, keepdims=True)
            w1 = jnp.sum(jnp.where(lane == 1, w, 0.0), axis=1, keepdims=True)
            for p0, c_lo, c_hi in pieces:
                lo0, hi0 = _unpack_bf16_pairs(buf_ref[0, rows, p0:p0 + cw])
                lo1, hi1 = _unpack_bf16_pairs(buf_ref[1, rows, p0:p0 + cw])
                for col, moe in ((c_lo, w0 * lo0 + w1 * lo1), (c_hi, w0 * hi0 + w1 * hi1)):
                    o_ref[rows, col:col + cw] = x_ref[rows, col:col + cw] + gate[:, col:col + cw] * moe
            return c

        lax.fori_loop(0, tm // rg, rows_body, 0)

    if len(out_refs) == 1:
        fill(out_refs[0])
    else:
        is_ctx = pl.program_id(0) < ctx_tiles
        pl.when(is_ctx)(lambda: fill(out_refs[0]))
        pl.when(jnp.logical_not(is_ctx))(lambda: fill(out_refs[1]))


def _combine(x, ys, pos, wgt_cols, mod4, comp_gate, final_g, geom, final_norm, split_out):
    n, d = x.shape
    tm = pos.shape[2]
    ctx_tiles = geom["n_ctx"] // tm
    if split_out:
        out_specs = [pl.BlockSpec((tm, d), lambda i: (jnp.minimum(i, ctx_tiles - 1), 0)),
                     pl.BlockSpec((tm, d), lambda i: (jnp.maximum(i - ctx_tiles, 0), 0))]
        out_shape = [jax.ShapeDtypeStruct((geom["n_ctx"], d), F32),
                     jax.ShapeDtypeStruct((n - geom["n_ctx"], d), F32)]
    else:
        out_specs = pl.BlockSpec((tm, d), lambda i: (i, 0))
        out_shape = jax.ShapeDtypeStruct((n, d), F32)
    return pl.pallas_call(
        functools.partial(_combine_kernel, final_norm=final_norm, ctx_tiles=ctx_tiles, tn=_ffn_down_tile(d)),
        grid=(n // tm,),
        in_specs=[
            pl.BlockSpec((1, 2, tm), lambda i: (i, 0, 0), memory_space=pltpu.SMEM),
            pl.BlockSpec((tm, d), lambda i: (i, 0)),
            pl.BlockSpec((tm, wgt_cols.shape[1]), lambda i: (i, 0)),
            _mod_spec(d, tm, geom, comp_gate),
            pl.BlockSpec((1, d), lambda i: (0, 0)),
            pl.BlockSpec(memory_space=pl.ANY),
        ],
        out_specs=out_specs,
        out_shape=out_shape,
        scratch_shapes=[pltpu.VMEM((2, tm, d // 2), jnp.uint32), pltpu.SemaphoreType.DMA(())],
        compiler_params=_cparams(("arbitrary",)),
        name="moe_combine",
    )(pos, x, wgt_cols, mod4, final_g.reshape(1, d), ys)


def _route_plan(eidx, rank, counts, n_exp, combine_tile):
    n = eidx.shape[1]
    tile = ROUTE_TILE
    n_tiles = (2 * n) // tile + n_exp
    cnt = counts.astype(jnp.int32)
    padded = ((cnt + tile - 1) // tile) * tile
    e_ids = jnp.arange(n_exp, dtype=jnp.int32)
    ends = jnp.sum(jnp.where(e_ids[None, :] <= e_ids[:, None], padded[None, :], 0), axis=1)
    offs = ends - padded
    pos = jnp.sum(jnp.where(eidx[:, :, None] == e_ids, offs, 0), axis=-1) + rank
    tile_start = jnp.arange(n_tiles, dtype=jnp.int32) * tile
    tile_expert = jnp.minimum(jnp.sum((ends[None, :] <= tile_start[:, None]).astype(jnp.int32), axis=1), n_exp - 1)
    tile_valid = (tile_start < ends[-1]).astype(jnp.int32)
    tok = jnp.broadcast_to(jnp.arange(n, dtype=jnp.int32), (2, n))
    src = jnp.zeros((n_tiles * tile,), jnp.int32).at[pos.reshape(-1)].set(tok.reshape(-1))
    pos_tiles = pos.reshape(2, n // combine_tile, combine_tile).transpose(1, 0, 2)
    return pos_tiles, src, tile_expert, tile_valid


def _s5_params(lam_re, lam_im, log_step, b_re, b_im, c_re, c_im):
    n_dir, groups, p = lam_re.shape
    gch = b_re.shape[-1]
    gps = S5_SLAB // gch
    n_slab = groups // gps
    step = jnp.exp(log_step)[..., None]
    mag = jnp.exp(lam_re * step)
    a_re = mag * jnp.cos(lam_im * step)
    a_im = mag * jnp.sin(lam_im * step)
    den = lam_re * lam_re + lam_im * lam_im
    z_re = ((a_re - 1.0) * lam_re + a_im * lam_im) / den
    z_im = (a_im * lam_re - (a_re - 1.0) * lam_im) / den
    bb_re = z_re[..., None] * b_re - z_im[..., None] * b_im
    bb_im = z_re[..., None] * b_im + z_im[..., None] * b_re
    eye = jnp.eye(gps, dtype=F32)
    bbs = jnp.stack([bb_re, bb_im], axis=2).reshape(n_dir, n_slab, gps, 2, p, gch)
    bm = jnp.einsum("dsgrpc,gh->dsgcrhp", bbs, eye).reshape(n_dir, n_slab, S5_SLAB, 2 * gps * p)
    cs = jnp.stack([c_re, -c_im], axis=0).reshape(2, n_slab, gps, gch, p)
    cm = jnp.einsum("rsgcp,gh->srgphc", cs, eye).reshape(n_slab, 2 * gps * p, S5_SLAB)
    a = jnp.stack([a_re, a_im], axis=2).reshape(n_dir, n_slab, gps, 2, p).transpose(0, 1, 3, 2, 4)
    a = a.reshape(n_dir, n_slab, 1, 2 * gps * p)
    a8 = jnp.broadcast_to(a, (n_dir, n_slab, SUBLANES, 2 * gps * p))
    return bm.astype(BF16), cm.astype(BF16), a8


def _axial_rope(t, dh):
    rows = t // GRID_W
    row = jnp.repeat(jnp.arange(rows, dtype=F32), GRID_W)
    col = (jnp.arange(rows * GRID_W) % GRID_W).astype(F32)
    n_freq = dh // 4
    inv = ROPE_BASE ** (-jnp.arange(n_freq, dtype=F32) / n_freq)
    ang = jnp.concatenate([row[:, None] * inv, col[:, None] * inv], axis=-1)
    return jnp.cos(ang), jnp.sin(ang)


def _to_time_major(u, nb, t_len, bp):
    u = u.reshape(nb, t_len, -1).transpose(1, 0, 2)
    if bp != nb:
        u = jnp.pad(u, ((0, 0), (0, bp - nb), (0, 0)))
    return u.reshape(t_len * bp, -1)


def _from_time_major(y, nb, t_len, bp):
    return y.reshape(t_len, bp, -1)[:, :nb].transpose(1, 0, 2).reshape(nb * t_len, -1)


def _s5_state_to_slabs(s_re, s_im, n_slab, bp):
    nb = s_re.shape[0]
    st = jnp.concatenate([s_re.reshape(nb, 2, n_slab, -1), s_im.reshape(nb, 2, n_slab, -1)], axis=-1)
    st = st.transpose(1, 2, 0, 3)
    return jnp.pad(st, ((0, 0), (0, 0), (0, bp - nb), (0, 0)))


def _s5_slabs_to_state(xf, nb, groups, p):
    half = xf.shape[-1] // 2
    re = xf[:, :, :nb, :half].transpose(2, 0, 1, 3).reshape(nb, 2, groups, p)
    im = xf[:, :, :nb, half:].transpose(2, 0, 1, 3).reshape(nb, 2, groups, p)
    return re, im


def _round_up(x, m):
    return (x + m - 1) // m * m


def kernel(x_prompt, x_sample, c, c_ctx, state_s5_re, state_s5_im, state_gla, state_ml_c, state_ml_n, state_ml_m, state_ret, w_ada, b_ada, norm1_g, norm2_g, w_in, s5_lambda_re, s5_lambda_im, s5_log_step, s5_b_re, s5_b_im, s5_c_re, s5_c_im, s5_d, s5_w_glu, gla_w_a, gla_b_a, gla_norm_g, ml_i_bias, ml_f_bias, ml_norm_g, ret_decay_logit, ret_norm_g, w_branch, w_out, w_router, router_bias, w_exp_gate, w_exp_up, w_exp_down, final_g):
    batch, seq, d = x_prompt.shape
    dec_batch, dec_seq, _ = x_sample.shape
    depth = w_in.shape[0]
    mix = s5_d.shape[-1]
    n_exp = w_router.shape[1]
    gla_heads = state_gla.shape[3]
    ml_heads = ml_i_bias.shape[-1]
    ret_heads = ret_decay_logit.shape[-1]
    gla_rank = gla_w_a.shape[2]
    groups, p_state = s5_lambda_re.shape[2], s5_lambda_re.shape[3]
    n_ctx, n_lat = batch * seq, dec_batch * dec_seq
    n = n_ctx + n_lat
    assert dec_batch + 1 <= SUBLANES and n_ctx % dec_seq == 0 and dec_seq % seq == 0
    geom = dict(batch=batch, seq=seq, dec_batch=dec_batch, dec_seq=dec_seq, n_ctx=n_ctx, n=n, ctx_row=dec_batch,
                mix=mix, gla_heads=gla_heads, ml_heads=ml_heads, ret_heads=ret_heads)
    n_slab = mix // S5_SLAB
    bp_ctx, bp_lat = _round_up(batch, SUBLANES), _round_up(dec_batch, SUBLANES)

    half = mix // 2
    widths = [("s5_u", mix), ("gla_q", half), ("gla_k", half), ("gla_v", mix), ("gla_g", mix), ("gla_r", gla_rank),
              ("ml_q", mix), ("ml_k", mix), ("ml_v", mix), ("ml_o", mix), ("ml_if", 4 * ml_heads),
              ("ret_q", mix), ("ret_k", mix), ("ret_v", mix), ("ret_g", mix), ("merge", 4 * d)]
    src_off, o = {}, 0
    for name, w in widths:
        src_off[name] = (o, w)
        o += w
    main_names = [nm for nm, _ in widths if nm not in ("gla_r", "ml_if", "merge")]
    cols, o = {}, 0
    for nm in main_names:
        cols[nm] = o
        o += src_off[nm][1]

    repack_tn = _pick(math.gcd(half, 4 * d), (512, 256, 128))
    main_starts = [src_off[nm][0] + off for nm in main_names for off in range(0, src_off[nm][1], repack_tn)]
    merge_starts = [src_off["merge"][0] + off for off in range(0, 4 * d, repack_tn)]
    gate_starts = [src_off["gla_r"][0], src_off["ml_if"][0]]
    w_in_t = jnp.swapaxes(w_in, 1, 2)

    x = (x_prompt.reshape(n_ctx, d), x_sample.reshape(n_lat, d))
    c8 = jnp.zeros((SUBLANES, d), F32).at[:dec_batch].set(c).at[dec_batch].set(c_ctx)
    mod = _ada(c8, w_ada, b_ada)
    rope_tabs = _axial_rope(dec_seq, mix // ret_heads)
    w_router_t = w_router.T

    ctx_states = []
    for l in range(depth):
        mod4 = mod[l].reshape(SUBLANES, N_MOD, 1, d)
        w_main = _repack_proj(w_in_t, l, main_starts, repack_tn)
        w_merge = _repack_proj(w_in_t, l, merge_starts, repack_tn)
        w_gate = _repack_proj(w_in_t, l, gate_starts, LANES)

        h = _norm_mod(x, norm1_g[l], mod4, 0, 1, geom, BF16)
        z = _matmul(h, w_main, BF16, _pick(n, (1024, 512, 256)), _pick(w_main.shape[1], (512, 256, 128)), "in_proj")
        zg = _matmul(h, w_gate, F32, _pick(n, (1024, 512, 256)), 2 * LANES, "gate_proj")
        zgt = zg[:, LANES:LANES + 4 * ml_heads].T

        bm, cm, a8 = _s5_params(s5_lambda_re[l], s5_lambda_im[l], s5_log_step[l], s5_b_re[l], s5_b_im[l],
                                s5_c_re[l], s5_c_im[l])
        dvec = s5_d[l].reshape(1, mix)
        u = z[:, :mix]
        y_c, xf = _s5_mixer(_to_time_major(u[:n_ctx], batch, seq, bp_ctx), bm, cm, a8, dvec, None, seq, bp_ctx, True)
        x0 = _s5_state_to_slabs(state_s5_re[:, l], state_s5_im[:, l], n_slab, bp_lat)
        y_l, _ = _s5_mixer(_to_time_major(u[n_ctx:], dec_batch, dec_seq, bp_lat), bm, cm, a8, dvec, x0, dec_seq,
                           bp_lat, False)
        w_glu = s5_w_glu[l].astype(BF16)
        y_s5 = (_glu(_from_time_major(y_c, batch, seq, bp_ctx), w_glu),
                _glu(_from_time_major(y_l, dec_batch, dec_seq, bp_lat), w_glu))
        s5_re_l, s5_im_l = _s5_slabs_to_state(xf, batch, groups, p_state)

        dk = half // gla_heads
        wa = gla_w_a[l].reshape(2, gla_rank, gla_heads, dk).transpose(0, 2, 1, 3)
        wa = jnp.pad(wa, ((0, 0), (0, 0), (0, LANES - gla_rank), (0, 0)))
        ba = gla_b_a[l].reshape(2, gla_heads, 1, dk)
        gp = dict(wa=wa, ba=ba, normg=gla_norm_g[l].reshape(1, mix))
        y_gla_c, gla_l = _gla_mixer(z, zg, gp, None, geom, cols, True)
        y_gla_l, _ = _gla_mixer(z, zg, gp, state_gla[:, l], geom, cols, False)

        ml_bias = jnp.stack([ml_i_bias[l], ml_f_bias[l]], axis=0).reshape(-1)
        ml_ng = ml_norm_g[l].reshape(1, mix)
        y_ml_c, ml_l = _ml_mixer(z, zg, 1, zgt, ml_bias, ml_ng, None, l, depth, 0, geom, cols, True)
        ml_init = dict(c=state_ml_c[:, l], n=state_ml_n[:, l][:, :, :, None, :], m=state_ml_m.reshape(-1))
        y_ml_l, _ = _ml_mixer(z, zg, 1, zgt, ml_bias, ml_ng, ml_init, l, depth, 0, geom, cols, False)

        lg = jax.nn.log_sigmoid(ret_decay_logit[l]).reshape(-1)
        ret_ng = ret_norm_g[l].reshape(1, mix)
        y_ret_c, ret_l = _ret_mixer(z, lg, ret_ng, None, None, geom, cols, True)
        y_ret_l, _ = _ret_mixer(z, lg, ret_ng, rope_tabs, state_ret[:, l], geom, cols, False)

        y_pairs = [y_s5, (y_gla_c, y_gla_l), (y_ml_c, y_ml_l), (y_ret_c, y_ret_l)]
        merged = _merge(h, y_pairs, w_merge, w_branch, l, geom)
        x = _outproj(merged, w_out, l, x, mod4, 2, geom)
        ctx_states.append((s5_re_l, s5_im_l, gla_l, ml_l[0], ml_l[1][:, :, :, 0, :], ml_l[2][:, :, :, 0, 0], ret_l))

        h2, eidx, wgt, rank, counts = _norm_router(x, norm2_g[l], mod4, 3, 4, w_router_t, router_bias, geom)
        last = l == depth - 1
        combine_tile = ROUTE_TILE if last else _pick(math.gcd(n_ctx, n_lat), (2 * ROUTE_TILE, ROUTE_TILE))
        pos_tiles, src, tile_expert, tile_valid = _route_plan(eidx, rank, counts[:, 0], n_exp, combine_tile)
        xs = _dispatch(h2, src)
        hmid = _ffn_up(xs, w_exp_gate, w_exp_up, l, tile_expert, tile_valid)
        ys = _ffn_down(hmid, w_exp_down, l, tile_expert, tile_valid)
        x = _combine(x, ys, pos_tiles, wgt.T, mod4, 5, final_g, geom, final_norm=last, split_out=last)

    y_prompt = x[0].reshape(batch, seq, d)
    y_sample = x[1].reshape(dec_batch, dec_seq, d)
    stacked = [jnp.stack([st[i] for st in ctx_states], axis=1) for i in range(7)]
    return (y_prompt, y_sample, *stacked)
```

```python
import functools
import math

import jax
import jax.numpy as jnp
from jax import lax
from jax.experimental import pallas as pl
from jax.experimental.pallas import tpu as pltpu

F32 = jnp.float32
BF16 = jnp.bfloat16

EPS = 1e-6
GRID_W = 64
ROPE_BASE = 10000.0
GLA_TAU = 16.0
N_EXPERT_GROUPS = 4
N_MOD = 6

LANES = 128
SUBLANES = 8
S5_SLAB = LANES
GLA_CHUNK = 64
SEQ_CHUNK = 256
VMEM_LIMIT_BYTES = 56 * 1024 * 1024
ROUTE_TILE = 256


def _cparams(sem):
    return pltpu.CompilerParams(dimension_semantics=sem, vmem_limit_bytes=VMEM_LIMIT_BYTES)


def _pick(n, cands):
    for c in cands:
        if c <= n and n % c == 0:
            return c
    return n


def _nt(a, b):
    return lax.dot_general(a, b, (((1,), (1,)), ((), ())), preferred_element_type=F32)


def _tn(a, b):
    return lax.dot_general(a, b, (((0,), (0,)), ((), ())), preferred_element_type=F32)


def _dot(a, b):
    return jnp.dot(a, b, preferred_element_type=F32)


def _split_bf16(x):
    hi = x.astype(BF16)
    return hi, (x - hi.astype(F32)).astype(BF16)


def _dot3(a, b, mm):
    ah, al = _split_bf16(a)
    bh, bl = _split_bf16(b)
    return mm(ah, bh) + (mm(ah, bl) + mm(al, bh))


def _log_sigmoid(x):
    return jnp.minimum(x, 0.0) - jnp.log1p(jnp.exp(-jnp.abs(x)))


def _silu(x):
    return x * jax.nn.sigmoid(x)


def _iota2(shape, dim):
    return lax.broadcasted_iota(jnp.int32, shape, dim)


HI16 = 0xFFFF0000


def _pack_bf16_pairs(x):
    c = x.shape[1] // 2
    bits = lax.bitcast_convert_type(x.astype(BF16).astype(F32), jnp.uint32)
    return (bits[:, :c] >> 16) | (bits[:, c:] & jnp.uint32(HI16))


def _unpack_bf16_pairs(w):
    lo = lax.bitcast_convert_type(w << 16, F32)
    hi = lax.bitcast_convert_type(w & jnp.uint32(HI16), F32)
    return lo, hi


def _ada_kernel(c_ref, w_ref, b_ref, o_ref):
    s = _silu(c_ref[...]).astype(BF16)
    o_ref[0] = _dot(s, w_ref[0].astype(BF16)) + b_ref[0]


def _ada(c8, w_ada, b_ada):
    depth, d, n = w_ada.shape
    tn = _pick(n, (512, 256, 128))
    return pl.pallas_call(
        _ada_kernel,
        grid=(depth, n // tn),
        in_specs=[
            pl.BlockSpec((SUBLANES, d), lambda l, j: (0, 0)),
            pl.BlockSpec((1, d, tn), lambda l, j: (l, 0, j)),
            pl.BlockSpec((1, 1, tn), lambda l, j: (l, 0, j)),
        ],
        out_specs=pl.BlockSpec((1, SUBLANES, tn), lambda l, j: (l, 0, j)),
        out_shape=jax.ShapeDtypeStruct((depth, SUBLANES, n), F32),
        compiler_params=_cparams(("arbitrary", "arbitrary")),
        name="ada_mod",
    )(c8, w_ada, b_ada.reshape(depth, 1, n))


def _mod_spec(d, tm, geom, comp):
    n_ctx, dec_seq, ctx_row = geom["n_ctx"], geom["dec_seq"], geom["ctx_row"]

    def index_map(i, *_):
        start = i * tm
        row = jnp.where(start < n_ctx, ctx_row, (start - n_ctx) // dec_seq)
        return (row, comp, 0, 0)

    return pl.BlockSpec((1, 1, 1, d), index_map)


def _row_specs(x, block, ctx_tiles, col=lambda *g: 0):
    if not isinstance(x, tuple):
        return [pl.BlockSpec(block, lambda i, *g: (i, col(i, *g)))], [x]
    return ([pl.BlockSpec(block, lambda i, *g: (jnp.minimum(i, ctx_tiles - 1),
                                                jnp.where(i < ctx_tiles, col(i, *g), 0))),
             pl.BlockSpec(block, lambda i, *g: (jnp.maximum(i - ctx_tiles, 0),
                                                jnp.where(i >= ctx_tiles, col(i, *g), 0)))], list(x))


def _read_rows(refs, ctx_tiles):
    if len(refs) == 1:
        return refs[0][...]
    return jnp.where(pl.program_id(0) < ctx_tiles, refs[0][...], refs[1][...])


def _norm_kernel(*refs, n_x, ctx_tiles):
    g_ref, sh_ref, sc_ref, o_ref = refs[n_x:]
    x = _read_rows(refs[:n_x], ctx_tiles)
    y = x * lax.rsqrt(jnp.mean(x * x, axis=-1, keepdims=True) + EPS)
    h = (y * g_ref[...]) * (1.0 + sc_ref[0, 0]) + sh_ref[0, 0]
    o_ref[...] = h.astype(o_ref.dtype)


def _norm_mod(x, g, mod4, comp_shift, comp_scale, geom, out_dtype):
    d = g.shape[0]
    n = geom["n"]
    tm = 256
    ctx_tiles = geom["n_ctx"] // tm
    x_specs, x_args = _row_specs(x, (tm, d), ctx_tiles)
    return pl.pallas_call(
        functools.partial(_norm_kernel, n_x=len(x_args), ctx_tiles=ctx_tiles),
        grid=(n // tm,),
        in_specs=x_specs + [
            pl.BlockSpec((1, d), lambda i: (0, 0)),
            _mod_spec(d, tm, geom, comp_shift),
            _mod_spec(d, tm, geom, comp_scale),
        ],
        out_specs=pl.BlockSpec((tm, d), lambda i: (i, 0)),
        out_shape=jax.ShapeDtypeStruct((n, d), out_dtype),
        compiler_params=_cparams(("arbitrary",)),
        name="norm_mod",
    )(*x_args, g.reshape(1, d), mod4, mod4)


def _mm_kernel(x_ref, w_ref, o_ref):
    x = x_ref[...]
    o_ref[...] = _dot(x, w_ref[...].astype(x.dtype)).astype(o_ref.dtype)


def _matmul(x, w, out_dtype, tm, tn, name):
    m, k = x.shape
    n = w.shape[1]
    return pl.pallas_call(
        _mm_kernel,
        grid=(m // tm, n // tn),
        in_specs=[
            pl.BlockSpec((tm, k), lambda i, j: (i, 0)),
            pl.BlockSpec((k, tn), lambda i, j: (0, j)),
        ],
        out_specs=pl.BlockSpec((tm, tn), lambda i, j: (i, j)),
        out_shape=jax.ShapeDtypeStruct((m, n), out_dtype),
        compiler_params=_cparams(("arbitrary", "arbitrary")),
        name=name,
    )(x, w)


def _repack_kernel(starts_ref, wt_ref, o_ref):
    del starts_ref
    o_ref[...] = wt_ref[...].T.astype(o_ref.dtype)


def _repack_proj(w_in_t, layer, starts, tn):
    d = w_in_t.shape[2]
    grid_spec = pltpu.PrefetchScalarGridSpec(
        num_scalar_prefetch=1,
        grid=(len(starts),),
        in_specs=[pl.BlockSpec((None, pl.Element(tn), pl.Element(d)),
                               lambda j, st: (layer, pl.multiple_of(st[j], 2 * SUBLANES), 0))],
        out_specs=pl.BlockSpec((d, tn), lambda j, st: (0, j)),
    )
    return pl.pallas_call(
        _repack_kernel,
        grid_spec=grid_spec,
        out_shape=jax.ShapeDtypeStruct((d, len(starts) * tn), BF16),
        compiler_params=_cparams(("arbitrary",)),
        name="repack_proj",
    )(jnp.asarray(starts, jnp.int32), w_in_t)


def _s5_kernel(*refs, t_len, bp, tc, has_init, want_final):
    it = iter(refs)
    u_ref, bm_ref, cm_ref, a_ref, d_ref = (next(it) for _ in range(5))
    x0_ref = next(it) if has_init else None
    y_ref = next(it)
    xf_ref = next(it) if want_final else None
    bu_refs = (next(it), next(it))
    y_refs = (next(it), next(it))
    st_refs = (next(it), next(it))

    nc = t_len // tc
    rc = tc * bp
    half = bu_refs[0].shape[1] // 2
    for d in (0, 1):
        st_refs[d][...] = x0_ref[d, 0] if has_init else jnp.zeros_like(st_refs[d])
    a_parts = [(a_ref[d, 0, :, :half], a_ref[d, 0, :, half:]) for d in (0, 1)]

    def chunk_body(c, carry):
        r0s = (pl.multiple_of(c * rc, rc), pl.multiple_of((nc - 1 - c) * rc, rc))
        for d in (0, 1):
            bu_refs[d][...] = _dot(u_ref[pl.ds(r0s[d], rc), :], bm_ref[d, 0])
        for rt in range(bp // SUBLANES):
            rows = slice(rt * SUBLANES, (rt + 1) * SUBLANES)

            def step(t, s, rt=rt):
                out = []
                for d in (0, 1):
                    sr, si = s[2 * d], s[2 * d + 1]
                    ar, ai = a_parts[d]
                    tt = t if d == 0 else tc - 1 - t
                    row = pl.multiple_of(tt * bp + rt * SUBLANES, SUBLANES)
                    b = bu_refs[d][pl.ds(row, SUBLANES), :]
                    xr = ar * sr - ai * si + b[:, :half]
                    xi = ar * si + ai * sr + b[:, half:]
                    bu_refs[d][pl.ds(row, SUBLANES), :] = jnp.concatenate([xr, xi], axis=-1)
                    out += [xr, xi]
                return tuple(out)

            init = tuple(st_refs[d][rows, sl] for d in (0, 1) for sl in (slice(0, half), slice(half, 2 * half)))
            fin = lax.fori_loop(0, tc, step, init, unroll=math.gcd(tc, 2))
            for d in (0, 1):
                st_refs[d][rows, :] = jnp.concatenate([fin[2 * d], fin[2 * d + 1]], axis=-1)
        for d in (0, 1):
            y_refs[d][pl.ds(r0s[d], rc), :] = _dot(bu_refs[d][...].astype(BF16), cm_ref[0])
        return carry

    lax.fori_loop(0, nc, chunk_body, 0)
    if want_final:
        for d in (0, 1):
            xf_ref[d, 0] = st_refs[d][...]
    y = y_refs[0][...] + y_refs[1][...] + d_ref[...] * u_ref[...].astype(F32)
    y_ref[...] = jax.nn.gelu(y).astype(y_ref.dtype)


def _s5_mixer(u_tm, bm, cm, a8, dvec, x0, t_len, bp, want_final):
    rows, mix = u_tm.shape
    n_slab = mix // S5_SLAB
    two_half = bm.shape[-1]
    tc = _pick(t_len, tuple(max(1, 1024 // bp) >> s for s in range(6)))
    has_init = x0 is not None
    in_specs = [
        pl.BlockSpec((rows, S5_SLAB), lambda s: (0, s)),
        pl.BlockSpec((2, 1, S5_SLAB, two_half), lambda s: (0, s, 0, 0)),
        pl.BlockSpec((1, two_half, S5_SLAB), lambda s: (s, 0, 0)),
        pl.BlockSpec((2, 1, SUBLANES, two_half), lambda s: (0, s, 0, 0)),
        pl.BlockSpec((1, S5_SLAB), lambda s: (0, s)),
    ]
    args = [u_tm, bm, cm, a8, dvec]
    if has_init:
        in_specs.append(pl.BlockSpec((2, 1, bp, two_half), lambda s: (0, s, 0, 0)))
        args.append(x0)
    out_specs = [pl.BlockSpec((rows, S5_SLAB), lambda s: (0, s))]
    out_shape = [jax.ShapeDtypeStruct((rows, mix), BF16)]
    if want_final:
        out_specs.append(pl.BlockSpec((2, 1, bp, two_half), lambda s: (0, s, 0, 0)))
        out_shape.append(jax.ShapeDtypeStruct((2, n_slab, bp, two_half), F32))
    res = pl.pallas_call(
        functools.partial(_s5_kernel, t_len=t_len, bp=bp, tc=tc, has_init=has_init, want_final=want_final),
        grid=(n_slab,),
        in_specs=in_specs,
        out_specs=out_specs,
        out_shape=out_shape,
        scratch_shapes=[pltpu.VMEM((tc * bp, two_half), F32)] * 2 + [pltpu.VMEM((rows, S5_SLAB), F32)] * 2
        + [pltpu.VMEM((bp, two_half), F32)] * 2,
        compiler_params=_cparams(("arbitrary",)),
        name="s5_ctx" if want_final else "s5_lat",
    )(*args)
    return (res[0], res[1]) if want_final else (res[0], None)


def _glu_kernel(y_ref, w_ref, o_ref):
    y = y_ref[...]
    z = _dot(y, w_ref[...])
    o_ref[...] = (y.astype(F32) * jax.nn.sigmoid(z)).astype(o_ref.dtype)


def _glu(y, w):
    n, mix = y.shape
    tm = _pick(n, (512, 256, 128))
    return pl.pallas_call(
        _glu_kernel,
        grid=(n // tm,),
        in_specs=[pl.BlockSpec((tm, mix), lambda i: (i, 0)), pl.BlockSpec((mix, mix), lambda i: (0, 0))],
        out_specs=pl.BlockSpec((tm, mix), lambda i: (i, 0)),
        out_shape=jax.ShapeDtypeStruct((n, mix), BF16),
        compiler_params=_cparams(("arbitrary",)),
        name="s5_glu",
    )(y, w)


def _head_norm_gate(o, normg_ref, gate, act):
    of = o * lax.rsqrt(jnp.mean(o * o, axis=-1, keepdims=True) + EPS)
    return of * normg_ref[...] * act(gate)


def _seq_geometry(geom, ctx):
    if ctx:
        return geom["batch"], geom["seq"], 0
    return geom["dec_batch"], geom["dec_seq"], geom["n_ctx"] // geom["dec_seq"]


def _col_spec(t_len, width, col0, blk0):
    base = col0 // width
    return pl.BlockSpec((t_len, width), lambda b, h: (blk0 + b, base + h))


def _launch_mixer(body, name, geom, ctx, width, in_specs, args, extra_out_specs, extra_out_shapes, scratch, **kw):
    nb, t_len, _ = _seq_geometry(geom, ctx)
    heads = geom["mix"] // width
    out_specs = [pl.BlockSpec((t_len, width), lambda b, h: (b, h))] + list(extra_out_specs)
    out_shapes = [jax.ShapeDtypeStruct((nb * t_len, geom["mix"]), BF16)] + list(extra_out_shapes)
    return pl.pallas_call(
        functools.partial(body, **kw),
        grid=(nb, heads),
        in_specs=list(in_specs),
        out_specs=out_specs,
        out_shape=out_shapes,
        scratch_shapes=scratch,
        compiler_params=_cparams(("arbitrary", "arbitrary")),
        name=name,
    )(*args)


def _gla_kernel(*refs, t_len, dk, has_init, want_final):
    it = iter(refs)
    q_ref, k_ref, v_ref, g_ref, zg_ref, wa_ref, ba_ref, normg_ref = (next(it) for _ in range(8))
    s0_ref = next(it) if has_init else None
    y_ref = next(it)
    sf_ref = next(it) if want_final else None
    of_ref, ob_ref, cumf_ref, cumb_ref, stf_ref, stb_ref = (next(it) for _ in range(6))

    cl = min(GLA_CHUNK, t_len)
    nc = t_len // cl
    blk = min(SEQ_CHUNK, t_len)
    scale = dk ** -0.5
    dirs = ((of_ref, cumf_ref, stf_ref), (ob_ref, cumb_ref, stb_ref))

    zg = zg_ref[...]
    ti = _iota2((blk, blk), 0)
    si = _iota2((blk, blk), 1)
    same_chunk = (ti // cl) == (si // cl)
    for d, (_, cum_ref, st_ref) in enumerate(dirs):
        la = _log_sigmoid(_dot3(zg, wa_ref[d, 0], _dot) + ba_ref[d, 0]) / GLA_TAU
        tri = jnp.where(same_chunk & ((si <= ti) if d == 0 else (si >= ti)), 1.0, 0.0).astype(BF16)
        la_hi, la_lo = _split_bf16(la)
        for p in range(t_len // blk):
            rows = slice(p * blk, (p + 1) * blk)
            cum_ref[rows, :] = _dot(tri, la_hi[rows]) + _dot(tri, la_lo[rows])
        st_ref[...] = s0_ref[0, d, 0].T if has_init else jnp.zeros_like(st_ref)

    tl = _iota2((cl, cl), 0)
    sl = _iota2((cl, cl), 1)
    keeps = (sl <= tl, sl >= tl)

    def chunk_body(c, carry):
        for d, (o_ref, cum_ref, st_ref) in enumerate(dirs):
            cc = c if d == 0 else nc - 1 - c
            r0 = pl.multiple_of(cc * cl, cl)
            cum = cum_ref[pl.ds(r0, cl), :]
            tot = cum[cl - 1:cl, :] if d == 0 else cum[0:1, :]
            q = q_ref[pl.ds(r0, cl), :].astype(F32) * scale
            k = k_ref[pl.ds(r0, cl), :].astype(F32)
            v = v_ref[pl.ds(r0, cl), :]
            qd = (q * jnp.exp(cum)).astype(BF16)
            kd = (k * jnp.exp(-cum)).astype(BF16)
            sc = jnp.where(keeps[d], _nt(qd, kd), 0.0)
            st = st_ref[...]
            o_ref[pl.ds(r0, cl), :] = _nt(qd, st.astype(BF16)) + _dot(sc.astype(BF16), v)
            kl = (k * jnp.exp(tot - cum)).astype(BF16)
            st_ref[...] = st * jnp.exp(tot) + _tn(v, kl)
        return carry

    lax.fori_loop(0, nc, chunk_body, 0)
    if want_final:
        sf_ref[0, 0, 0] = stf_ref[...].T
        sf_ref[0, 1, 0] = stb_ref[...].T
    y = _head_norm_gate(of_ref[...] + ob_ref[...], normg_ref, g_ref[...].astype(F32), _silu)
    y_ref[...] = y.astype(y_ref.dtype)


def _gla_mixer(z, zg, gp, s0, geom, cols, ctx):
    nb, t_len, blk0 = _seq_geometry(geom, ctx)
    mix, heads = geom["mix"], geom["gla_heads"]
    dk, dv = mix // 2 // heads, mix // heads
    has_init, want_final = s0 is not None, ctx
    in_specs = [
        _col_spec(t_len, dk, cols["gla_q"], blk0),
        _col_spec(t_len, dk, cols["gla_k"], blk0),
        _col_spec(t_len, dv, cols["gla_v"], blk0),
        _col_spec(t_len, dv, cols["gla_g"], blk0),
        pl.BlockSpec((t_len, LANES), lambda b, h: (blk0 + b, 0)),
        pl.BlockSpec((2, 1, LANES, dk), lambda b, h: (0, h, 0, 0)),
        pl.BlockSpec((2, 1, 1, dk), lambda b, h: (0, h, 0, 0)),
        pl.BlockSpec((1, dv), lambda b, h: (0, h)),
    ]
    args = [z, z, z, z, zg, gp["wa"], gp["ba"], gp["normg"]]
    if has_init:
        in_specs.append(pl.BlockSpec((1, 2, 1, dk, dv), lambda b, h: (b, 0, h, 0, 0)))
        args.append(s0)
    x_specs, x_shapes = [], []
    if want_final:
        x_specs.append(pl.BlockSpec((1, 2, 1, dk, dv), lambda b, h: (b, 0, h, 0, 0)))
        x_shapes.append(jax.ShapeDtypeStruct((nb, 2, heads, dk, dv), F32))
    res = _launch_mixer(_gla_kernel, "gla_ctx" if ctx else "gla_lat", geom, ctx, dv, in_specs, args,
                        x_specs, x_shapes,
                        [pltpu.VMEM((t_len, dv), F32)] * 2 + [pltpu.VMEM((t_len, dk), F32)] * 2
                        + [pltpu.VMEM((dv, dk), F32)] * 2,
                        t_len=t_len, dk=dk, has_init=has_init, want_final=want_final)
    return (res[0], res[1]) if want_final else (res[0], None)


def _ret_kernel(*refs, t_len, dh, has_init, want_final, rope):
    it = iter(refs)
    lg_ref = next(it)
    q_ref, k_ref, v_ref, g_ref, normg_ref = (next(it) for _ in range(5))
    cos_ref, sin_ref = (next(it), next(it)) if rope else (None, None)
    s0_ref = next(it) if has_init else None
    y_ref = next(it)
    sf_ref = next(it) if want_final else None
    o_ref = next(it)

    h = pl.program_id(1)
    n_heads = lg_ref.shape[0] // 2
    lgf = lg_ref[h]
    lgb = lg_ref[n_heads + h]
    cl = min(SEQ_CHUNK, t_len)
    nc = t_len // cl
    scale = dh ** -0.5
    half = dh // 2

    q = q_ref[...].astype(F32)
    k = k_ref[...].astype(F32) * scale
    if rope:
        cos = cos_ref[...]
        sin = sin_ref[...]

        def rot(x):
            x1, x2 = x[:, :half], x[:, half:]
            return jnp.concatenate([x1 * cos - x2 * sin, x1 * sin + x2 * cos], axis=-1)

        q, k = rot(q), rot(k)
    qb = q.astype(BF16)
    kb = k.astype(BF16)

    dt = (_iota2((cl, cl), 0) - _iota2((cl, cl), 1)).astype(F32)
    decay = (jnp.where(dt >= 0, jnp.exp(lgf * jnp.maximum(dt, 0.0)), 0.0)
             + jnp.where(dt <= 0, jnp.exp(lgb * jnp.maximum(-dt, 0.0)), 0.0))
    tcol = _iota2((cl, 1), 0).astype(F32)

    def rows(c):
        return slice(c * cl, (c + 1) * cl)

    s_f = s0_ref[0, 0, 0] if has_init else None
    for c in range(nc):
        qc, kc, vc = qb[rows(c)], kb[rows(c)], v_ref[rows(c), :]
        o = _dot((_nt(qc, kc) * decay).astype(BF16), vc)
        if s_f is not None:
            o = o + _dot((q[rows(c)] * jnp.exp(lgf * (tcol + 1.0))).astype(BF16), s_f.astype(BF16))
        o_ref[rows(c), :] = o
        if c < nc - 1 or want_final:
            upd = _tn((k[rows(c)] * jnp.exp(lgf * (cl - 1.0 - tcol))).astype(BF16), vc)
            s_f = upd if s_f is None else jnp.exp(lgf * cl) * s_f + upd
    if want_final:
        sf_ref[0, 0, 0] = s_f
    s_b = s0_ref[0, 1, 0] if has_init else None
    for c in range(nc - 1, -1, -1):
        vc = v_ref[rows(c), :]
        if s_b is not None:
            o_ref[rows(c), :] += _dot((q[rows(c)] * jnp.exp(lgb * (cl - tcol))).astype(BF16), s_b.astype(BF16))
        if c > 0 or want_final:
            upd = _tn((k[rows(c)] * jnp.exp(lgb * tcol)).astype(BF16), vc)
            s_b = upd if s_b is None else jnp.exp(lgb * cl) * s_b + upd
    if want_final:
        sf_ref[0, 1, 0] = s_b
    y = _head_norm_gate(o_ref[...], normg_ref, g_ref[...].astype(F32), _silu)
    y_ref[...] = y.astype(y_ref.dtype)


def _ret_mixer(z, lg, normg, rope_tabs, s0, geom, cols, ctx):
    nb, t_len, blk0 = _seq_geometry(geom, ctx)
    mix, heads = geom["mix"], geom["ret_heads"]
    dh = mix // heads
    has_init, want_final, rope = s0 is not None, ctx, rope_tabs is not None
    in_specs = [
        pl.BlockSpec(memory_space=pltpu.SMEM),
        _col_spec(t_len, dh, cols["ret_q"], blk0),
        _col_spec(t_len, dh, cols["ret_k"], blk0),
        _col_spec(t_len, dh, cols["ret_v"], blk0),
        _col_spec(t_len, dh, cols["ret_g"], blk0),
        pl.BlockSpec((1, dh), lambda b, h: (0, h)),
    ]
    args = [lg, z, z, z, z, normg]
    if rope:
        in_specs += [pl.BlockSpec((t_len, dh // 2), lambda b, h: (0, 0))] * 2
        args += list(rope_tabs)
    if has_init:
        in_specs.append(pl.BlockSpec((1, 2, 1, dh, dh), lambda b, h: (b, 0, h, 0, 0)))
        args.append(s0)
    x_specs, x_shapes = [], []
    if want_final:
        x_specs.append(pl.BlockSpec((1, 2, 1, dh, dh), lambda b, h: (b, 0, h, 0, 0)))
        x_shapes.append(jax.ShapeDtypeStruct((nb, 2, heads, dh, dh), F32))
    res = _launch_mixer(_ret_kernel, "ret_ctx" if ctx else "ret_lat", geom, ctx, dh, in_specs, args,
                        x_specs, x_shapes, [pltpu.VMEM((t_len, dh), F32)],
                        t_len=t_len, dh=dh, has_init=has_init, want_final=want_final, rope=rope)
    return (res[0], res[1]) if want_final else (res[0], None)


def _ml_kernel(*refs, t_len, dh, n_heads, layer, depth, gate_lane0, has_init, want_final):
    it = iter(refs)
    bias_ref = next(it)
    m0_ref = next(it) if has_init else None
    q_ref, k_ref, v_ref, og_ref, zg_ref, zgt_ref, normg_ref = (next(it) for _ in range(7))
    c0_ref, n0_ref = (next(it), next(it)) if has_init else (None, None)
    y_ref = next(it)
    cf_ref, nf_ref, mf_ref = (next(it), next(it), next(it)) if want_final else (None, None, None)
    o_ref = next(it)

    b = pl.program_id(0)
    h = pl.program_id(1)
    cl = min(SEQ_CHUNK, t_len)
    nc = t_len // cl
    scale = dh ** -0.5

    qb = q_ref[...]
    qf = qb.astype(F32)
    kf = k_ref[...].astype(F32) * scale
    kb = kf.astype(BF16)

    zg = zg_ref[...]
    zgt = zgt_ref[...]
    lane = _iota2((1, zg.shape[1]), 1)
    sub = _iota2((zgt.shape[0], 1), 0)
    ti = _iota2((cl, cl), 0)
    si = _iota2((cl, cl), 1)

    def rows(c):
        return slice(c * cl, (c + 1) * cl)

    for d in (0, 1):
        gi = d * 2 * n_heads + h
        gf = gi + n_heads
        bi = bias_ref[d * n_heads + h]
        bf = bias_ref[(2 + d) * n_heads + h]
        i_col = jnp.sum(jnp.where(lane == gate_lane0 + gi, zg, 0.0), axis=1, keepdims=True) + bi
        f_col = _log_sigmoid(jnp.sum(jnp.where(lane == gate_lane0 + gf, zg, 0.0), axis=1, keepdims=True) + bf)
        i_row = jnp.sum(jnp.where(sub == gi, zgt, 0.0), axis=0, keepdims=True) + bi
        f_row = _log_sigmoid(jnp.sum(jnp.where(sub == gf, zgt, 0.0), axis=0, keepdims=True) + bf)
        keep = (si <= ti) if d == 0 else (si >= ti)
        keep_t = (ti <= si) if d == 0 else (ti >= si)

        if has_init:
            c_st = c0_ref[0, d, 0]
            n_st = n0_ref[0, d, 0]
            m_st = jnp.full((1, 1), m0_ref[((b * depth + layer) * 2 + d) * n_heads + h], F32)
        else:
            c_st = None
            n_st = None
            m_st = jnp.zeros((1, 1), F32)

        order = range(nc) if d == 0 else range(nc - 1, -1, -1)
        for pos, c in enumerate(order):
            rc = rows(c)
            qc, kc, vc = qb[rc], kb[rc], v_ref[rc, :]
            ic = i_col[rc]
            ir, fr = i_row[:, rc], f_row[:, rc]
            f_cum_col = jnp.sum(jnp.where(keep, fr, 0.0), axis=1, keepdims=True)
            f_cum_row = jnp.sum(jnp.where(keep_t, f_col[rc], 0.0), axis=0, keepdims=True)
            dmat = jnp.where(keep, f_cum_col + (ir - f_cum_row), -jnp.inf)
            g = f_cum_col + m_st
            m_t = jnp.maximum(g, jnp.max(dmat, axis=1, keepdims=True))
            p = _nt(qc, kc) * jnp.exp(dmat - m_t)
            num = _dot(p.astype(BF16), vc)
            den = jnp.sum(p, axis=1, keepdims=True)
            if c_st is not None:
                w_state = jnp.exp(g - m_t)
                num = num + w_state * _dot(qc, c_st.astype(BF16))
                den = den + w_state * jnp.sum(qf[rc] * n_st, axis=1, keepdims=True)
            hh = num / jnp.maximum(jnp.abs(den), jnp.exp(-m_t))
            if d == 0:
                o_ref[rc, :] = hh
            else:
                o_ref[rc, :] += hh
            if pos < nc - 1 or want_final:
                f_last = jnp.sum(fr, axis=1, keepdims=True)
                src = f_last - f_cum_col + ic
                m_new = jnp.maximum(f_last + m_st, jnp.max(src, axis=0, keepdims=True))
                kw = kf[rc] * jnp.exp(src - m_new)
                upd_c = _tn(kw.astype(BF16), vc)
                upd_n = jnp.sum(kw, axis=0, keepdims=True)
                if c_st is not None:
                    w_keep = jnp.exp(f_last + m_st - m_new)
                    c_st = w_keep * c_st + upd_c
                    n_st = w_keep * n_st + upd_n
                else:
                    c_st, n_st = upd_c, upd_n
                m_st = m_new
        if want_final:
            cf_ref[0, d, 0] = c_st
            nf_ref[0, d, 0] = n_st
            mf_ref[0, d, 0] = jnp.broadcast_to(m_st, (1, mf_ref.shape[-1]))
    y = _head_norm_gate(o_ref[...], normg_ref, og_ref[...].astype(F32), jax.nn.sigmoid)
    y_ref[...] = y.astype(y_ref.dtype)


def _ml_mixer(z, zg, zg_block, zgt, bias, normg, init, layer, depth, gate_lane0, geom, cols, ctx):
    nb, t_len, blk0 = _seq_geometry(geom, ctx)
    mix, heads = geom["mix"], geom["ml_heads"]
    dh = mix // heads
    has_init, want_final = init is not None, ctx
    smem = pl.BlockSpec(memory_space=pltpu.SMEM)
    in_specs, args = [smem], [bias]
    if has_init:
        in_specs.append(smem)
        args.append(init["m"])
    in_specs += [
        _col_spec(t_len, dh, cols["ml_q"], blk0),
        _col_spec(t_len, dh, cols["ml_k"], blk0),
        _col_spec(t_len, dh, cols["ml_v"], blk0),
        _col_spec(t_len, dh, cols["ml_o"], blk0),
        pl.BlockSpec((t_len, LANES), lambda b, h: (blk0 + b, zg_block)),
        pl.BlockSpec((zgt.shape[0], t_len), lambda b, h: (0, blk0 + b)),
        pl.BlockSpec((1, dh), lambda b, h: (0, h)),
    ]
    args += [z, z, z, z, zg, zgt, normg]
    if has_init:
        in_specs += [pl.BlockSpec((1, 2, 1, dh, dh), lambda b, h: (b, 0, h, 0, 0)),
                     pl.BlockSpec((1, 2, 1, 1, dh), lambda b, h: (b, 0, h, 0, 0))]
        args += [init["c"], init["n"]]
    x_specs, x_shapes = [], []
    if want_final:
        x_specs = [pl.BlockSpec((1, 2, 1, dh, dh), lambda b, h: (b, 0, h, 0, 0)),
                   pl.BlockSpec((1, 2, 1, 1, dh), lambda b, h: (b, 0, h, 0, 0)),
                   pl.BlockSpec((1, 2, 1, 1, LANES), lambda b, h: (b, 0, h, 0, 0))]
        x_shapes = [jax.ShapeDtypeStruct((nb, 2, heads, dh, dh), F32),
                    jax.ShapeDtypeStruct((nb, 2, heads, 1, dh), F32),
                    jax.ShapeDtypeStruct((nb, 2, heads, 1, LANES), F32)]
    res = _launch_mixer(_ml_kernel, "mlstm_ctx" if ctx else "mlstm_lat", geom, ctx, dh, in_specs, args,
                        x_specs, x_shapes, [pltpu.VMEM((t_len, dh), F32)],
                        t_len=t_len, dh=dh, n_heads=heads, layer=layer, depth=depth, gate_lane0=gate_lane0,
                        has_init=has_init, want_final=want_final)
    return (res[0], res[1:]) if want_final else (res[0], None)


def _merge_kernel(*refs, nbr, ctx_tiles):
    h_ref = refs[0]
    y_refs = refs[1:1 + 2 * nbr]
    m_refs = refs[1 + 2 * nbr:1 + 3 * nbr]
    wb_ref, o_ref = refs[1 + 3 * nbr], refs[2 + 3 * nbr]
    is_ctx = pl.program_id(0) < ctx_tiles
    h = h_ref[...]
    acc = None
    for i in range(nbr):
        y = jnp.where(is_ctx, y_refs[2 * i][...], y_refs[2 * i + 1][...])
        term = jax.nn.sigmoid(_dot(h, m_refs[i][...])) * _dot(y, wb_ref[i].astype(y.dtype))
        acc = term if acc is None else acc + term
    o_ref[...] = acc.astype(o_ref.dtype)


def _merge(h, y_pairs, w_merge, w_branch, layer, geom):
    n, d = h.shape
    mix = geom["mix"]
    nbr = len(y_pairs)
    tm = _pick(math.gcd(geom["n_ctx"], n - geom["n_ctx"]), (512, 256, 128))
    tn = _pick(d, (256, 128))
    nj = d // tn
    ctx_tiles = geom["n_ctx"] // tm
    in_specs = [pl.BlockSpec((tm, d), lambda i, j: (i, 0))]
    for _ in range(nbr):
        in_specs.append(pl.BlockSpec((tm, mix), lambda i, j: (jnp.minimum(i, ctx_tiles - 1), 0)))
        in_specs.append(pl.BlockSpec((tm, mix), lambda i, j: (jnp.maximum(i - ctx_tiles, 0), 0)))
    in_specs += [pl.BlockSpec((d, tn), lambda i, j, br=br: (0, br * nj + j)) for br in range(nbr)]
    in_specs += [pl.BlockSpec((None, nbr, mix, tn), lambda i, j: (layer, 0, 0, j))]
    flat = [y for pair in y_pairs for y in pair]
    return pl.pallas_call(
        functools.partial(_merge_kernel, nbr=nbr, ctx_tiles=ctx_tiles),
        grid=(n // tm, nj),
        in_specs=in_specs,
        out_specs=pl.BlockSpec((tm, tn), lambda i, j: (i, j)),
        out_shape=jax.ShapeDtypeStruct((n, d), BF16),
        compiler_params=_cparams(("arbitrary", "arbitrary")),
        name="merge",
    )(h, *flat, *([w_merge] * nbr), w_branch)


def _outproj_kernel(*refs, n_x, ctx_tiles):
    m_ref, w_ref = refs[:2]
    g_ref, o_ref = refs[2 + n_x:]
    x = _read_rows(refs[2:2 + n_x], ctx_tiles)
    m = m_ref[...]
    o_ref[...] = x + g_ref[0, 0] * _dot(m, w_ref[...].astype(m.dtype))


def _outproj(merged, w_out, layer, x, mod4, comp_gate, geom):
    n, d = merged.shape
    tm = _pick(math.gcd(geom["n_ctx"], geom["dec_seq"]), (1024, 512, 256, 128))
    tn = _pick(d, (512, 256, 128))
    n_ctx, dec_seq, ctx_row = geom["n_ctx"], geom["dec_seq"], geom["ctx_row"]
    ctx_tiles = n_ctx // tm

    def gate_map(i, j):
        start = i * tm
        return (jnp.where(start < n_ctx, ctx_row, (start - n_ctx) // dec_seq), comp_gate, 0, j)

    x_specs, x_args = _row_specs(x, (tm, tn), ctx_tiles, col=lambda i, j: j)
    return pl.pallas_call(
        functools.partial(_outproj_kernel, n_x=len(x_args), ctx_tiles=ctx_tiles),
        grid=(n // tm, d // tn),
        in_specs=[
            pl.BlockSpec((tm, d), lambda i, j: (i, 0)),
            pl.BlockSpec((None, d, tn), lambda i, j: (layer, 0, j)),
        ] + x_specs + [pl.BlockSpec((1, 1, 1, tn), gate_map)],
        out_specs=pl.BlockSpec((tm, tn), lambda i, j: (i, j)),
        out_shape=jax.ShapeDtypeStruct((n, d), F32),
        compiler_params=_cparams(("arbitrary", "arbitrary")),
        name="out_proj",
    )(merged, w_out, *x_args, mod4)


def _first_argmax(vals):
    best, idx = vals[0], jnp.zeros_like(vals[0])
    for j in range(1, len(vals)):
        better = vals[j] > best
        idx = jnp.where(better, float(j), idx)
        best = jnp.where(better, vals[j], best)
    return idx, best


def _pick_row(vals, idx):
    out = vals[0]
    for j in range(1, len(vals)):
        out = jnp.where(idx == float(j), vals[j], out)
    return out


def _router_kernel(x_ref, g_ref, sh_ref, sc_ref, wrt_ref, rb_ref, h_ref, eidx_ref, wgt_ref, rank_ref, cnt_ref,
                   carry_ref, *, n_exp, n_groups):
    i = pl.program_id(0)
    x = x_ref[...]
    y = x * lax.rsqrt(jnp.mean(x * x, axis=-1, keepdims=True) + EPS)
    h = (y * g_ref[...]) * (1.0 + sc_ref[0, 0]) + sh_ref[0, 0]
    h_ref[...] = _pack_bf16_pairs(h)
    tm = x.shape[0]
    per = n_exp // n_groups

    logits = _dot3(wrt_ref[...], h, _nt)
    scores = jax.nn.sigmoid(logits)
    sel = scores + rb_ref[...]
    sel_rows = [sel[e:e + 1, :] for e in range(n_exp)]
    sc_rows = [scores[e:e + 1, :] for e in range(n_exp)]
    group_scores = []
    for g in range(n_groups):
        r = sel_rows[g * per:(g + 1) * per]
        pair = None
        for a in range(per):
            for bb in range(a + 1, per):
                s2 = r[a] + r[bb]
                pair = s2 if pair is None else jnp.maximum(pair, s2)
        group_scores.append(pair)
    g_best, _ = _first_argmax(group_scores)
    in_sel = [_pick_row([sel_rows[g * per + j] for g in range(n_groups)], g_best) for j in range(per)]
    in_sc = [_pick_row([sc_rows[g * per + j] for g in range(n_groups)], g_best) for j in range(per)]
    j1, _ = _first_argmax(in_sel)
    masked = [jnp.where(j1 == float(j), -jnp.inf, in_sel[j]) for j in range(per)]
    j2, _ = _first_argmax(masked)
    w1 = _pick_row(in_sc, j1)
    w2 = _pick_row(in_sc, j2)
    tot = w1 + w2
    e1 = (g_best * per + j1).astype(jnp.int32)
    e2 = (g_best * per + j2).astype(jnp.int32)
    eidx_ref[...] = jnp.concatenate([e1, e2], axis=0)
    wgt_ref[...] = jnp.concatenate([w1 / tot, w2 / tot], axis=0)

    @pl.when(i == 0)
    def _():
        carry_ref[...] = jnp.zeros_like(carry_ref)

    eio = _iota2((n_exp, tm), 0)
    hit1 = eio == e1
    hit2 = eio == e2
    onehot = jnp.where(hit1 | hit2, 1.0, 0.0)
    before = (_iota2((tm, tm), 0) < _iota2((tm, tm), 1)).astype(BF16)
    prefix = _dot(onehot.astype(BF16), before) + carry_ref[:, 0:1]
    r1 = jnp.sum(jnp.where(hit1, prefix, 0.0), axis=0, keepdims=True)
    r2 = jnp.sum(jnp.where(hit2, prefix, 0.0), axis=0, keepdims=True)
    rank_ref[...] = jnp.concatenate([r1, r2], axis=0).astype(jnp.int32)
    carry_ref[...] = carry_ref[...] + jnp.sum(onehot, axis=1, keepdims=True)
    cnt_ref[...] = carry_ref[...]


def _norm_router(x, g, mod4, comp_shift, comp_scale, w_router_t, router_bias, geom):
    n, d = x.shape
    n_exp = w_router_t.shape[0]
    tm = 256
    row2 = pl.BlockSpec((2, tm), lambda i: (0, i))
    return pl.pallas_call(
        functools.partial(_router_kernel, n_exp=n_exp, n_groups=N_EXPERT_GROUPS),
        grid=(n // tm,),
        in_specs=[
            pl.BlockSpec((tm, d), lambda i: (i, 0)),
            pl.BlockSpec((1, d), lambda i: (0, 0)),
            _mod_spec(d, tm, geom, comp_shift),
            _mod_spec(d, tm, geom, comp_scale),
            pl.BlockSpec((n_exp, d), lambda i: (0, 0)),
            pl.BlockSpec((n_exp, 1), lambda i: (0, 0)),
        ],
        out_specs=[pl.BlockSpec((tm, d // 2), lambda i: (i, 0)), row2, row2, row2,
                   pl.BlockSpec((n_exp, LANES), lambda i: (0, 0))],
        out_shape=[jax.ShapeDtypeStruct((n, d // 2), jnp.uint32),
                   jax.ShapeDtypeStruct((2, n), jnp.int32),
                   jax.ShapeDtypeStruct((2, n), F32),
                   jax.ShapeDtypeStruct((2, n), jnp.int32),
                   jax.ShapeDtypeStruct((n_exp, LANES), F32)],
        scratch_shapes=[pltpu.VMEM((n_exp, LANES), F32)],
        compiler_params=_cparams(("arbitrary",)),
        name="norm_router",
    )(x, g.reshape(1, d), mod4, mod4, w_router_t, router_bias.reshape(n_exp, 1))


def _row_copy(src_hbm, row, dst_vmem, r, sem):
    return pltpu.make_async_copy(src_hbm.at[pl.ds(row, 1), :], dst_vmem.at[pl.ds(r, 1), :], sem)


def _dispatch_kernel(src_ref, h_hbm, o_ref, sem):
    tg = o_ref.shape[0]

    def start(i, c):
        for u in range(2):
            r = 2 * i + u
            _row_copy(h_hbm, src_ref[0, 0, r], o_ref, r, sem).start(priority=u)
        return c

    def wait(r, c):
        _row_copy(h_hbm, 0, o_ref, r, sem).wait()
        return c

    lax.fori_loop(0, tg // 2, start, 0, unroll=8)
    lax.fori_loop(0, tg, wait, 0, unroll=16)


def _dispatch(h, src):
    p_rows = src.shape[0]
    dw = h.shape[1]
    tg = _pick(p_rows, (8 * ROUTE_TILE, 4 * ROUTE_TILE, 2 * ROUTE_TILE, ROUTE_TILE))
    return pl.pallas_call(
        _dispatch_kernel,
        grid=(p_rows // tg,),
        in_specs=[
            pl.BlockSpec((1, 1, tg), lambda i: (i, 0, 0), memory_space=pltpu.SMEM),
            pl.BlockSpec(memory_space=pl.ANY),
        ],
        out_specs=pl.BlockSpec((tg, dw), lambda i: (i, 0)),
        out_shape=jax.ShapeDtypeStruct((p_rows, dw), h.dtype),
        scratch_shapes=[pltpu.SemaphoreType.DMA(())],
        compiler_params=_cparams(("arbitrary",)),
        name="moe_dispatch",
    )(src.reshape(p_rows // tg, 1, tg), h)


def _new_expert(te_ref, t):
    return (t == 0) | (te_ref[t] != te_ref[jnp.maximum(t - 1, 0)])


def _ffn_up_kernel(te_ref, tv_ref, x_ref, wg_ref, wu_ref, o_ref, wgb_ref, wub_ref):
    t = pl.program_id(1)

    @pl.when(_new_expert(te_ref, t))
    def _():
        wgb_ref[...] = wg_ref[0].astype(BF16)
        wub_ref[...] = wu_ref[0].astype(BF16)

    @pl.when(tv_ref[t] == 1)
    def _():
        lo, hi = _unpack_bf16_pairs(x_ref[...])
        lo, hi = lo.astype(BF16), hi.astype(BF16)
        half = lo.shape[1]
        g = _dot(lo, wgb_ref[:half, :]) + _dot(hi, wgb_ref[half:, :])
        u = _dot(lo, wub_ref[:half, :]) + _dot(hi, wub_ref[half:, :])
        o_ref[...] = (_silu(g) * u).astype(o_ref.dtype)

    @pl.when(tv_ref[t] == 0)
    def _():
        o_ref[...] = jnp.zeros_like(o_ref)


def _ffn_up(xs, w_gate, w_up, layer, tile_expert, tile_valid):
    p_rows, dw = xs.shape
    d, f = w_gate.shape[2], w_gate.shape[3]
    tm = ROUTE_TILE
    tf = _pick(f, (512, 256, 128))
    w_spec = pl.BlockSpec((None, 1, d, tf), lambda j, t, te, tv: (layer, te[t], 0, j))
    grid_spec = pltpu.PrefetchScalarGridSpec(
        num_scalar_prefetch=2,
        grid=(f // tf, p_rows // tm),
        in_specs=[pl.BlockSpec((tm, dw), lambda j, t, te, tv: (t, 0)), w_spec, w_spec],
        out_specs=pl.BlockSpec((tm, tf), lambda j, t, te, tv: (t, j)),
        scratch_shapes=[pltpu.VMEM((d, tf), BF16), pltpu.VMEM((d, tf), BF16)],
    )
    return pl.pallas_call(
        _ffn_up_kernel,
        grid_spec=grid_spec,
        out_shape=jax.ShapeDtypeStruct((p_rows, f), BF16),
        compiler_params=_cparams(("arbitrary", "arbitrary")),
        name="moe_ffn_up",
    )(tile_expert, tile_valid, xs, w_gate, w_up)


def _ffn_down_kernel(te_ref, tv_ref, h_ref, wd_ref, o_ref, wdb_ref):
    t = pl.program_id(1)

    @pl.when(_new_expert(te_ref, t))
    def _():
        wdb_ref[...] = wd_ref[0].astype(BF16)

    @pl.when(tv_ref[t] == 1)
    def _():
        o_ref[...] = _pack_bf16_pairs(_dot(h_ref[...], wdb_ref[...]))

    @pl.when(tv_ref[t] == 0)
    def _():
        o_ref[...] = jnp.zeros_like(o_ref)


def _ffn_down_tile(d):
    return _pick(d, (4096, 2048, 1024, 512, 256))


def _ffn_down(hmid, w_down, layer, tile_expert, tile_valid):
    p_rows, f = hmid.shape
    d = w_down.shape[3]
    tm = ROUTE_TILE
    tn = _ffn_down_tile(d)
    grid_spec = pltpu.PrefetchScalarGridSpec(
        num_scalar_prefetch=2,
        grid=(d // tn, p_rows // tm),
        in_specs=[
            pl.BlockSpec((tm, f), lambda j, t, te, tv: (t, 0)),
            pl.BlockSpec((None, 1, f, tn), lambda j, t, te, tv: (layer, te[t], 0, j)),
        ],
        out_specs=pl.BlockSpec((tm, tn // 2), lambda j, t, te, tv: (t, j)),
        scratch_shapes=[pltpu.VMEM((f, tn), BF16)],
    )
    return pl.pallas_call(
        _ffn_down_kernel,
        grid_spec=grid_spec,
        out_shape=jax.ShapeDtypeStruct((p_rows, d // 2), jnp.uint32),
        compiler_params=_cparams(("arbitrary", "arbitrary")),
        name="moe_ffn_down",
    )(tile_expert, tile_valid, hmid, w_down)


def _combine_kernel(*refs, final_norm, ctx_tiles, tn):
    pos_ref, x_ref, wt_ref, g_ref, fg_ref, ys_hbm = refs[:6]
    out_refs = refs[6:-2]
    buf_ref, sem = refs[-2:]
    tm = x_ref.shape[0]

    def start(r, c):
        for s in range(2):
            _row_copy(ys_hbm, pos_ref[0, s, r], buf_ref.at[s], r, sem).start(priority=s)
        return c

    def wait(r, c):
        for s in range(2):
            _row_copy(ys_hbm, 0, buf_ref.at[s], r, sem).wait()
        return c

    lax.fori_loop(0, tm, start, 0, unroll=4)
    lax.fori_loop(0, tm, wait, 0, unroll=4)

    d = x_ref.shape[1]
    rg = 2 * SUBLANES
    cw = min(2 * LANES, tn // 2)
    gate = g_ref[0, 0]
    lane = _iota2((1, wt_ref.shape[1]), 1)
    pieces = [(j * tn // 2 + p, j * tn + p, j * tn + tn // 2 + p)
              for j in range(d // tn) for p in range(0, tn // 2, cw)]

    def fill_normed(o_ref):
        w = wt_ref[...]
        w0 = jnp.sum(jnp.where(lane == 0, w, 0.0), axis=1, keepdims=True)
        w1 = jnp.sum(jnp.where(lane == 1, w, 0.0), axis=1, keepdims=True)
        parts = []
        for j in range(d // tn):
            seg = slice(j * tn // 2, (j + 1) * tn // 2)
            lo0, hi0 = _unpack_bf16_pairs(buf_ref[0, :, seg])
            lo1, hi1 = _unpack_bf16_pairs(buf_ref[1, :, seg])
            parts += [w0 * lo0 + w1 * lo1, w0 * hi0 + w1 * hi1]
        x = x_ref[...] + gate * jnp.concatenate(parts, axis=1)
        o_ref[...] = (x * lax.rsqrt(jnp.mean(x * x, axis=-1, keepdims=True) + EPS)) * fg_ref[...]

    def fill(o_ref):
        if final_norm:
            return fill_normed(o_ref)

        def rows_body(gi, c):
            rows = pl.ds(pl.multiple_of(gi * rg, rg), rg)
            w = wt_ref[rows, :]
            w0 = jnp.sum(jnp.where(lane == 0, w, 0.0), axis=1, keepdims=True)
            w1 = jnp.sum(jnp.where(lane == 1, w, 0.0), axis=1, keepdims=True)
            for p0, c_lo, c_hi in pieces:
                lo0, hi0 = _unpack_bf16_pairs(buf_ref[0, rows, p0:p0 + cw])
                lo1, hi1 = _unpack_bf16_pairs(buf_ref[1, rows, p0:p0 + cw])
                for col, moe in ((c_lo, w0 * lo0 + w1 * lo1), (c_hi, w0 * hi0 + w1 * hi1)):
                    o_ref[rows, col:col + cw] = x_ref[rows, col:col + cw] + gate[:, col:col + cw] * moe
            return c

        lax.fori_loop(0, tm // rg, rows_body, 0)

    if len(out_refs) == 1:
        fill(out_refs[0])
    else:
        is_ctx = pl.program_id(0) < ctx_tiles
        pl.when(is_ctx)(lambda: fill(out_refs[0]))
        pl.when(jnp.logical_not(is_ctx))(lambda: fill(out_refs[1]))


def _combine(x, ys, pos, wgt_cols, mod4, comp_gate, final_g, geom, final_norm, split_out):
    n, d = x.shape
    tm = pos.shape[2]
    ctx_tiles = geom["n_ctx"] // tm
    if split_out:
        out_specs = [pl.BlockSpec((tm, d), lambda i: (jnp.minimum(i, ctx_tiles - 1), 0)),
                     pl.BlockSpec((tm, d), lambda i: (jnp.maximum(i - ctx_tiles, 0), 0))]
        out_shape = [jax.ShapeDtypeStruct((geom["n_ctx"], d), F32),
                     jax.ShapeDtypeStruct((n - geom["n_ctx"], d), F32)]
    else:
        out_specs = pl.BlockSpec((tm, d), lambda i: (i, 0))
        out_shape = jax.ShapeDtypeStruct((n, d), F32)
    return pl.pallas_call(
        functools.partial(_combine_kernel, final_norm=final_norm, ctx_tiles=ctx_tiles, tn=_ffn_down_tile(d)),
        grid=(n // tm,),
        in_specs=[
            pl.BlockSpec((1, 2, tm), lambda i: (i, 0, 0), memory_space=pltpu.SMEM),
            pl.BlockSpec((tm, d), lambda i: (i, 0)),
            pl.BlockSpec((tm, wgt_cols.shape[1]), lambda i: (i, 0)),
            _mod_spec(d, tm, geom, comp_gate),
            pl.BlockSpec((1, d), lambda i: (0, 0)),
            pl.BlockSpec(memory_space=pl.ANY),
        ],
        out_specs=out_specs,
        out_shape=out_shape,
        scratch_shapes=[pltpu.VMEM((2, tm, d // 2), jnp.uint32), pltpu.SemaphoreType.DMA(())],
        compiler_params=_cparams(("arbitrary",)),
        name="moe_combine",
    )(pos, x, wgt_cols, mod4, final_g.reshape(1, d), ys)


def _route_plan(eidx, rank, counts, n_exp, combine_tile):
    n = eidx.shape[1]
    tile = ROUTE_TILE
    n_tiles = (2 * n) // tile + n_exp
    cnt = counts.astype(jnp.int32)
    padded = ((cnt + tile - 1) // tile) * tile
    e_ids = jnp.arange(n_exp, dtype=jnp.int32)
    ends = jnp.sum(jnp.where(e_ids[None, :] <= e_ids[:, None], padded[None, :], 0), axis=1)
    offs = ends - padded
    pos = jnp.sum(jnp.where(eidx[:, :, None] == e_ids, offs, 0), axis=-1) + rank
    tile_start = jnp.arange(n_tiles, dtype=jnp.int32) * tile
    tile_expert = jnp.minimum(jnp.sum((ends[None, :] <= tile_start[:, None]).astype(jnp.int32), axis=1), n_exp - 1)
    tile_valid = (tile_start < ends[-1]).astype(jnp.int32)
    tok = jnp.broadcast_to(jnp.arange(n, dtype=jnp.int32), (2, n))
    src = jnp.zeros((n_tiles * tile,), jnp.int32).at[pos.reshape(-1)].set(tok.reshape(-1))
    pos_tiles = pos.reshape(2, n // combine_tile, combine_tile).transpose(1, 0, 2)
    return pos_tiles, src, tile_expert, tile_valid


def _s5_params(lam_re, lam_im, log_step, b_re, b_im, c_re, c_im):
    n_dir, groups, p = lam_re.shape
    gch = b_re.shape[-1]
    gps = S5_SLAB // gch
    n_slab = groups // gps
    step = jnp.exp(log_step)[..., None]
    mag = jnp.exp(lam_re * step)
    a_re = mag * jnp.cos(lam_im * step)
    a_im = mag * jnp.sin(lam_im * step)
    den = lam_re * lam_re + lam_im * lam_im
    z_re = ((a_re - 1.0) * lam_re + a_im * lam_im) / den
    z_im = (a_im * lam_re - (a_re - 1.0) * lam_im) / den
    bb_re = z_re[..., None] * b_re - z_im[..., None] * b_im
    bb_im = z_re[..., None] * b_im + z_im[..., None] * b_re
    eye = jnp.eye(gps, dtype=F32)
    bbs = jnp.stack([bb_re, bb_im], axis=2).reshape(n_dir, n_slab, gps, 2, p, gch)
    bm = jnp.einsum("dsgrpc,gh->dsgcrhp", bbs, eye).reshape(n_dir, n_slab, S5_SLAB, 2 * gps * p)
    cs = jnp.stack([c_re, -c_im], axis=0).reshape(2, n_slab, gps, gch, p)
    cm = jnp.einsum("rsgcp,gh->srgphc", cs, eye).reshape(n_slab, 2 * gps * p, S5_SLAB)
    a = jnp.stack([a_re, a_im], axis=2).reshape(n_dir, n_slab, gps, 2, p).transpose(0, 1, 3, 2, 4)
    a = a.reshape(n_dir, n_slab, 1, 2 * gps * p)
    a8 = jnp.broadcast_to(a, (n_dir, n_slab, SUBLANES, 2 * gps * p))
    return bm.astype(BF16), cm.astype(BF16), a8


def _axial_rope(t, dh):
    rows = t // GRID_W
    row = jnp.repeat(jnp.arange(rows, dtype=F32), GRID_W)
    col = (jnp.arange(rows * GRID_W) % GRID_W).astype(F32)
    n_freq = dh // 4
    inv = ROPE_BASE ** (-jnp.arange(n_freq, dtype=F32) / n_freq)
    ang = jnp.concatenate([row[:, None] * inv, col[:, None] * inv], axis=-1)
    return jnp.cos(ang), jnp.sin(ang)


def _to_time_major(u, nb, t_len, bp):
    u = u.reshape(nb, t_len, -1).transpose(1, 0, 2)
    if bp != nb:
        u = jnp.pad(u, ((0, 0), (0, bp - nb), (0, 0)))
    return u.reshape(t_len * bp, -1)


def _from_time_major(y, nb, t_len, bp):
    return y.reshape(t_len, bp, -1)[:, :nb].transpose(1, 0, 2).reshape(nb * t_len, -1)


def _s5_state_to_slabs(s_re, s_im, n_slab, bp):
    nb = s_re.shape[0]
    st = jnp.concatenate([s_re.reshape(nb, 2, n_slab, -1), s_im.reshape(nb, 2, n_slab, -1)], axis=-1)
    st = st.transpose(1, 2, 0, 3)
    return jnp.pad(st, ((0, 0), (0, 0), (0, bp - nb), (0, 0)))


def _s5_slabs_to_state(xf, nb, groups, p):
    half = xf.shape[-1] // 2
    re = xf[:, :, :nb, :half].transpose(2, 0, 1, 3).reshape(nb, 2, groups, p)
    im = xf[:, :, :nb, half:].transpose(2, 0, 1, 3).reshape(nb, 2, groups, p)
    return re, im


def _round_up(x, m):
    return (x + m - 1) // m * m


def kernel(x_prompt, x_sample, c, c_ctx, state_s5_re, state_s5_im, state_gla, state_ml_c, state_ml_n, state_ml_m, state_ret, w_ada, b_ada, norm1_g, norm2_g, w_in, s5_lambda_re, s5_lambda_im, s5_log_step, s5_b_re, s5_b_im, s5_c_re, s5_c_im, s5_d, s5_w_glu, gla_w_a, gla_b_a, gla_norm_g, ml_i_bias, ml_f_bias, ml_norm_g, ret_decay_logit, ret_norm_g, w_branch, w_out, w_router, router_bias, w_exp_gate, w_exp_up, w_exp_down, final_g):
    batch, seq, d = x_prompt.shape
    dec_batch, dec_seq, _ = x_sample.shape
    depth = w_in.shape[0]
    mix = s5_d.shape[-1]
    n_exp = w_router.shape[1]
    gla_heads = state_gla.shape[3]
    ml_heads = ml_i_bias.shape[-1]
    ret_heads = ret_decay_logit.shape[-1]
    gla_rank = gla_w_a.shape[2]
    groups, p_state = s5_lambda_re.shape[2], s5_lambda_re.shape[3]
    n_ctx, n_lat = batch * seq, dec_batch * dec_seq
    n = n_ctx + n_lat
    assert dec_batch + 1 <= SUBLANES and n_ctx % dec_seq == 0 and dec_seq % seq == 0
    geom = dict(batch=batch, seq=seq, dec_batch=dec_batch, dec_seq=dec_seq, n_ctx=n_ctx, n=n, ctx_row=dec_batch,
                mix=mix, gla_heads=gla_heads, ml_heads=ml_heads, ret_heads=ret_heads)
    n_slab = mix // S5_SLAB
    bp_ctx, bp_lat = _round_up(batch, SUBLANES), _round_up(dec_batch, SUBLANES)

    half = mix // 2
    widths = [("s5_u", mix), ("gla_q", half), ("gla_k", half), ("gla_v", mix), ("gla_g", mix), ("gla_r", gla_rank),
              ("ml_q", mix), ("ml_k", mix), ("ml_v", mix), ("ml_o", mix), ("ml_if", 4 * ml_heads),
              ("ret_q", mix), ("ret_k", mix), ("ret_v", mix), ("ret_g", mix), ("merge", 4 * d)]
    src_off, o = {}, 0
    for name, w in widths:
        src_off[name] = (o, w)
        o += w
    main_names = [nm for nm, _ in widths if nm not in ("gla_r", "ml_if", "merge")]
    cols, o = {}, 0
    for nm in main_names:
        cols[nm] = o
        o += src_off[nm][1]

    repack_tn = _pick(math.gcd(half, 4 * d), (512, 256, 128))
    main_starts = [src_off[nm][0] + off for nm in main_names for off in range(0, src_off[nm][1], repack_tn)]
    merge_starts = [src_off["merge"][0] + off for off in range(0, 4 * d, repack_tn)]
    gate_starts = [src_off["gla_r"][0], src_off["ml_if"][0]]
    w_in_t = jnp.swapaxes(w_in, 1, 2)

    x = (x_prompt.reshape(n_ctx, d), x_sample.reshape(n_lat, d))
    c8 = jnp.zeros((SUBLANES, d), F32).at[:dec_batch].set(c).at[dec_batch].set(c_ctx)
    mod = _ada(c8, w_ada, b_ada)
    rope_tabs = _axial_rope(dec_seq, mix // ret_heads)
    w_router_t = w_router.T

    ctx_states = []
    for l in range(depth):
        mod4 = mod[l].reshape(SUBLANES, N_MOD, 1, d)
        w_main = _repack_proj(w_in_t, l, main_starts, repack_tn)
        w_merge = _repack_proj(w_in_t, l, merge_starts, repack_tn)
        w_gate = _repack_proj(w_in_t, l, gate_starts, LANES)

        h = _norm_mod(x, norm1_g[l], mod4, 0, 1, geom, BF16)
        z = _matmul(h, w_main, BF16, _pick(n, (1024, 512, 256)), _pick(w_main.shape[1], (512, 256, 128)), "in_proj")
        zg = _matmul(h, w_gate, F32, _pick(n, (1024, 512, 256)), 2 * LANES, "gate_proj")
        zgt = zg[:, LANES:LANES + 4 * ml_heads].T

        bm, cm, a8 = _s5_params(s5_lambda_re[l], s5_lambda_im[l], s5_log_step[l], s5_b_re[l], s5_b_im[l],
                                s5_c_re[l], s5_c_im[l])
        dvec = s5_d[l].reshape(1, mix)
        u = z[:, :mix]
        y_c, xf = _s5_mixer(_to_time_major(u[:n_ctx], batch, seq, bp_ctx), bm, cm, a8, dvec, None, seq, bp_ctx, True)
        x0 = _s5_state_to_slabs(state_s5_re[:, l], state_s5_im[:, l], n_slab, bp_lat)
        y_l, _ = _s5_mixer(_to_time_major(u[n_ctx:], dec_batch, dec_seq, bp_lat), bm, cm, a8, dvec, x0, dec_seq,
                           bp_lat, False)
        w_glu = s5_w_glu[l].astype(BF16)
        y_s5 = (_glu(_from_time_major(y_c, batch, seq, bp_ctx), w_glu),
                _glu(_from_time_major(y_l, dec_batch, dec_seq, bp_lat), w_glu))
        s5_re_l, s5_im_l = _s5_slabs_to_state(xf, batch, groups, p_state)

        dk = half // gla_heads
        wa = gla_w_a[l].reshape(2, gla_rank, gla_heads, dk).transpose(0, 2, 1, 3)
        wa = jnp.pad(wa, ((0, 0), (0, 0), (0, LANES - gla_rank), (0, 0)))
        ba = gla_b_a[l].reshape(2, gla_heads, 1, dk)
        gp = dict(wa=wa, ba=ba, normg=gla_norm_g[l].reshape(1, mix))
        y_gla_c, gla_l = _gla_mixer(z, zg, gp, None, geom, cols, True)
        y_gla_l, _ = _gla_mixer(z, zg, gp, state_gla[:, l], geom, cols, False)

        ml_bias = jnp.stack([ml_i_bias[l], ml_f_bias[l]], axis=0).reshape(-1)
        ml_ng = ml_norm_g[l].reshape(1, mix)
        y_ml_c, ml_l = _ml_mixer(z, zg, 1, zgt, ml_bias, ml_ng, None, l, depth, 0, geom, cols, True)
        ml_init = dict(c=state_ml_c[:, l], n=state_ml_n[:, l][:, :, :, None, :], m=state_ml_m.reshape(-1))
        y_ml_l, _ = _ml_mixer(z, zg, 1, zgt, ml_bias, ml_ng, ml_init, l, depth, 0, geom, cols, False)

        lg = jax.nn.log_sigmoid(ret_decay_logit[l]).reshape(-1)
        ret_ng = ret_norm_g[l].reshape(1, mix)
        y_ret_c, ret_l = _ret_mixer(z, lg, ret_ng, None, None, geom, cols, True)
        y_ret_l, _ = _ret_mixer(z, lg, ret_ng, rope_tabs, state_ret[:, l], geom, cols, False)

        y_pairs = [y_s5, (y_gla_c, y_gla_l), (y_ml_c, y_ml_l), (y_ret_c, y_ret_l)]
        merged = _merge(h, y_pairs, w_merge, w_branch, l, geom)
        x = _outproj(merged, w_out, l, x, mod4, 2, geom)
        ctx_states.append((s5_re_l, s5_im_l, gla_l, ml_l[0], ml_l[1][:, :, :, 0, :], ml_l[2][:, :, :, 0, 0], ret_l))

        h2, eidx, wgt, rank, counts = _norm_router(x, norm2_g[l], mod4, 3, 4, w_router_t, router_bias, geom)
        last = l == depth - 1
        combine_tile = ROUTE_TILE if last else _pick(math.gcd(n_ctx, n_lat), (2 * ROUTE_TILE, ROUTE_TILE))
        pos_tiles, src, tile_expert, tile_valid = _route_plan(eidx, rank, counts[:, 0], n_exp, combine_tile)
        xs = _dispatch(h2, src)
        hmid = _ffn_up(xs, w_exp_gate, w_exp_up, l, tile_expert, tile_valid)
        ys = _ffn_down(hmid, w_exp_down, l, tile_expert, tile_valid)
        x = _combine(x, ys, pos_tiles, wgt.T, mod4, 5, final_g, geom, final_norm=last, split_out=last)

    y_prompt = x[0].reshape(batch, seq, d)
    y_sample = x[1].reshape(dec_batch, dec_seq, d)
    stacked = [jnp.stack([st[i] for st in ctx_states], axis=1) for i in range(7)]
    return (y_prompt, y_sample, *stacked)
```
